```python
import jax, jax.numpy as jnp
from jax import lax
import numpy as np

D_MODEL = 2048
BATCH = 4
SEQ = 4096
DEPTH = 1

MOBA_HEADS = 8
MOBA_HEAD_DIM = 128
MOBA_BLOCK = 256
MOBA_TOPK = 3
MOBA_Q_CHUNK = 32
SWA_HEADS = 16
SWA_KV_HEADS = 2
SWA_HEAD_DIM = 64
SWA_WINDOW = 128
D_FF = 5632
ROPE_THETA = 10000.0
EPS = 1e-6
N_ADA = 9

MOBA_W = MOBA_HEADS * MOBA_HEAD_DIM
SWA_QW = SWA_HEADS * SWA_HEAD_DIM
SWA_KVW = SWA_KV_HEADS * SWA_HEAD_DIM
IN_COLS = 3 * MOBA_W + SWA_QW + 2 * SWA_KVW + 2 * D_MODEL

kernel_name = "hybrid_moba_swa_sink_macaron_adaln"


def rms_norm(x, g):
    xf = x.astype(jnp.float32)
    y = xf * lax.rsqrt(jnp.mean(xf * xf, axis=-1, keepdims=True) + EPS)
    return (y * g.astype(jnp.float32)).astype(x.dtype)


def modulate(h, shift, scale):
    return h * (1.0 + scale[:, None, :]) + shift[:, None, :]


def rope(x, pos):
    half = x.shape[-1] // 2
    inv = ROPE_THETA ** (-jnp.arange(half, dtype=jnp.float32) / half)
    ang = pos.astype(jnp.float32)[:, None] * inv[None, :]
    cos = jnp.cos(ang)[None, :, None, :]
    sin = jnp.sin(ang)[None, :, None, :]
    x1 = x[..., :half].astype(jnp.float32)
    x2 = x[..., half:].astype(jnp.float32)
    return jnp.concatenate([x1 * cos - x2 * sin, x2 * cos + x1 * sin], axis=-1).astype(x.dtype)


def swiglu(h, w_gate, w_up, w_down):
    return jnp.einsum('bsf,fd->bsd', jax.nn.silu(jnp.einsum('bsd,df->bsf', h, w_gate)) * jnp.einsum('bsd,df->bsf', h, w_up), w_down)


def moba_attention(q, k, v):
    B, S, H, dh = q.shape
    BLK, C = MOBA_BLOCK, MOBA_Q_CHUNK
    nb = -(-S // BLK)
    Sp = nb * BLK
    K = min(MOBA_TOPK, nb)
    pad = ((0, 0), (0, Sp - S), (0, 0), (0, 0))
    qp, kp, vp = [jnp.pad(t, pad).transpose(0, 2, 1, 3) for t in (q, k, v)]
    kb = kp.reshape(B, H, nb, BLK, dh)
    vb = vp.reshape(B, H, nb, BLK, dh)
    kmean = jnp.mean(kb.astype(jnp.float32), axis=3).astype(q.dtype)
    n_chunks = Sp // C
    qc = qp.reshape(B, H, n_chunks, C, dh).transpose(2, 0, 1, 3, 4)
    scale = dh ** -0.5
    neg = jnp.finfo(jnp.float32).min
    b_ix = jnp.arange(B)[:, None, None, None]
    h_ix = jnp.arange(H)[None, :, None, None]

    def chunk_fn(args):
        ci, qi = args
        q_pos = ci * C + jnp.arange(C)
        own = (ci * C) // BLK
        gs = jnp.einsum('bhcd,bhnd->bhcn', qi, kmean, preferred_element_type=jnp.float32)
        gs = jnp.where(jnp.arange(nb) < own, gs, neg)
        _, sel = lax.top_k(gs, K)
        sel_valid = jnp.arange(K) < own
        ksel = kb[b_ix, h_ix, sel]
        vsel = vb[b_ix, h_ix, sel]
        s_sel = jnp.einsum('bhcd,bhckjd->bhckj', qi, ksel, preferred_element_type=jnp.float32) * scale
        s_sel = jnp.where(sel_valid[:, None], s_sel, neg).reshape(B, H, C, K * BLK)
        kown = lax.dynamic_index_in_dim(kb, own, axis=2, keepdims=False)
        vown = lax.dynamic_index_in_dim(vb, own, axis=2, keepdims=False)
        s_own = jnp.einsum('bhcd,bhjd->bhcj', qi, kown, preferred_element_type=jnp.float32) * scale
        k_pos = own * BLK + jnp.arange(BLK)
        s_own = jnp.where(k_pos[None, :] <= q_pos[:, None], s_own, neg)
        p = jax.nn.softmax(jnp.concatenate([s_sel, s_own], axis=-1), axis=-1)
        p_sel = p[..., :K * BLK].reshape(B, H, C, K, BLK).astype(vb.dtype)
        p_own = p[..., K * BLK:].astype(vb.dtype)
        out = jnp.einsum('bhckj,bhckjd->bhcd', p_sel, vsel) + jnp.einsum('bhcj,bhjd->bhcd', p_own, vown)
        return out.astype(q.dtype)

    outs = lax.map(chunk_fn, (jnp.arange(n_chunks), qc))
    return outs.transpose(1, 0, 3, 2, 4).reshape(B, Sp, H, dh)[:, :S]


def sliding_window_attention(q, k, v, sinks):
    B, S, Hq, dh = q.shape
    Hkv = k.shape[2]
    G = Hq // Hkv
    W = SWA_WINDOW
    nb = S // W
    qb = q.reshape(B, nb, W, Hkv, G, dh)
    kb = k.reshape(B, nb, W, Hkv, dh)
    vb = v.reshape(B, nb, W, Hkv, dh)
    shift = ((0, 0), (1, 0), (0, 0), (0, 0), (0, 0))
    kcat = jnp.concatenate([jnp.pad(kb, shift)[:, :-1], kb], axis=2)
    vcat = jnp.concatenate([jnp.pad(vb, shift)[:, :-1], vb], axis=2)
    s = jnp.einsum('bnqkgd,bnjkd->bnkgqj', qb, kcat, preferred_element_type=jnp.float32) * (dh ** -0.5)
    qrel = W + jnp.arange(W)[:, None]
    krel = jnp.arange(2 * W)[None, :]
    band = (krel <= qrel) & (krel > qrel - W)
    valid = band[None] & ((jnp.arange(nb)[:, None, None] > 0) | (krel[None] >= W))
    s = jnp.where(valid[None, :, None, None], s, jnp.finfo(jnp.float32).min)
    sink = jnp.broadcast_to(sinks.astype(jnp.float32).reshape(Hkv, G)[None, None, :, :, None, None], s.shape[:-1] + (1,))
    p = jax.nn.softmax(jnp.concatenate([s, sink], axis=-1), axis=-1)[..., :-1].astype(v.dtype)
    out = jnp.einsum('bnkgqj,bnjkd->bnqkgd', p, vcat)
    return out.reshape(B, S, Hq, dh)


def setup_inputs(seed: int = 0) -> dict:
    key = jax.random.key(seed)
    ks = jax.random.split(key, 24)
    f32 = jnp.float32

    def nrm(k, shape, fan_in):
        return jax.random.normal(k, shape, f32) * (fan_in ** -0.5)

    def gain(k):
        return 1.0 + 0.02 * jax.random.normal(k, (DEPTH, D_MODEL), f32)

    return {
        "x": jax.random.normal(ks[0], (BATCH, SEQ, D_MODEL), f32),
        "c": jax.random.normal(ks[1], (BATCH, D_MODEL), f32),
        "w_ada": nrm(ks[2], (DEPTH, D_MODEL, N_ADA * D_MODEL), D_MODEL),
        "b_ada": 0.01 * jax.random.normal(ks[3], (DEPTH, N_ADA * D_MODEL), f32),
        "norm_ffn1": gain(ks[4]),
        "ffn1_gate": nrm(ks[5], (DEPTH, D_MODEL, D_FF), D_MODEL),
        "ffn1_up": nrm(ks[6], (DEPTH, D_MODEL, D_FF), D_MODEL),
        "ffn1_down": nrm(ks[7], (DEPTH, D_FF, D_MODEL), D_FF),
        "norm_mix": gain(ks[8]),
        "w_in": nrm(ks[9], (DEPTH, D_MODEL, IN_COLS), D_MODEL),
        "swa_sinks": jax.random.normal(ks[10], (DEPTH, SWA_HEADS), f32),
        "w_branch_moba": nrm(ks[11], (DEPTH, MOBA_W, D_MODEL), MOBA_W),
        "w_branch_swa": nrm(ks[12], (DEPTH, SWA_QW, D_MODEL), SWA_QW),
        "w_out": nrm(ks[13], (DEPTH, D_MODEL, D_MODEL), D_MODEL),
        "norm_ffn2": gain(ks[14]),
        "ffn2_gate": nrm(ks[15], (DEPTH, D_MODEL, D_FF), D_MODEL),
        "ffn2_up": nrm(ks[16], (DEPTH, D_MODEL, D_FF), D_MODEL),
        "ffn2_down": nrm(ks[17], (DEPTH, D_FF, D_MODEL), D_FF),
        "norm_final": 1.0 + 0.02 * jax.random.normal(ks[18], (D_MODEL,), f32),
    }


def reference(x, c, w_ada, b_ada, norm_ffn1, ffn1_gate, ffn1_up, ffn1_down, norm_mix, w_in, swa_sinks,
              w_branch_moba, w_branch_swa, w_out, norm_ffn2, ffn2_gate, ffn2_up, ffn2_down, norm_final):
    B, S, _ = x.shape
    pos = jnp.arange(S)
    split_at = list(np.cumsum([MOBA_W, MOBA_W, MOBA_W, SWA_QW, SWA_KVW, SWA_KVW, D_MODEL]))
    for l in range(DEPTH):
        mod = jnp.einsum('bd,de->be', jax.nn.silu(c), w_ada[l]) + b_ada[l]
        (sh1, sc1, g1, sh2, sc2, g2, sh3, sc3, g3) = jnp.split(mod, N_ADA, axis=-1)

        h = modulate(rms_norm(x, norm_ffn1[l]), sh1, sc1)
        x = x + 0.5 * g1[:, None, :] * swiglu(h, ffn1_gate[l], ffn1_up[l], ffn1_down[l])

        h = modulate(rms_norm(x, norm_mix[l]), sh2, sc2)
        proj = jnp.einsum('bsd,de->bse', h, w_in[l])
        qa, ka, va, qb, kb, vb, ga, gb = jnp.split(proj, split_at, axis=-1)
        qa = rope(qa.reshape(B, S, MOBA_HEADS, MOBA_HEAD_DIM), pos)
        ka = rope(ka.reshape(B, S, MOBA_HEADS, MOBA_HEAD_DIM), pos)
        va = va.reshape(B, S, MOBA_HEADS, MOBA_HEAD_DIM)
        qb = rope(qb.reshape(B, S, SWA_HEADS, SWA_HEAD_DIM), pos)
        kb = rope(kb.reshape(B, S, SWA_KV_HEADS, SWA_HEAD_DIM), pos)
        vb = vb.reshape(B, S, SWA_KV_HEADS, SWA_HEAD_DIM)
        ya = moba_attention(qa, ka, va).reshape(B, S, MOBA_W)
        yb = sliding_window_attention(qb, kb, vb, swa_sinks[l]).reshape(B, S, SWA_QW)
        merged = (jax.nn.sigmoid(ga) * jnp.einsum('bse,ed->bsd', ya, w_branch_moba[l])
                  + jax.nn.sigmoid(gb) * jnp.einsum('bse,ed->bsd', yb, w_branch_swa[l]))
        x = x + g2[:, None, :] * jnp.einsum('bsd,de->bse', merged, w_out[l])

        h = modulate(rms_norm(x, norm_ffn2[l]), sh3, sc3)
        x = x + 0.5 * g3[:, None, :] * swiglu(h, ffn2_gate[l], ffn2_up[l], ffn2_down[l])
    return rms_norm(x, norm_final)
```

```python
import functools

import jax
import jax.numpy as jnp
from jax import lax
from jax.experimental import pallas as pl
from jax.experimental.pallas import tpu as pltpu

MOBA_HEAD_DIM = 128
MOBA_BLOCK = 256
MOBA_TOPK = 3
SWA_HEAD_DIM = 64
SWA_KV_HEADS = 2
SWA_WINDOW = 128
ROPE_THETA = 10000.0
EPS = 1e-6
N_ADA = 9

LANES = 128
VMEM_LIMIT = 56 * 1024 * 1024
MASK_VALUE = -1e30

F32 = jnp.float32
BF16 = jnp.bfloat16


def _sigmoid(x):
    return 1.0 / (1.0 + jnp.exp(-x))


def _rms_modulate(x, norm_w, shift, scale):
    y = x * lax.rsqrt(jnp.mean(x * x, axis=-1, keepdims=True) + EPS)
    return (y * norm_w) * (1.0 + scale) + shift


def _dot(a, b):
    return jnp.dot(a, b, preferred_element_type=F32)


def _dot_nt(a, b):
    return lax.dot_general(a, b, (((1,), (1,)), ((), ())), preferred_element_type=F32)


def _ada_kernel(c_ref, w_ref, b_ref, o_ref):
    c = c_ref[...]
    s = (c * _sigmoid(c)).astype(BF16)
    o_ref[...] = _dot(s, w_ref[...].astype(BF16)) + b_ref[...]


def _ada_modulation(c_pad, w, b):
    rows, d = c_pad.shape
    n = w.shape[1]
    tn = 1024
    return pl.pallas_call(
        _ada_kernel,
        grid=(n // tn,),
        in_specs=[
            pl.BlockSpec((rows, d), lambda j: (0, 0)),
            pl.BlockSpec((d, tn), lambda j: (0, j)),
            pl.BlockSpec((1, tn), lambda j: (0, j)),
        ],
        out_specs=pl.BlockSpec((rows, tn), lambda j: (0, j)),
        out_shape=jax.ShapeDtypeStruct((rows, n), F32),
        compiler_params=pltpu.CompilerParams(
            dimension_semantics=("arbitrary",), vmem_limit_bytes=VMEM_LIMIT),
        name="ada_modulation",
    )(c_pad, w, b.reshape(1, n))


def _ffn_kernel(x_ref, nw_ref, sh_ref, sc_ref, g_ref, wg_ref, wu_ref, wd_ref, fw_ref,
                o_ref, h_scr, acc_scr, *, final_norm):
    f = pl.program_id(1)
    nf = pl.num_programs(1)

    @pl.when(f == 0)
    def _():
        h = _rms_modulate(x_ref[...], nw_ref[...], sh_ref[...], sc_ref[...])
        h_scr[...] = h.astype(BF16)
        acc_scr[...] = jnp.zeros_like(acc_scr)

    h = h_scr[...]
    gate = _dot(h, wg_ref[...])
    up = _dot(h, wu_ref[...])
    act = (gate * _sigmoid(gate) * up).astype(BF16)
    acc_scr[...] += _dot(act, wd_ref[...])

    @pl.when(f == nf - 1)
    def _():
        y = x_ref[...] + (0.5 * g_ref[...]) * acc_scr[...]
        if final_norm:
            y = y * lax.rsqrt(jnp.mean(y * y, axis=-1, keepdims=True) + EPS) * fw_ref[...]
        o_ref[...] = y


def _ffn(x2d, norm_w, shift, scale, gate, wg, wu, wd, final_w, *, seq, final_norm):
    m, d = x2d.shape
    dff = wg.shape[1]
    tm, tf = 512, 512
    tiles_per_seq = seq // tm
    row = lambda i, f: (i, 0)
    per_batch = lambda i, f: (i // tiles_per_seq, 0, 0)
    return pl.pallas_call(
        functools.partial(_ffn_kernel, final_norm=final_norm),
        grid=(m // tm, dff // tf),
        in_specs=[
            pl.BlockSpec((tm, d), row),
            pl.BlockSpec((1, d), lambda i, f: (0, 0)),
            pl.BlockSpec((None, 1, d), per_batch),
            pl.BlockSpec((None, 1, d), per_batch),
            pl.BlockSpec((None, 1, d), per_batch),
            pl.BlockSpec((d, tf), lambda i, f: (0, f)),
            pl.BlockSpec((d, tf), lambda i, f: (0, f)),
            pl.BlockSpec((tf, d), lambda i, f: (f, 0)),
            pl.BlockSpec((1, d), lambda i, f: (0, 0)),
        ],
        out_specs=pl.BlockSpec((tm, d), row),
        out_shape=jax.ShapeDtypeStruct((m, d), F32),
        scratch_shapes=[pltpu.VMEM((tm, d), BF16), pltpu.VMEM((tm, d), F32)],
        compiler_params=pltpu.CompilerParams(
            dimension_semantics=("arbitrary", "arbitrary"), vmem_limit_bytes=VMEM_LIMIT),
        name="ffn_final" if final_norm else "ffn",
    )(x2d, norm_w.reshape(1, d), shift, scale, gate, wg, wu, wd, final_w.reshape(1, d))


def _rope128(x, cos, sin):
    return x * cos + pltpu.roll(x, 64, 1) * sin


def _rope64(x, cos, sin_lo, sin_hi):
    return x * cos + pltpu.roll(x, 96, 1) * sin_lo + pltpu.roll(x, 32, 1) * sin_hi


def _proj_kernel(x_ref, nw_ref, sh_ref, sc_ref, w_ref, wkv_ref,
                 cos_a_ref, sin_a_ref, cos_b_ref, sin_lo_ref, sin_hi_ref,
                 qa_ref, ka_ref, va_ref, qb_ref, kb_ref, vb_ref, ga_ref, gb_ref, h_scr):
    n = pl.program_id(1)

    @pl.when(n == 0)
    def _():
        h = _rms_modulate(x_ref[...], nw_ref[...], sh_ref[...], sc_ref[...])
        h_scr[...] = h.astype(BF16)

    h = h_scr[...]
    acc = _dot(h, w_ref[...])
    width = acc.shape[1]

    def rope_a(dst_ref):
        cos, sin = cos_a_ref[...], sin_a_ref[...]
        for s in range(0, width, LANES):
            dst_ref[:, s:s + LANES] = _rope128(acc[:, s:s + LANES], cos, sin).astype(BF16)

    @pl.when(n == 0)
    def _():
        rope_a(qa_ref)

    @pl.when(n == 1)
    def _():
        rope_a(ka_ref)

    @pl.when(n == 2)
    def _():
        va_ref[...] = acc.astype(BF16)

    @pl.when(n == 3)
    def _():
        cos, lo, hi = cos_b_ref[...], sin_lo_ref[...], sin_hi_ref[...]
        for s in range(0, width, LANES):
            qb_ref[:, s:s + LANES] = _rope64(acc[:, s:s + LANES], cos, lo, hi).astype(BF16)
        kv = _dot(h, wkv_ref[...])
        kb_ref[...] = _rope64(kv[:, :LANES], cos, lo, hi).astype(BF16)
        vb_ref[...] = kv[:, LANES:].astype(BF16)

    @pl.when((n == 4) | (n == 5))
    def _():
        ga_ref[...] = _sigmoid(acc).astype(BF16)

    @pl.when(n >= 6)
    def _():
        gb_ref[...] = _sigmoid(acc).astype(BF16)


def _in_projection(x2d, norm_w, shift, scale, w_main, w_kv, tables, *, seq):
    m, d = x2d.shape
    tm, tn = 512, 1024
    n_tiles = w_main.shape[1] // tn
    tiles_per_seq = seq // tm
    per_batch = lambda i, n: (i // tiles_per_seq, 0, 0)
    pos = lambda i, n: (i % tiles_per_seq, 0)
    full = lambda i, n: (i, 0)
    wide = lambda cols: jax.ShapeDtypeStruct((m, cols), BF16)
    table_spec = pl.BlockSpec((tm, LANES), pos)
    return pl.pallas_call(
        _proj_kernel,
        grid=(m // tm, n_tiles),
        in_specs=[
            pl.BlockSpec((tm, d), full),
            pl.BlockSpec((1, d), lambda i, n: (0, 0)),
            pl.BlockSpec((None, 1, d), per_batch),
            pl.BlockSpec((None, 1, d), per_batch),
            pl.BlockSpec((d, tn), lambda i, n: (0, n)),
            pl.BlockSpec((d, 2 * LANES), lambda i, n: (0, 0)),
            table_spec, table_spec, table_spec, table_spec, table_spec,
        ],
        out_specs=[
            pl.BlockSpec((tm, tn), full),
            pl.BlockSpec((tm, tn), full),
            pl.BlockSpec((tm, tn), full),
            pl.BlockSpec((tm, tn), full),
            pl.BlockSpec((tm, LANES), full),
            pl.BlockSpec((tm, LANES), full),
            pl.BlockSpec((tm, tn), lambda i, n: (i, jnp.clip(n - 4, 0, 1))),
            pl.BlockSpec((tm, tn), lambda i, n: (i, jnp.clip(n - 6, 0, 1))),
        ],
        out_shape=[wide(tn), wide(tn), wide(tn), wide(tn), wide(LANES), wide(LANES),
                   wide(2 * tn), wide(2 * tn)],
        scratch_shapes=[pltpu.VMEM((tm, d), BF16)],
        compiler_params=pltpu.CompilerParams(
            dimension_semantics=("arbitrary", "arbitrary"), vmem_limit_bytes=VMEM_LIMIT),
        name="in_projection",
    )(x2d, norm_w.reshape(1, d), shift, scale, w_main, w_kv, *tables)


def _rope_tables(seq):
    pos = jnp.arange(seq, dtype=F32)[:, None]

    def cos_sin(half):
        inv = ROPE_THETA ** (-jnp.arange(half, dtype=F32) / half)
        ang = pos * inv[None, :]
        return jnp.cos(ang), jnp.sin(ang)

    cos, sin = cos_sin(MOBA_HEAD_DIM // 2)
    cos_a = jnp.concatenate([cos, cos], axis=1)
    sin_a = jnp.concatenate([-sin, sin], axis=1)
    cos, sin = cos_sin(SWA_HEAD_DIM // 2)
    zero = jnp.zeros_like(sin)
    cos_b = jnp.concatenate([cos, cos, cos, cos], axis=1)
    sin_lo = jnp.concatenate([-sin, zero, -sin, zero], axis=1)
    sin_hi = jnp.concatenate([zero, sin, zero, sin], axis=1)
    return cos_a, sin_a, cos_b, sin_lo, sin_hi


def _moba_kernel(q_ref, k_ref, v_ref, o_ref, kaug_scr, kmean_scr, *, topk):
    i = pl.program_id(2)
    blk, dh = q_ref.shape
    seq = k_ref.shape[0]
    nb = seq // blk
    scale = dh ** -0.5

    @pl.when(i == 0)
    def _():
        k = k_ref[...]
        kmean = jnp.mean(k.astype(F32).reshape(nb, blk, dh), axis=1)
        kmean_scr[...] = kmean.astype(BF16)
        row = lax.broadcasted_iota(jnp.int32, (seq, LANES), 0)
        col = lax.broadcasted_iota(jnp.int32, (seq, LANES), 1)
        in_block = (row >= col * blk) & (row < (col + 1) * blk)
        kaug_scr[:, :dh] = k
        kaug_scr[:, dh:] = jnp.where(in_block, 1.0, 0.0).astype(BF16)

    q = q_ref[...]

    gs = _dot_nt(kmean_scr[...], q)
    blk_id = lax.broadcasted_iota(jnp.int32, gs.shape, 0)
    valid = blk_id < i
    vals = jnp.where(valid, gs, -jnp.inf)
    sel = jnp.zeros(gs.shape, dtype=jnp.bool_)
    for _ in range(topk):
        best = jnp.max(vals, axis=0, keepdims=True)
        first = jnp.min(jnp.where(vals == best, blk_id, nb), axis=0, keepdims=True)
        pick = blk_id == first
        sel = sel | pick
        vals = jnp.where(pick, -jnp.inf, vals)
    sel = sel & valid
    bias_t = jnp.where(sel, 0.0, MASK_VALUE).astype(F32)
    bias_t = jnp.concatenate([bias_t, jnp.zeros((LANES - nb, blk), F32)], axis=0)
    q_aug = jnp.concatenate([q, bias_t.T.astype(BF16)], axis=1)

    own = pl.multiple_of(i * blk, blk)
    s = _dot_nt(q, k_ref[pl.ds(own, blk), :]) * scale
    r = lax.broadcasted_iota(jnp.int32, s.shape, 0)
    c = lax.broadcasted_iota(jnp.int32, s.shape, 1)
    s = jnp.where(c <= r, s, MASK_VALUE)
    m0 = jnp.max(s, axis=-1, keepdims=True)
    p = jnp.exp(s - m0)
    l0 = jnp.sum(p, axis=-1, keepdims=True)
    acc0 = _dot(p.astype(BF16), v_ref[pl.ds(own, blk), :])

    def body(j, carry):
        m, l, acc = carry
        start = pl.multiple_of(j * blk, blk)
        s = _dot_nt(q_aug, kaug_scr[pl.ds(start, blk), :]) * scale
        m_new = jnp.maximum(m, jnp.max(s, axis=-1, keepdims=True))
        alpha = jnp.exp(m - m_new)
        p = jnp.exp(s - m_new)
        l = alpha * l + jnp.sum(p, axis=-1, keepdims=True)
        acc = alpha * acc + _dot(p.astype(BF16), v_ref[pl.ds(start, blk), :])
        return m_new, l, acc

    _, l, acc = lax.fori_loop(0, i, body, (m0, l0, acc0))
    o_ref[...] = (acc / l).astype(o_ref.dtype)


def _moba_attention(q, k, v, *, heads):
    b, seq, _ = q.shape
    dh, blk = MOBA_HEAD_DIM, MOBA_BLOCK
    nb = seq // blk
    q_spec = pl.BlockSpec((None, blk, dh), lambda bi, h, i: (bi, i, h))
    kv_spec = pl.BlockSpec((None, seq, dh), lambda bi, h, i: (bi, 0, h))
    return pl.pallas_call(
        functools.partial(_moba_kernel, topk=min(MOBA_TOPK, nb)),
        grid=(b, heads, nb),
        in_specs=[q_spec, kv_spec, kv_spec],
        out_specs=q_spec,
        out_shape=jax.ShapeDtypeStruct(q.shape, BF16),
        scratch_shapes=[pltpu.VMEM((seq, dh + LANES), BF16), pltpu.VMEM((nb, dh), BF16)],
        compiler_params=pltpu.CompilerParams(
            dimension_semantics=("arbitrary", "arbitrary", "arbitrary"),
            vmem_limit_bytes=VMEM_LIMIT),
        name="moba_attention",
    )(q, k, v)


def _swa_kernel(sink_ref, q_ref, kp_ref, kc_ref, vp_ref, vc_ref, o_ref, k_scr, v_scr, *, q_lane_blocks):
    i = pl.program_id(1)
    w = kp_ref.shape[0]
    tq = q_ref.shape[0]
    sub_blocks = tq // w
    half = LANES // 2
    scale = half ** -0.5
    groups = q_lane_blocks // SWA_KV_HEADS

    k_scr[:w, :] = kp_ref[...]
    k_scr[w:, :] = kc_ref[...]
    v_scr[:w, :] = vp_ref[...]
    v_scr[w:, :] = vc_ref[...]

    rows = lax.broadcasted_iota(jnp.int32, (2 * w, 2 * w), 0)
    qi = jnp.where(rows >= w, rows - w, rows)
    kc = lax.broadcasted_iota(jnp.int32, (2 * w, 2 * w), 1)
    band = (kc > qi) & (kc <= qi + w)
    lane = lax.broadcasted_iota(jnp.int32, (w, LANES), 1)
    low_lanes = lane < half
    top_rows = lax.broadcasted_iota(jnp.int32, (2 * w, 1), 0) < w

    def sub_block(u, carry):
        start = pl.multiple_of(u * w, w)
        first_key = jnp.where((i * sub_blocks + u) == 0, w, 0)
        valid = band & (kc >= first_key)
        k2 = k_scr[pl.ds(start, 2 * w), :]
        v2 = v_scr[pl.ds(start, 2 * w), :]
        for kvh in range(SWA_KV_HEADS):
            kd = jnp.concatenate([k2[:, kvh * half:(kvh + 1) * half]] * 2, axis=1)
            vd = jnp.concatenate([v2[:, kvh * half:(kvh + 1) * half]] * 2, axis=1)
            for g in range(groups):
                jb = kvh * groups + g
                qblk = q_ref[pl.ds(start, w), jb * LANES:(jb + 1) * LANES]
                zero = jnp.zeros_like(qblk)
                q2 = jnp.concatenate([jnp.where(low_lanes, qblk, zero),
                                      jnp.where(low_lanes, zero, qblk)], axis=0)
                s = _dot_nt(q2, kd) * scale
                s = jnp.where(valid, s, MASK_VALUE)
                sink = jnp.where(top_rows, sink_ref[2 * jb], sink_ref[2 * jb + 1])
                m = jnp.maximum(jnp.max(s, axis=-1, keepdims=True), sink)
                p = jnp.exp(s - m)
                den = jnp.sum(p, axis=-1, keepdims=True) + jnp.exp(sink - m)
                o = _dot(p.astype(BF16), vd) / den
                o_ref[pl.ds(start, w), jb * LANES:(jb + 1) * LANES] = jnp.where(
                    low_lanes, o[:w], o[w:]).astype(o_ref.dtype)
        return carry

    lax.fori_loop(0, sub_blocks, sub_block, 0)


def _swa_attention(q, k, v, sinks):
    b, seq, qw = q.shape
    w = SWA_WINDOW
    tq = 512
    sub_blocks = tq // w
    q_spec = pl.BlockSpec((None, tq, qw), lambda bi, i: (bi, i, 0))
    cur = pl.BlockSpec((None, tq, LANES), lambda bi, i: (bi, i, 0))
    prev = pl.BlockSpec((None, w, LANES), lambda bi, i: (bi, jnp.maximum(i * sub_blocks - 1, 0), 0))
    return pl.pallas_call(
        functools.partial(_swa_kernel, q_lane_blocks=qw // LANES),
        grid=(b, seq // tq),
        in_specs=[pl.BlockSpec(memory_space=pltpu.SMEM), q_spec, prev, cur, prev, cur],
        out_specs=q_spec,
        out_shape=jax.ShapeDtypeStruct(q.shape, BF16),
        scratch_shapes=[pltpu.VMEM((tq + w, LANES), BF16), pltpu.VMEM((tq + w, LANES), BF16)],
        compiler_params=pltpu.CompilerParams(
            dimension_semantics=("arbitrary", "arbitrary"), vmem_limit_bytes=VMEM_LIMIT),
        name="swa_attention",
    )(sinks, q, k, k, v, v)


def _merge_kernel(ya_ref, yb_ref, ga_ref, gb_ref, x_ref, g_ref, wa_ref, wb_ref, wo_ref, o_ref):
    merged = (ga_ref[...].astype(F32) * _dot(ya_ref[...], wa_ref[...])
              + gb_ref[...].astype(F32) * _dot(yb_ref[...], wb_ref[...]))
    o_ref[...] = x_ref[...] + g_ref[...] * _dot(merged.astype(BF16), wo_ref[...])


def _merge(ya, yb, ga, gb, x2d, gate, wa, wb, wo, *, seq):
    m, d = x2d.shape
    tm = 256
    tiles_per_seq = seq // tm
    row = lambda i: (i, 0)
    const = lambda i: (0, 0)
    resident = lambda shape: pl.BlockSpec(shape, const, pipeline_mode=pl.Buffered(1))
    return pl.pallas_call(
        _merge_kernel,
        grid=(m // tm,),
        in_specs=[
            pl.BlockSpec((tm, ya.shape[1]), row),
            pl.BlockSpec((tm, yb.shape[1]), row),
            pl.BlockSpec((tm, d), row),
            pl.BlockSpec((tm, d), row),
            pl.BlockSpec((tm, d), row),
            pl.BlockSpec((None, 1, d), lambda i: (i // tiles_per_seq, 0, 0)),
            resident(wa.shape), resident(wb.shape), resident(wo.shape),
        ],
        out_specs=pl.BlockSpec((tm, d), row),
        out_shape=jax.ShapeDtypeStruct((m, d), F32),
        compiler_params=pltpu.CompilerParams(
            dimension_semantics=("arbitrary",), vmem_limit_bytes=VMEM_LIMIT),
        name="merge_out_projection",
    )(ya, yb, ga, gb, x2d, gate, wa, wb, wo)


def kernel(x, c, w_ada, b_ada, norm_ffn1, ffn1_gate, ffn1_up, ffn1_down, norm_mix, w_in, swa_sinks,
           w_branch_moba, w_branch_swa, w_out, norm_ffn2, ffn2_gate, ffn2_up, ffn2_down, norm_final):
    b, seq, d = x.shape
    depth = w_ada.shape[0]
    moba_w = w_branch_moba.shape[1]
    swa_qw = w_branch_swa.shape[1]
    kv_w = SWA_KV_HEADS * SWA_HEAD_DIM
    qkv_cols = 3 * moba_w + swa_qw
    assert w_in.shape[2] == qkv_cols + 2 * kv_w + 2 * d
    assert moba_w == swa_qw == 1024 and d == 2048 and kv_w == LANES

    tables = _rope_tables(seq)
    c_pad = jnp.pad(c, ((0, 8 - b), (0, 0)))
    x2d = x.reshape(b * seq, d)

    for l in range(depth):
        mod = _ada_modulation(c_pad, w_ada[l], b_ada[l])[:b].reshape(b, N_ADA, 1, d)
        sh1, sc1, g1, sh2, sc2, g2, sh3, sc3, g3 = [mod[:, t] for t in range(N_ADA)]

        x2d = _ffn(x2d, norm_ffn1[l], sh1, sc1, g1,
                   ffn1_gate[l].astype(BF16), ffn1_up[l].astype(BF16), ffn1_down[l].astype(BF16),
                   norm_final, seq=seq, final_norm=False)

        w_l = w_in[l]
        w_main = jnp.concatenate([w_l[:, :qkv_cols], w_l[:, qkv_cols + 2 * kv_w:]], axis=1).astype(BF16)
        w_kv = w_l[:, qkv_cols:qkv_cols + 2 * kv_w].astype(BF16)
        qa, ka, va, qb, kb, vb, ga, gb = _in_projection(
            x2d, norm_mix[l], sh2, sc2, w_main, w_kv, tables, seq=seq)

        rs = lambda t: t.reshape(b, seq, t.shape[-1])
        ya = _moba_attention(rs(qa), rs(ka), rs(va), heads=moba_w // MOBA_HEAD_DIM)
        yb = _swa_attention(rs(qb), rs(kb), rs(vb), swa_sinks[l])

        x2d = _merge(ya.reshape(b * seq, moba_w), yb.reshape(b * seq, swa_qw), ga, gb, x2d, g2,
                     w_branch_moba[l].astype(BF16), w_branch_swa[l].astype(BF16),
                     w_out[l].astype(BF16), seq=seq)

        x2d = _ffn(x2d, norm_ffn2[l], sh3, sc3, g3,
                   ffn2_gate[l].astype(BF16), ffn2_up[l].astype(BF16), ffn2_down[l].astype(BF16),
                   norm_final, seq=seq, final_norm=(l == depth - 1))

    return x2d.reshape(b, seq, d)
```

```python
import functools

import jax
import jax.numpy as jnp
from jax import lax
from jax.experimental import pallas as pl
from jax.experimental.pallas import tpu as pltpu

MOBA_HEAD_DIM = 128
MOBA_BLOCK = 256
MOBA_TOPK = 3
SWA_HEAD_DIM = 64
SWA_KV_HEADS = 2
SWA_WINDOW = 128
ROPE_THETA = 10000.0
EPS = 1e-6
N_ADA = 9

LANES = 128
VMEM_LIMIT = 56 * 1024 * 1024
MASK_VALUE = -1e30
LOG2_E = 1.4426950408889634
MOBA_LOOP_GROUP = 2
MOBA_HEADS_PER_STEP = 2

F32 = jnp.float32
BF16 = jnp.bfloat16


def _sigmoid(x):
    return 1.0 / (1.0 + jnp.exp(-x))


def _rms_modulate(x, norm_w, shift, scale):
    y = x * lax.rsqrt(jnp.mean(x * x, axis=-1, keepdims=True) + EPS)
    return (y * norm_w) * (1.0 + scale) + shift


def _dot(a, b):
    return jnp.dot(a, b, preferred_element_type=F32)


def _dot_nt(a, b):
    return lax.dot_general(a, b, (((1,), (1,)), ((), ())), preferred_element_type=F32)


def _ada_kernel(c_ref, w_ref, b_ref, o_ref):
    c = c_ref[...]
    s = (c * _sigmoid(c)).astype(BF16)
    o_ref[...] = _dot(s, w_ref[...].astype(BF16)) + b_ref[...]


def _ada_modulation(c_pad, w, b):
    rows, d = c_pad.shape
    n = w.shape[1]
    tn = 1024
    return pl.pallas_call(
        _ada_kernel,
        grid=(n // tn,),
        in_specs=[
            pl.BlockSpec((rows, d), lambda j: (0, 0)),
            pl.BlockSpec((d, tn), lambda j: (0, j)),
            pl.BlockSpec((1, tn), lambda j: (0, j)),
        ],
        out_specs=pl.BlockSpec((rows, tn), lambda j: (0, j)),
        out_shape=jax.ShapeDtypeStruct((rows, n), F32),
        compiler_params=pltpu.CompilerParams(
            dimension_semantics=("arbitrary",), vmem_limit_bytes=VMEM_LIMIT),
        name="ada_modulation",
    )(c_pad, w, b.reshape(1, n))


def _ffn_kernel(x_ref, nw_ref, sh_ref, sc_ref, g_ref, wg_ref, wu_ref, wd_ref, fw_ref,
                o_ref, h_scr, acc_scr, *, final_norm):
    f = pl.program_id(1)
    nf = pl.num_programs(1)

    @pl.when(f == 0)
    def _():
        h = _rms_modulate(x_ref[...], nw_ref[...], sh_ref[...], sc_ref[...])
        h_scr[...] = h.astype(BF16)
        acc_scr[...] = jnp.zeros_like(acc_scr)

    h = h_scr[...]
    gate = _dot(h, wg_ref[...])
    up = _dot(h, wu_ref[...])
    act = (gate * _sigmoid(gate) * up).astype(BF16)
    acc_scr[...] += _dot(act, wd_ref[...])

    @pl.when(f == nf - 1)
    def _():
        y = x_ref[...] + (0.5 * g_ref[...]) * acc_scr[...]
        if final_norm:
            y = y * lax.rsqrt(jnp.mean(y * y, axis=-1, keepdims=True) + EPS) * fw_ref[...]
        o_ref[...] = y


def _ffn(x2d, norm_w, shift, scale, gate, wg, wu, wd, final_w, *, seq, final_norm):
    m, d = x2d.shape
    dff = wg.shape[1]
    tm, tf = 512, 512
    tiles_per_seq = seq // tm
    row = lambda i, f: (i, 0)
    per_batch = lambda i, f: (i // tiles_per_seq, 0, 0)
    return pl.pallas_call(
        functools.partial(_ffn_kernel, final_norm=final_norm),
        grid=(m // tm, dff // tf),
        in_specs=[
            pl.BlockSpec((tm, d), row),
            pl.BlockSpec((1, d), lambda i, f: (0, 0)),
            pl.BlockSpec((None, 1, d), per_batch),
            pl.BlockSpec((None, 1, d), per_batch),
            pl.BlockSpec((None, 1, d), per_batch),
            pl.BlockSpec((d, tf), lambda i, f: (0, f)),
            pl.BlockSpec((d, tf), lambda i, f: (0, f)),
            pl.BlockSpec((tf, d), lambda i, f: (f, 0)),
            pl.BlockSpec((1, d), lambda i, f: (0, 0)),
        ],
        out_specs=pl.BlockSpec((tm, d), row),
        out_shape=jax.ShapeDtypeStruct((m, d), F32),
        scratch_shapes=[pltpu.VMEM((tm, d), BF16), pltpu.VMEM((tm, d), F32)],
        compiler_params=pltpu.CompilerParams(
            dimension_semantics=("arbitrary", "arbitrary"), vmem_limit_bytes=VMEM_LIMIT),
        name="ffn_final" if final_norm else "ffn",
    )(x2d, norm_w.reshape(1, d), shift, scale, gate, wg, wu, wd, final_w.reshape(1, d))


def _rope128(x, cos, sin):
    return x * cos + pltpu.roll(x, 64, 1) * sin


def _rope64(x, cos, sin_lo, sin_hi):
    return x * cos + pltpu.roll(x, 96, 1) * sin_lo + pltpu.roll(x, 32, 1) * sin_hi


def _proj_kernel(x_ref, nw_ref, sh_ref, sc_ref, w_ref, wkv_ref,
                 cos_a_ref, sin_a_ref, cos_b_ref, sin_lo_ref, sin_hi_ref,
                 qa_ref, ka_ref, va_ref, qb_ref, kb_ref, vb_ref, ga_ref, gb_ref, h_scr):
    n = pl.program_id(1)

    @pl.when(n == 0)
    def _():
        h = _rms_modulate(x_ref[...], nw_ref[...], sh_ref[...], sc_ref[...])
        h_scr[...] = h.astype(BF16)

    h = h_scr[...]
    width = w_ref.shape[1]
    chunk = 2 * LANES

    def project(dst_ref, epilogue):
        for s in range(0, width, chunk):
            acc = _dot(h, w_ref[:, s:s + chunk])
            for t in range(0, chunk, LANES):
                dst_ref[:, s + t:s + t + LANES] = epilogue(acc[:, t:t + LANES]).astype(BF16)

    rope_a = lambda x: _rope128(x, cos_a_ref[...], sin_a_ref[...])
    rope_b = lambda x: _rope64(x, cos_b_ref[...], sin_lo_ref[...], sin_hi_ref[...])

    @pl.when(n == 0)
    def _():
        project(qa_ref, rope_a)

    @pl.when(n == 1)
    def _():
        project(ka_ref, rope_a)

    @pl.when(n == 2)
    def _():
        project(va_ref, lambda x: x)

    @pl.when(n == 3)
    def _():
        project(qb_ref, rope_b)
        kv = _dot(h, wkv_ref[...])
        kb_ref[...] = rope_b(kv[:, :LANES]).astype(BF16)
        vb_ref[...] = kv[:, LANES:].astype(BF16)

    @pl.when((n == 4) | (n == 5))
    def _():
        project(ga_ref, _sigmoid)

    @pl.when(n >= 6)
    def _():
        project(gb_ref, _sigmoid)


def _in_projection(x2d, norm_w, shift, scale, w_main, w_kv, tables, *, seq):
    m, d = x2d.shape
    tm, tn = 512, 1024
    n_tiles = w_main.shape[1] // tn
    tiles_per_seq = seq // tm
    per_batch = lambda i, n: (i // tiles_per_seq, 0, 0)
    pos = lambda i, n: (i % tiles_per_seq, 0)
    full = lambda i, n: (i, 0)
    wide = lambda cols: jax.ShapeDtypeStruct((m, cols), BF16)
    table_spec = pl.BlockSpec((tm, LANES), pos)
    return pl.pallas_call(
        _proj_kernel,
        grid=(m // tm, n_tiles),
        in_specs=[
            pl.BlockSpec((tm, d), full),
            pl.BlockSpec((1, d), lambda i, n: (0, 0)),
            pl.BlockSpec((None, 1, d), per_batch),
            pl.BlockSpec((None, 1, d), per_batch),
            pl.BlockSpec((d, tn), lambda i, n: (0, n)),
            pl.BlockSpec((d, 2 * LANES), lambda i, n: (0, 0)),
            table_spec, table_spec, table_spec, table_spec, table_spec,
        ],
        out_specs=[
            pl.BlockSpec((tm, tn), full),
            pl.BlockSpec((tm, tn), full),
            pl.BlockSpec((tm, tn), full),
            pl.BlockSpec((tm, tn), full),
            pl.BlockSpec((tm, LANES), full),
            pl.BlockSpec((tm, LANES), full),
            pl.BlockSpec((tm, tn), lambda i, n: (i, jnp.clip(n - 4, 0, 1))),
            pl.BlockSpec((tm, tn), lambda i, n: (i, jnp.clip(n - 6, 0, 1))),
        ],
        out_shape=[wide(tn), wide(tn), wide(tn), wide(tn), wide(LANES), wide(LANES),
                   wide(2 * tn), wide(2 * tn)],
        scratch_shapes=[pltpu.VMEM((tm, d), BF16)],
        compiler_params=pltpu.CompilerParams(
            dimension_semantics=("arbitrary", "arbitrary"), vmem_limit_bytes=VMEM_LIMIT),
        name="in_projection",
    )(x2d, norm_w.reshape(1, d), shift, scale, w_main, w_kv, *tables)


def _rope_tables(seq):
    pos = jnp.arange(seq, dtype=F32)[:, None]

    def cos_sin(half):
        inv = ROPE_THETA ** (-jnp.arange(half, dtype=F32) / half)
        ang = pos * inv[None, :]
        return jnp.cos(ang), jnp.sin(ang)

    cos, sin = cos_sin(MOBA_HEAD_DIM // 2)
    cos_a = jnp.concatenate([cos, cos], axis=1)
    sin_a = jnp.concatenate([-sin, sin], axis=1)
    cos, sin = cos_sin(SWA_HEAD_DIM // 2)
    zero = jnp.zeros_like(sin)
    cos_b = jnp.concatenate([cos, cos, cos, cos], axis=1)
    sin_lo = jnp.concatenate([-sin, zero, -sin, zero], axis=1)
    sin_hi = jnp.concatenate([zero, sin, zero, sin], axis=1)
    return cos_a, sin_a, cos_b, sin_lo, sin_hi


def _moba_kernel(q_ref, k_ref, v_ref, o_ref, kaug_scr, vt_scr, kmean_scr, s_scr, *, topk, group):
    i = pl.program_id(2)
    blk = q_ref.shape[0]
    heads, nb, dh, _ = vt_scr.shape
    exp2_scale = dh ** -0.5 * LOG2_E
    head_lanes = lambda h: slice(h * dh, (h + 1) * dh)

    @pl.when(i == 0)
    def _():
        col = lax.broadcasted_iota(jnp.int32, (blk, LANES), 1)
        for h in range(heads):
            for j in range(nb):
                k = k_ref[j * blk:(j + 1) * blk, head_lanes(h)]
                kmean_scr[h, j:j + 1, :] = jnp.mean(k.astype(F32), axis=0, keepdims=True)
                kaug_scr[h, j, :, :dh] = k
                kaug_scr[h, j, :, dh:] = jnp.where(col == j, 1.0, 0.0).astype(BF16)
                v = v_ref[j * blk:(j + 1) * blk, head_lanes(h)]
                vt_scr[h, j] = v.astype(F32).T.astype(BF16)

    def gate(h):
        q = q_ref[:, head_lanes(h)]
        gs = _dot_nt(kmean_scr[h].astype(BF16), q)
        blk_id = lax.broadcasted_iota(jnp.int32, gs.shape, 0)
        valid = blk_id < i
        vals = jnp.where(valid, gs, -jnp.inf)
        sel = jnp.zeros(gs.shape, dtype=jnp.bool_)
        for _ in range(topk):
            best = jnp.max(vals, axis=0, keepdims=True)
            first = jnp.min(jnp.where(vals == best, blk_id, nb), axis=0, keepdims=True)
            pick = blk_id == first
            sel = sel | pick
            vals = jnp.where(pick, -jnp.inf, vals)
        sel = sel & valid
        bias_t = jnp.where(sel, 0.0, MASK_VALUE).astype(F32)
        bias_t = jnp.concatenate([bias_t, jnp.zeros((LANES - nb, blk), F32)], axis=0)
        q_t = q.astype(F32).T.astype(BF16)
        return jnp.concatenate([q_t, bias_t.astype(BF16)], axis=0)

    q_aug_t = [gate(h) for h in range(heads)]

    def attend(n_past):
        own_slot = n_past
        r = lax.broadcasted_iota(jnp.int32, (blk, blk), 0)
        c = lax.broadcasted_iota(jnp.int32, (blk, blk), 1)
        m = []
        for h in range(heads):
            s = _dot(kaug_scr[h, i, :, :dh], q_aug_t[h][:dh]) * exp2_scale
            s = jnp.where(r <= c, s, MASK_VALUE)
            s_scr[h, own_slot] = s
            mh = jnp.max(s, axis=0, keepdims=True)
            for j in range(n_past):
                s = _dot(kaug_scr[h, j], q_aug_t[h]) * exp2_scale
                s_scr[h, j] = s
                mh = jnp.maximum(mh, jnp.max(s, axis=0, keepdims=True))
            m.append(mh)

        for h in range(heads):
            l = jnp.zeros_like(m[h])
            acc = jnp.zeros((dh, blk), F32)
            for slot in range(n_past + 1):
                p = jnp.exp2(s_scr[h, slot] - m[h])
                l = l + jnp.sum(p, axis=0, keepdims=True)
                vt = vt_scr[h, i] if slot == own_slot else vt_scr[h, slot]
                acc = acc + _dot(vt, p.astype(BF16))
            o_ref[:, head_lanes(h)] = (acc / l).T.astype(o_ref.dtype)

    n_groups = (i + group - 1) // group
    for n in range(nb // group + 1):
        pl.when(n_groups == n)(functools.partial(attend, n * group))


def _moba_attention(q, k, v, *, heads):
    b, seq, _ = q.shape
    dh, blk = MOBA_HEAD_DIM, MOBA_BLOCK
    nb = seq // blk
    hps = MOBA_HEADS_PER_STEP
    q_spec = pl.BlockSpec((None, blk, hps * dh), lambda bi, h, i: (bi, i, h))
    kv_spec = pl.BlockSpec((None, seq, hps * dh), lambda bi, h, i: (bi, 0, h))
    return pl.pallas_call(
        functools.partial(_moba_kernel, topk=min(MOBA_TOPK, nb), group=MOBA_LOOP_GROUP),
        grid=(b, heads // hps, nb),
        in_specs=[q_spec, kv_spec, kv_spec],
        out_specs=q_spec,
        out_shape=jax.ShapeDtypeStruct(q.shape, BF16),
        scratch_shapes=[pltpu.VMEM((hps, nb, blk, dh + LANES), BF16),
                        pltpu.VMEM((hps, nb, dh, blk), BF16),
                        pltpu.VMEM((hps, nb, dh), F32),
                        pltpu.VMEM((hps, nb + 1, blk, blk), F32)],
        compiler_params=pltpu.CompilerParams(
            dimension_semantics=("arbitrary", "arbitrary", "arbitrary"),
            vmem_limit_bytes=VMEM_LIMIT),
        name="moba_attention",
    )(q, k, v)


def _swa_kernel(sink_ref, q_ref, kp_ref, kc_ref, vp_ref, vc_ref, o_ref, k_scr, v_scr, *, q_lane_blocks):
    i = pl.program_id(1)
    w = kp_ref.shape[0]
    tq = q_ref.shape[0]
    sub_blocks = tq // w
    half = LANES // 2
    scale = half ** -0.5
    groups = q_lane_blocks // SWA_KV_HEADS

    k_scr[:w, :] = kp_ref[...]
    k_scr[w:, :] = kc_ref[...]
    v_scr[:w, :] = vp_ref[...]
    v_scr[w:, :] = vc_ref[...]

    rows = lax.broadcasted_iota(jnp.int32, (2 * w, 2 * w), 0)
    qi = jnp.where(rows >= w, rows - w, rows)
    kc = lax.broadcasted_iota(jnp.int32, (2 * w, 2 * w), 1)
    band = (kc > qi) & (kc <= qi + w)
    lane = lax.broadcasted_iota(jnp.int32, (w, LANES), 1)
    low_lanes = lane < half
    top_rows = lax.broadcasted_iota(jnp.int32, (2 * w, 1), 0) < w

    def sub_block(u, carry):
        start = pl.multiple_of(u * w, w)
        first_key = jnp.where((i * sub_blocks + u) == 0, w, 0)
        valid = band & (kc >= first_key)
        k2 = k_scr[pl.ds(start, 2 * w), :]
        v2 = v_scr[pl.ds(start, 2 * w), :]
        for kvh in range(SWA_KV_HEADS):
            kd = jnp.concatenate([k2[:, kvh * half:(kvh + 1) * half]] * 2, axis=1)
            vd = jnp.concatenate([v2[:, kvh * half:(kvh + 1) * half]] * 2, axis=1)
            for g in range(groups):
                jb = kvh * groups + g
                qblk = q_ref[pl.ds(start, w), jb * LANES:(jb + 1) * LANES]
                zero = jnp.zeros_like(qblk)
                q2 = jnp.concatenate([jnp.where(low_lanes, qblk, zero),
                                      jnp.where(low_lanes, zero, qblk)], axis=0)
                s = _dot_nt(q2, kd) * scale
                s = jnp.where(valid, s, MASK_VALUE)
                sink = jnp.where(top_rows, sink_ref[2 * jb], sink_ref[2 * jb + 1])
                m = jnp.maximum(jnp.max(s, axis=-1, keepdims=True), sink)
                p = jnp.exp(s - m)
                den = jnp.sum(p, axis=-1, keepdims=True) + jnp.exp(sink - m)
                o = _dot(p.astype(BF16), vd) / den
                o_ref[pl.ds(start, w), jb * LANES:(jb + 1) * LANES] = jnp.where(
                    low_lanes, o[:w], o[w:]).astype(o_ref.dtype)
        return carry

    lax.fori_loop(0, sub_blocks, sub_block, 0)


def _swa_attention(q, k, v, sinks):
    b, seq, qw = q.shape
    w = SWA_WINDOW
    tq = 512
    sub_blocks = tq // w
    q_spec = pl.BlockSpec((None, tq, qw), lambda bi, i: (bi, i, 0))
    cur = pl.BlockSpec((None, tq, LANES), lambda bi, i: (bi, i, 0))
    prev = pl.BlockSpec((None, w, LANES), lambda bi, i: (bi, jnp.maximum(i * sub_blocks - 1, 0), 0))
    return pl.pallas_call(
        functools.partial(_swa_kernel, q_lane_blocks=qw // LANES),
        grid=(b, seq // tq),
        in_specs=[pl.BlockSpec(memory_space=pltpu.SMEM), q_spec, prev, cur, prev, cur],
        out_specs=q_spec,
        out_shape=jax.ShapeDtypeStruct(q.shape, BF16),
        scratch_shapes=[pltpu.VMEM((tq + w, LANES), BF16), pltpu.VMEM((tq + w, LANES), BF16)],
        compiler_params=pltpu.CompilerParams(
            dimension_semantics=("arbitrary", "arbitrary"), vmem_limit_bytes=VMEM_LIMIT),
        name="swa_attention",
    )(sinks, q, k, k, v, v)


def _merge_kernel(ya_ref, yb_ref, ga_ref, gb_ref, x_ref, g_ref, wa_ref, wb_ref, wo_ref, o_ref):
    merged = (ga_ref[...].astype(F32) * _dot(ya_ref[...], wa_ref[...])
              + gb_ref[...].astype(F32) * _dot(yb_ref[...], wb_ref[...]))
    o_ref[...] = x_ref[...] + g_ref[...] * _dot(merged.astype(BF16), wo_ref[...])


def _merge(ya, yb, ga, gb, x2d, gate, wa, wb, wo, *, seq):
    m, d = x2d.shape
    tm = 256
    tiles_per_seq = seq // tm
    row = lambda i: (i, 0)
    const = lambda i: (0, 0)
    resident = lambda shape: pl.BlockSpec(shape, const, pipeline_mode=pl.Buffered(1))
    return pl.pallas_call(
        _merge_kernel,
        grid=(m // tm,),
        in_specs=[
            pl.BlockSpec((tm, ya.shape[1]), row),
            pl.BlockSpec((tm, yb.shape[1]), row),
            pl.BlockSpec((tm, d), row),
            pl.BlockSpec((tm, d), row),
            pl.BlockSpec((tm, d), row),
            pl.BlockSpec((None, 1, d), lambda i: (i // tiles_per_seq, 0, 0)),
            resident(wa.shape), resident(wb.shape), resident(wo.shape),
        ],
        out_specs=pl.BlockSpec((tm, d), row),
        out_shape=jax.ShapeDtypeStruct((m, d), F32),
        compiler_params=pltpu.CompilerParams(
            dimension_semantics=("arbitrary",), vmem_limit_bytes=VMEM_LIMIT),
        name="merge_out_projection",
    )(ya, yb, ga, gb, x2d, gate, wa, wb, wo)


def kernel(x, c, w_ada, b_ada, norm_ffn1, ffn1_gate, ffn1_up, ffn1_down, norm_mix, w_in, swa_sinks,
           w_branch_moba, w_branch_swa, w_out, norm_ffn2, ffn2_gate, ffn2_up, ffn2_down, norm_final):
    b, seq, d = x.shape
    depth = w_ada.shape[0]
    moba_w = w_branch_moba.shape[1]
    swa_qw = w_branch_swa.shape[1]
    kv_w = SWA_KV_HEADS * SWA_HEAD_DIM
    qkv_cols = 3 * moba_w + swa_qw
    assert w_in.shape[2] == qkv_cols + 2 * kv_w + 2 * d
    assert moba_w == swa_qw == 1024 and d == 2048 and kv_w == LANES

    tables = _rope_tables(seq)
    c_pad = jnp.pad(c, ((0, 8 - b), (0, 0)))
    x2d = x.reshape(b * seq, d)

    for l in range(depth):
        mod = _ada_modulation(c_pad, w_ada[l], b_ada[l])[:b].reshape(b, N_ADA, 1, d)
        sh1, sc1, g1, sh2, sc2, g2, sh3, sc3, g3 = [mod[:, t] for t in range(N_ADA)]

        x2d = _ffn(x2d, norm_ffn1[l], sh1, sc1, g1,
                   ffn1_gate[l].astype(BF16), ffn1_up[l].astype(BF16), ffn1_down[l].astype(BF16),
                   norm_final, seq=seq, final_norm=False)

        w_l = w_in[l]
        w_main = jnp.concatenate([w_l[:, :qkv_cols], w_l[:, qkv_cols + 2 * kv_w:]], axis=1).astype(BF16)
        w_kv = w_l[:, qkv_cols:qkv_cols + 2 * kv_w].astype(BF16)
        qa, ka, va, qb, kb, vb, ga, gb = _in_projection(
            x2d, norm_mix[l], sh2, sc2, w_main, w_kv, tables, seq=seq)

        rs = lambda t: t.reshape(b, seq, t.shape[-1])
        ya = _moba_attention(rs(qa), rs(ka), rs(va), heads=moba_w // MOBA_HEAD_DIM)
        yb = _swa_attention(rs(qb), rs(kb), rs(vb), swa_sinks[l])

        x2d = _merge(ya.reshape(b * seq, moba_w), yb.reshape(b * seq, swa_qw), ga, gb, x2d, g2,
                     w_branch_moba[l].astype(BF16), w_branch_swa[l].astype(BF16),
                     w_out[l].astype(BF16), seq=seq)

        x2d = _ffn(x2d, norm_ffn2[l], sh3, sc3, g3,
                   ffn2_gate[l].astype(BF16), ffn2_up[l].astype(BF16), ffn2_down[l].astype(BF16),
                   norm_final, seq=seq, final_norm=(l == depth - 1))

    return x2d.reshape(b, seq, d)
```

```python
import functools

import jax
import jax.numpy as jnp
from jax import lax
from jax.experimental import pallas as pl
from jax.experimental.pallas import tpu as pltpu

MOBA_HEAD_DIM = 128
MOBA_BLOCK = 256
MOBA_TOPK = 3
SWA_HEAD_DIM = 64
SWA_KV_HEADS = 2
SWA_WINDOW = 128
ROPE_THETA = 10000.0
EPS = 1e-6
N_ADA = 9

LANES = 128
ROW_CHUNK = 16
VMEM_LIMIT = 56 * 1024 * 1024
MASK_VALUE = -1e30
LOG2_E = 1.4426950408889634
MOBA_LOOP_GROUP = 2
MOBA_HEADS_PER_STEP = 4

F32 = jnp.float32
BF16 = jnp.bfloat16


def _sigmoid(x):
    return 1.0 / (1.0 + jnp.exp(-x))


def _row_rsqrt_mean_square(x):
    r = lax.rsqrt(jnp.mean(x * x, axis=-1, keepdims=True) + EPS)
    return jnp.broadcast_to(r, (x.shape[0], LANES))


def _across_lanes(r, width):
    return pltpu.repeat(r, width // LANES, axis=1)


def _for_row_chunks(rows, body):
    def step(c, carry):
        body(pl.ds(pl.multiple_of(c * ROW_CHUNK, ROW_CHUNK), ROW_CHUNK))
        return carry
    lax.fori_loop(0, rows // ROW_CHUNK, step, 0, unroll=4)


def _rms_modulate_to(h_ref, x_ref, rinv_scr, norm_w, shift, scale):
    rinv_scr[...] = _row_rsqrt_mean_square(x_ref[...])
    gain = norm_w * (1.0 + scale)

    def slab(rows):
        rinv = _across_lanes(rinv_scr[rows, :], x_ref.shape[1])
        h_ref[rows, :] = (x_ref[rows, :] * rinv * gain + shift).astype(BF16)

    _for_row_chunks(x_ref.shape[0], slab)


def _dot(a, b):
    return jnp.dot(a, b, preferred_element_type=F32)


def _dot_nt(a, b):
    return lax.dot_general(a, b, (((1,), (1,)), ((), ())), preferred_element_type=F32)


def _ada_kernel(c_ref, w_ref, b_ref, o_ref):
    c = c_ref[...]
    s = (c * _sigmoid(c)).astype(BF16)
    o_ref[...] = _dot(s, w_ref[...].astype(BF16)) + b_ref[...]


def _ada_modulation(c_pad, w, b):
    rows, d = c_pad.shape
    n = w.shape[1]
    tn = 1024
    return pl.pallas_call(
        _ada_kernel,
        grid=(n // tn,),
        in_specs=[
            pl.BlockSpec((rows, d), lambda j: (0, 0)),
            pl.BlockSpec((d, tn), lambda j: (0, j)),
            pl.BlockSpec((1, tn), lambda j: (0, j)),
        ],
        out_specs=pl.BlockSpec((rows, tn), lambda j: (0, j)),
        out_shape=jax.ShapeDtypeStruct((rows, n), F32),
        compiler_params=pltpu.CompilerParams(
            dimension_semantics=("arbitrary",), vmem_limit_bytes=VMEM_LIMIT),
        name="ada_modulation",
    )(c_pad, w, b.reshape(1, n))


def _ffn_kernel(x_ref, nw_ref, sh_ref, sc_ref, g_ref, wg_ref, wu_ref, wd_ref, fw_ref,
                o_ref, h_scr, acc_scr, rinv_scr, *, final_norm):
    f = pl.program_id(1)
    nf = pl.num_programs(1)

    @pl.when(f == 0)
    def _():
        _rms_modulate_to(h_scr, x_ref, rinv_scr, nw_ref[...], sh_ref[...], sc_ref[...])
        acc_scr[...] = jnp.zeros_like(acc_scr)

    h = h_scr[...]
    gate = _dot(h, wg_ref[...])
    up = _dot(h, wu_ref[...])
    act = (gate * _sigmoid(gate) * up).astype(BF16)
    acc_scr[...] += _dot(act, wd_ref[...])

    @pl.when(f == nf - 1)
    def _():
        y = x_ref[...] + (0.5 * g_ref[...]) * acc_scr[...]
        o_ref[...] = y
        if final_norm:
            rinv_scr[...] = _row_rsqrt_mean_square(y)
            final_w = fw_ref[...]

            def slab(rows):
                rinv = _across_lanes(rinv_scr[rows, :], o_ref.shape[1])
                o_ref[rows, :] = o_ref[rows, :] * rinv * final_w

            _for_row_chunks(o_ref.shape[0], slab)


def _ffn(x2d, norm_w, shift, scale, gate, wg, wu, wd, final_w, *, seq, final_norm):
    m, d = x2d.shape
    dff = wg.shape[1]
    tm, tf = 512, 512
    tiles_per_seq = seq // tm
    row = lambda i, f: (i, 0)
    per_batch = lambda i, f: (i // tiles_per_seq, 0, 0)
    return pl.pallas_call(
        functools.partial(_ffn_kernel, final_norm=final_norm),
        grid=(m // tm, dff // tf),
        in_specs=[
            pl.BlockSpec((tm, d), row),
            pl.BlockSpec((1, d), lambda i, f: (0, 0)),
            pl.BlockSpec((None, 1, d), per_batch),
            pl.BlockSpec((None, 1, d), per_batch),
            pl.BlockSpec((None, 1, d), per_batch),
            pl.BlockSpec((d, tf), lambda i, f: (0, f)),
            pl.BlockSpec((d, tf), lambda i, f: (0, f)),
            pl.BlockSpec((tf, d), lambda i, f: (f, 0)),
            pl.BlockSpec((1, d), lambda i, f: (0, 0)),
        ],
        out_specs=pl.BlockSpec((tm, d), row),
        out_shape=jax.ShapeDtypeStruct((m, d), F32),
        scratch_shapes=[pltpu.VMEM((tm, d), BF16), pltpu.VMEM((tm, d), F32),
                        pltpu.VMEM((tm, LANES), F32)],
        compiler_params=pltpu.CompilerParams(
            dimension_semantics=("arbitrary", "arbitrary"), vmem_limit_bytes=VMEM_LIMIT),
        name="ffn_final" if final_norm else "ffn",
    )(x2d, norm_w.reshape(1, d), shift, scale, gate, wg, wu, wd, final_w.reshape(1, d))


def _rope128(x, cos, sin):
    return x * cos + pltpu.roll(x, 64, 1) * sin


def _rope64(x, cos, sin_lo, sin_hi):
    return x * cos + pltpu.roll(x, 96, 1) * sin_lo + pltpu.roll(x, 32, 1) * sin_hi


def _proj_kernel(x_ref, nw_ref, sh_ref, sc_ref, w_ref, wkv_ref,
                 cos_a_ref, sin_a_ref, cos_b_ref, sin_lo_ref, sin_hi_ref,
                 qa_ref, ka_ref, va_ref, qb_ref, kb_ref, vb_ref, ga_ref, gb_ref, h_scr, rinv_scr):
    n = pl.program_id(1)

    @pl.when(n == 0)
    def _():
        _rms_modulate_to(h_scr, x_ref, rinv_scr, nw_ref[...], sh_ref[...], sc_ref[...])

    h = h_scr[...]
    chunk = 2 * LANES
    identity = lambda x: x

    def project(dst_ref, epilogue, col0):
        for s in range(0, dst_ref.shape[1], chunk):
            acc = _dot(h, w_ref[:, col0 + s:col0 + s + chunk])
            for t in range(0, chunk, LANES):
                dst_ref[:, s + t:s + t + LANES] = epilogue(acc[:, t:t + LANES]).astype(BF16)

    rope_a = lambda x: _rope128(x, cos_a_ref[...], sin_a_ref[...])
    rope_b = lambda x: _rope64(x, cos_b_ref[...], sin_lo_ref[...], sin_hi_ref[...])

    @pl.when(n == 0)
    def _():
        project(qa_ref, rope_a, 0)
        project(ka_ref, rope_a, qa_ref.shape[1])

    @pl.when(n == 1)
    def _():
        project(va_ref, identity, 0)
        project(qb_ref, rope_b, va_ref.shape[1])
        kv = _dot(h, wkv_ref[...])
        kb_ref[...] = rope_b(kv[:, :LANES]).astype(BF16)
        vb_ref[...] = kv[:, LANES:].astype(BF16)

    @pl.when(n == 2)
    def _():
        project(ga_ref, identity, 0)

    @pl.when(n == 3)
    def _():
        project(gb_ref, identity, 0)


def _in_projection(x2d, norm_w, shift, scale, w_main, w_kv, tables, *, seq):
    m, d = x2d.shape
    tm, tn, half = 512, 2048, 1024
    n_tiles = w_main.shape[1] // tn
    assert n_tiles == 4 and d == tn
    tiles_per_seq = seq // tm
    per_batch = lambda i, n: (i // tiles_per_seq, 0, 0)
    pos = lambda i, n: (i % tiles_per_seq, 0)
    full = lambda i, n: (i, 0)
    wide = lambda cols: jax.ShapeDtypeStruct((m, cols), BF16)
    table_spec = pl.BlockSpec((tm, LANES), pos)
    return pl.pallas_call(
        _proj_kernel,
        grid=(m // tm, n_tiles),
        in_specs=[
            pl.BlockSpec((tm, d), full),
            pl.BlockSpec((1, d), lambda i, n: (0, 0)),
            pl.BlockSpec((None, 1, d), per_batch),
            pl.BlockSpec((None, 1, d), per_batch),
            pl.BlockSpec((d, tn), lambda i, n: (0, n)),
            pl.BlockSpec((d, 2 * LANES), lambda i, n: (0, 0)),
            table_spec, table_spec, table_spec, table_spec, table_spec,
        ],
        out_specs=[
            pl.BlockSpec((tm, half), full),
            pl.BlockSpec((tm, half), full),
            pl.BlockSpec((tm, half), full),
            pl.BlockSpec((tm, half), full),
            pl.BlockSpec((tm, LANES), full),
            pl.BlockSpec((tm, LANES), full),
            pl.BlockSpec((tm, tn), full),
            pl.BlockSpec((tm, tn), full),
        ],
        out_shape=[wide(half), wide(half), wide(half), wide(half), wide(LANES), wide(LANES),
                   wide(tn), wide(tn)],
        scratch_shapes=[pltpu.VMEM((tm, d), BF16), pltpu.VMEM((tm, LANES), F32)],
        compiler_params=pltpu.CompilerParams(
            dimension_semantics=("arbitrary", "arbitrary"), vmem_limit_bytes=VMEM_LIMIT),
        name="in_projection",
    )(x2d, norm_w.reshape(1, d), shift, scale, w_main, w_kv, *tables)


def _rope_tables(seq):
    pos = jnp.arange(seq, dtype=F32)[:, None]

    def cos_sin(half):
        inv = ROPE_THETA ** (-jnp.arange(half, dtype=F32) / half)
        ang = pos * inv[None, :]
        return jnp.cos(ang), jnp.sin(ang)

    cos, sin = cos_sin(MOBA_HEAD_DIM // 2)
    cos_a = jnp.concatenate([cos, cos], axis=1)
    sin_a = jnp.concatenate([-sin, sin], axis=1)
    cos, sin = cos_sin(SWA_HEAD_DIM // 2)
    zero = jnp.zeros_like(sin)
    cos_b = jnp.concatenate([cos, cos, cos, cos], axis=1)
    sin_lo = jnp.concatenate([-sin, zero, -sin, zero], axis=1)
    sin_hi = jnp.concatenate([zero, sin, zero, sin], axis=1)
    return cos_a, sin_a, cos_b, sin_lo, sin_hi


def _moba_kernel(q_ref, k_ref, v_ref, o_ref, kaug_scr, vt_scr, kmean_scr, s_scr, *, topk, group):
    i = pl.program_id(2)
    blk = q_ref.shape[0]
    heads, nb, dh, _ = vt_scr.shape
    exp2_scale = dh ** -0.5 * LOG2_E
    head_lanes = lambda h: slice(h * dh, (h + 1) * dh)

    @pl.when(i == 0)
    def _():
        col = lax.broadcasted_iota(jnp.int32, (blk, LANES), 1)
        for h in range(heads):
            for j in range(nb):
                k = k_ref[j * blk:(j + 1) * blk, head_lanes(h)]
                kmean_scr[h, j:j + 1, :] = jnp.mean(k.astype(F32), axis=0, keepdims=True)
                kaug_scr[h, j, :, :dh] = k
                kaug_scr[h, j, :, dh:] = jnp.where(col == j, 1.0, 0.0).astype(BF16)
                v = v_ref[j * blk:(j + 1) * blk, head_lanes(h)]
                vt_scr[h, j] = v.astype(F32).T.astype(BF16)

    def gate(h):
        q = q_ref[:, head_lanes(h)]
        gs = _dot_nt(kmean_scr[h].astype(BF16), q)
        blk_id = lax.broadcasted_iota(jnp.int32, gs.shape, 0)
        valid = blk_id < i
        vals = jnp.where(valid, gs, -jnp.inf)
        sel = jnp.zeros(gs.shape, dtype=jnp.bool_)
        for _ in range(topk):
            best = jnp.max(vals, axis=0, keepdims=True)
            first = jnp.min(jnp.where(vals == best, blk_id, nb), axis=0, keepdims=True)
            pick = blk_id == first
            sel = sel | pick
            vals = jnp.where(pick, -jnp.inf, vals)
        sel = sel & valid
        bias_t = jnp.where(sel, 0.0, MASK_VALUE).astype(F32)
        bias_t = jnp.concatenate([bias_t, jnp.zeros((LANES - nb, blk), F32)], axis=0)
        q_t = q.astype(F32).T.astype(BF16)
        return jnp.concatenate([q_t, bias_t.astype(BF16)], axis=0)

    q_aug_t = [gate(h) for h in range(heads)]

    def attend(n_past):
        own_slot = n_past
        r = lax.broadcasted_iota(jnp.int32, (blk, blk), 0)
        c = lax.broadcasted_iota(jnp.int32, (blk, blk), 1)
        m = []
        for h in range(heads):
            s = _dot(kaug_scr[h, i, :, :dh], q_aug_t[h][:dh]) * exp2_scale
            s = jnp.where(r <= c, s, MASK_VALUE)
            s_scr[h, own_slot] = s
            mh = jnp.max(s, axis=0, keepdims=True)
            for j in range(n_past):
                s = _dot(kaug_scr[h, j], q_aug_t[h]) * exp2_scale
                s_scr[h, j] = s
                mh = jnp.maximum(mh, jnp.max(s, axis=0, keepdims=True))
            m.append(mh)

        for h in range(heads):
            l = jnp.zeros_like(m[h])
            acc = jnp.zeros((dh, blk), F32)
            for slot in range(n_past + 1):
                p = jnp.exp2(s_scr[h, slot] - m[h])
                l = l + jnp.sum(p, axis=0, keepdims=True)
                vt = vt_scr[h, i] if slot == own_slot else vt_scr[h, slot]
                acc = acc + _dot(vt, p.astype(BF16))
            o_ref[:, head_lanes(h)] = (acc / l).T.astype(o_ref.dtype)

    n_groups = (i + group - 1) // group
    for n in range(nb // group + 1):
        pl.when(n_groups == n)(functools.partial(attend, n * group))


def _moba_attention(q, k, v, *, heads):
    b, seq, _ = q.shape
    dh, blk = MOBA_HEAD_DIM, MOBA_BLOCK
    nb = seq // blk
    hps = MOBA_HEADS_PER_STEP
    q_spec = pl.BlockSpec((None, blk, hps * dh), lambda bi, h, i: (bi, i, h))
    kv_spec = pl.BlockSpec((None, seq, hps * dh), lambda bi, h, i: (bi, 0, h))
    return pl.pallas_call(
        functools.partial(_moba_kernel, topk=min(MOBA_TOPK, nb), group=MOBA_LOOP_GROUP),
        grid=(b, heads // hps, nb),
        in_specs=[q_spec, kv_spec, kv_spec],
        out_specs=q_spec,
        out_shape=jax.ShapeDtypeStruct(q.shape, BF16),
        scratch_shapes=[pltpu.VMEM((hps, nb, blk, dh + LANES), BF16),
                        pltpu.VMEM((hps, nb, dh, blk), BF16),
                        pltpu.VMEM((hps, nb, dh), F32),
                        pltpu.VMEM((hps, nb + 1, blk, blk), F32)],
        compiler_params=pltpu.CompilerParams(
            dimension_semantics=("arbitrary", "arbitrary", "arbitrary"),
            vmem_limit_bytes=VMEM_LIMIT),
        name="moba_attention",
    )(q, k, v)


def _swa_kernel(sink_ref, q_ref, kp_ref, kc_ref, vp_ref, vc_ref, o_ref, k_scr, v_scr, *, q_lane_blocks):
    i = pl.program_id(1)
    w = kp_ref.shape[0]
    tq = q_ref.shape[0]
    sub_blocks = tq // w
    half = LANES // 2
    scale = half ** -0.5
    groups = q_lane_blocks // SWA_KV_HEADS

    k_scr[:w, :] = kp_ref[...]
    k_scr[w:, :] = kc_ref[...]
    v_scr[:w, :] = vp_ref[...]
    v_scr[w:, :] = vc_ref[...]

    rows = lax.broadcasted_iota(jnp.int32, (2 * w, 2 * w), 0)
    qi = jnp.where(rows >= w, rows - w, rows)
    kc = lax.broadcasted_iota(jnp.int32, (2 * w, 2 * w), 1)
    band = (kc > qi) & (kc <= qi + w)
    lane = lax.broadcasted_iota(jnp.int32, (w, LANES), 1)
    low_lanes = lane < half
    top_rows = lax.broadcasted_iota(jnp.int32, (2 * w, 1), 0) < w

    def sub_block(u, carry):
        start = pl.multiple_of(u * w, w)
        first_key = jnp.where((i * sub_blocks + u) == 0, w, 0)
        valid = band & (kc >= first_key)
        k2 = k_scr[pl.ds(start, 2 * w), :]
        v2 = v_scr[pl.ds(start, 2 * w), :]
        for kvh in range(SWA_KV_HEADS):
            kd = jnp.concatenate([k2[:, kvh * half:(kvh + 1) * half]] * 2, axis=1)
            vd = jnp.concatenate([v2[:, kvh * half:(kvh + 1) * half]] * 2, axis=1)
            for g in range(groups):
                jb = kvh * groups + g
                qblk = q_ref[pl.ds(start, w), jb * LANES:(jb + 1) * LANES]
                zero = jnp.zeros_like(qblk)
                q2 = jnp.concatenate([jnp.where(low_lanes, qblk, zero),
                                      jnp.where(low_lanes, zero, qblk)], axis=0)
                s = _dot_nt(q2, kd) * scale
                s = jnp.where(valid, s, MASK_VALUE)
                sink = jnp.where(top_rows, sink_ref[2 * jb], sink_ref[2 * jb + 1])
                m = jnp.maximum(jnp.max(s, axis=-1, keepdims=True), sink)
                p = jnp.exp(s - m)
                den = jnp.sum(p, axis=-1, keepdims=True) + jnp.exp(sink - m)
                o = _dot(p.astype(BF16), vd) / den
                o_ref[pl.ds(start, w), jb * LANES:(jb + 1) * LANES] = jnp.where(
                    low_lanes, o[:w], o[w:]).astype(o_ref.dtype)
        return carry

    lax.fori_loop(0, sub_blocks, sub_block, 0)


def _swa_attention(q, k, v, sinks):
    b, seq, qw = q.shape
    w = SWA_WINDOW
    tq = 512
    sub_blocks = tq // w
    q_spec = pl.BlockSpec((None, tq, qw), lambda bi, i: (bi, i, 0))
    cur = pl.BlockSpec((None, tq, LANES), lambda bi, i: (bi, i, 0))
    prev = pl.BlockSpec((None, w, LANES), lambda bi, i: (bi, jnp.maximum(i * sub_blocks - 1, 0), 0))
    return pl.pallas_call(
        functools.partial(_swa_kernel, q_lane_blocks=qw // LANES),
        grid=(b, seq // tq),
        in_specs=[pl.BlockSpec(memory_space=pltpu.SMEM), q_spec, prev, cur, prev, cur],
        out_specs=q_spec,
        out_shape=jax.ShapeDtypeStruct(q.shape, BF16),
        scratch_shapes=[pltpu.VMEM((tq + w, LANES), BF16), pltpu.VMEM((tq + w, LANES), BF16)],
        compiler_params=pltpu.CompilerParams(
            dimension_semantics=("arbitrary", "arbitrary"), vmem_limit_bytes=VMEM_LIMIT),
        name="swa_attention",
    )(sinks, q, k, k, v, v)


def _merge_kernel(ya_ref, yb_ref, ga_ref, gb_ref, x_ref, g_ref, wa_ref, wb_ref, wo_ref, o_ref):
    merged = (_sigmoid(ga_ref[...].astype(F32)) * _dot(ya_ref[...], wa_ref[...])
              + _sigmoid(gb_ref[...].astype(F32)) * _dot(yb_ref[...], wb_ref[...]))
    o_ref[...] = x_ref[...] + g_ref[...] * _dot(merged.astype(BF16), wo_ref[...])


def _merge(ya, yb, ga, gb, x2d, gate, wa, wb, wo, *, seq):
    m, d = x2d.shape
    tm = 256
    tiles_per_seq = seq // tm
    row = lambda i: (i, 0)
    const = lambda i: (0, 0)
    resident = lambda shape: pl.BlockSpec(shape, const, pipeline_mode=pl.Buffered(1))
    return pl.pallas_call(
        _merge_kernel,
        grid=(m // tm,),
        in_specs=[
            pl.BlockSpec((tm, ya.shape[1]), row),
            pl.BlockSpec((tm, yb.shape[1]), row),
            pl.BlockSpec((tm, d), row),
            pl.BlockSpec((tm, d), row),
            pl.BlockSpec((tm, d), row),
            pl.BlockSpec((None, 1, d), lambda i: (i // tiles_per_seq, 0, 0)),
            resident(wa.shape), resident(wb.shape), resident(wo.shape),
        ],
        out_specs=pl.BlockSpec((tm, d), row),
        out_shape=jax.ShapeDtypeStruct((m, d), F32),
        compiler_params=pltpu.CompilerParams(
            dimension_semantics=("arbitrary",), vmem_limit_bytes=VMEM_LIMIT),
        name="merge_out_projection",
    )(ya, yb, ga, gb, x2d, gate, wa, wb, wo)


def kernel(x, c, w_ada, b_ada, norm_ffn1, ffn1_gate, ffn1_up, ffn1_down, norm_mix, w_in, swa_sinks,
           w_branch_moba, w_branch_swa, w_out, norm_ffn2, ffn2_gate, ffn2_up, ffn2_down, norm_final):
    b, seq, d = x.shape
    depth = w_ada.shape[0]
    moba_w = w_branch_moba.shape[1]
    swa_qw = w_branch_swa.shape[1]
    kv_w = SWA_KV_HEADS * SWA_HEAD_DIM
    qkv_cols = 3 * moba_w + swa_qw
    assert w_in.shape[2] == qkv_cols + 2 * kv_w + 2 * d
    assert moba_w == swa_qw == 1024 and d == 2048 and kv_w == LANES

    tables = _rope_tables(seq)
    c_pad = jnp.pad(c, ((0, 8 - b), (0, 0)))
    x2d = x.reshape(b * seq, d)

    for l in range(depth):
        mod = _ada_modulation(c_pad, w_ada[l], b_ada[l])[:b].reshape(b, N_ADA, 1, d)
        sh1, sc1, g1, sh2, sc2, g2, sh3, sc3, g3 = [mod[:, t] for t in range(N_ADA)]

        x2d = _ffn(x2d, norm_ffn1[l], sh1, sc1, g1,
                   ffn1_gate[l].astype(BF16), ffn1_up[l].astype(BF16), ffn1_down[l].astype(BF16),
                   norm_final, seq=seq, final_norm=False)

        w_l = w_in[l]
        w_main = jnp.concatenate([w_l[:, :qkv_cols], w_l[:, qkv_cols + 2 * kv_w:]], axis=1).astype(BF16)
        w_kv = w_l[:, qkv_cols:qkv_cols + 2 * kv_w].astype(BF16)
        qa, ka, va, qb, kb, vb, ga, gb = _in_projection(
            x2d, norm_mix[l], sh2, sc2, w_main, w_kv, tables, seq=seq)

        rs = lambda t: t.reshape(b, seq, t.shape[-1])
        ya = _moba_attention(rs(qa), rs(ka), rs(va), heads=moba_w // MOBA_HEAD_DIM)
        yb = _swa_attention(rs(qb), rs(kb), rs(vb), swa_sinks[l])

        x2d = _merge(ya.reshape(b * seq, moba_w), yb.reshape(b * seq, swa_qw), ga, gb, x2d, g2,
                     w_branch_moba[l].astype(BF16), w_branch_swa[l].astype(BF16),
                     w_out[l].astype(BF16), seq=seq)

        x2d = _ffn(x2d, norm_ffn2[l], sh3, sc3, g3,
                   ffn2_gate[l].astype(BF16), ffn2_up[l].astype(BF16), ffn2_down[l].astype(BF16),
                   norm_final, seq=seq, final_norm=(l == depth - 1))

    return x2d.reshape(b, seq, d)
```

```python
import functools

import jax
import jax.numpy as jnp
from jax import lax
from jax.experimental import pallas as pl
from jax.experimental.pallas import tpu as pltpu

MOBA_HEAD_DIM = 128
MOBA_BLOCK = 256
MOBA_TOPK = 3
SWA_HEAD_DIM = 64
SWA_KV_HEADS = 2
SWA_WINDOW = 128
ROPE_THETA = 10000.0
EPS = 1e-6
N_ADA = 9
VEC_NORM_FFN1_ROW, VEC_NORM_MIX_ROW, VEC_NORM_FFN2_ROW, VEC_FINAL_NORM_ROW = 9, 10, 11, 12
VEC_ROWS = 16

LANES = 128
ROW_CHUNK = 16
VMEM_LIMIT = 56 * 1024 * 1024
MASK_VALUE = -1e30
LOG2_E = 1.4426950408889634
MOBA_LOOP_GROUP = 2
MOBA_HEADS_PER_STEP = 4

F32 = jnp.float32
BF16 = jnp.bfloat16


def _sigmoid(x):
    return 1.0 / (1.0 + jnp.exp(-x))


def _row_rsqrt_mean_square(x):
    r = lax.rsqrt(jnp.mean(x * x, axis=-1, keepdims=True) + EPS)
    return jnp.broadcast_to(r, (x.shape[0], LANES))


def _across_lanes(r, width):
    return jnp.concatenate([r] * (width // LANES), axis=1)


def _for_row_chunks(rows, body):
    def step(c, carry):
        body(pl.ds(pl.multiple_of(c * ROW_CHUNK, ROW_CHUNK), ROW_CHUNK))
        return carry
    lax.fori_loop(0, rows // ROW_CHUNK, step, 0, unroll=4)


def _rms_modulate_to(h_ref, x_ref, rinv_scr, norm_w, shift, scale):
    rinv_scr[...] = _row_rsqrt_mean_square(x_ref[...])
    gain = norm_w * (1.0 + scale)

    def slab(rows):
        rinv = _across_lanes(rinv_scr[rows, :], x_ref.shape[1])
        h_ref[rows, :] = (x_ref[rows, :] * rinv * gain + shift).astype(BF16)

    _for_row_chunks(x_ref.shape[0], slab)


def _rms_modulate_slab(h_ref, x_ref, rows, norm_w, shift, scale):
    x = x_ref[rows, :]
    rinv = lax.rsqrt(jnp.mean(x * x, axis=-1, keepdims=True) + EPS)
    h_ref[rows, :] = (x * rinv * (norm_w * (1.0 + scale)) + shift).astype(BF16)


def _dot(a, b):
    return jnp.dot(a, b, preferred_element_type=F32)


def _dot_nt(a, b):
    return lax.dot_general(a, b, (((1,), (1,)), ((), ())), preferred_element_type=F32)


def _cast_kernel(*refs):
    n = len(refs) // 2
    for src, dst in zip(refs[:n], refs[n:]):
        dst[...] = src[...].astype(dst.dtype)


def _cast_bf16(arrays, block_rows):
    rows, cols = arrays[0].shape
    spec = pl.BlockSpec((block_rows, cols), lambda r: (r, 0))
    n = len(arrays)
    return pl.pallas_call(
        _cast_kernel,
        grid=(rows // block_rows,),
        in_specs=[spec] * n,
        out_specs=[spec] * n,
        out_shape=[jax.ShapeDtypeStruct((rows, cols), BF16)] * n,
        compiler_params=pltpu.CompilerParams(
            dimension_semantics=("arbitrary",), vmem_limit_bytes=VMEM_LIMIT),
        name="cast_bf16",
    )(*arrays)


def _split_w_in_kernel(w_ref, main_ref, kv_ref, *, kv_block):
    j = pl.program_id(0)

    @pl.when(j != kv_block)
    def _():
        main_ref[...] = w_ref[...].astype(BF16)

    @pl.when(j == kv_block)
    def _():
        kv_ref[...] = w_ref[...].astype(BF16)


def _split_w_in(w, kv_start, kv_width):
    d, cols = w.shape
    assert kv_start % kv_width == 0 and cols % kv_width == 0
    kv_block = kv_start // kv_width
    main_col = lambda j: (0, jnp.where(j > kv_block, j - 1, jnp.minimum(j, kv_block - 1)))
    return pl.pallas_call(
        functools.partial(_split_w_in_kernel, kv_block=kv_block),
        grid=(cols // kv_width,),
        in_specs=[pl.BlockSpec((d, kv_width), lambda j: (0, j))],
        out_specs=[pl.BlockSpec((d, kv_width), main_col),
                   pl.BlockSpec((d, kv_width), lambda j: (0, 0))],
        out_shape=[jax.ShapeDtypeStruct((d, cols - kv_width), BF16),
                   jax.ShapeDtypeStruct((d, kv_width), BF16)],
        compiler_params=pltpu.CompilerParams(
            dimension_semantics=("arbitrary",), vmem_limit_bytes=VMEM_LIMIT),
        name="split_w_in",
    )(w)


def _ada_kernel(c_ref, w_ref, b_ref, o_ref):
    c = c_ref[...]
    s = (c * _sigmoid(c)).astype(BF16)
    o_ref[...] = _dot(s, w_ref[...].astype(BF16)) + b_ref[...]


def _ada_modulation(c_pad, w, b):
    rows, d = c_pad.shape
    n = w.shape[1]
    tn = 1024
    return pl.pallas_call(
        _ada_kernel,
        grid=(n // tn,),
        in_specs=[
            pl.BlockSpec((rows, d), lambda j: (0, 0)),
            pl.BlockSpec((d, tn), lambda j: (0, j)),
            pl.BlockSpec((1, tn), lambda j: (0, j)),
        ],
        out_specs=pl.BlockSpec((rows, tn), lambda j: (0, j)),
        out_shape=jax.ShapeDtypeStruct((rows, n), F32),
        compiler_params=pltpu.CompilerParams(
            dimension_semantics=("arbitrary",), vmem_limit_bytes=VMEM_LIMIT),
        name="ada_modulation",
    )(c_pad, w, b.reshape(1, n))


def _ffn_kernel(x_ref, xn_ref, vec_ref, wg_ref, wu_ref, wd_ref, o_ref, h_scr, rinv_scr, *,
                final_norm, slab_rows, tiles_per_seq, norm_row, ada_row):
    i = pl.program_id(0)
    f = pl.program_id(1)
    nf = pl.num_programs(1)
    tm = x_ref.shape[0]
    cur = i % 2
    batch = i // tiles_per_seq
    batch_next = jnp.minimum(i + 1, pl.num_programs(0) - 1) // tiles_per_seq
    vec = lambda b, r: vec_ref[b, r:r + 1, :]

    @pl.when((i == 0) & (f == 0))
    def _():
        _rms_modulate_to(h_scr.at[0], x_ref, rinv_scr, vec(0, norm_row), vec(0, ada_row),
                         vec(0, ada_row + 1))

    def step(first):
        slab = jnp.minimum(f, tm // slab_rows - 1)
        rows = pl.ds(pl.multiple_of(slab * slab_rows, slab_rows), slab_rows)
        h = h_scr[cur]
        gate = _dot(h, wg_ref[...])
        up = _dot(h, wu_ref[...])
        act = (gate * _sigmoid(gate) * up).astype(BF16)
        update = (0.5 * vec(batch, ada_row + 2)) * _dot(act, wd_ref[...])
        base = x_ref[...] if first else o_ref[...]
        o_ref[...] = base + update
        _rms_modulate_slab(h_scr.at[1 - cur], xn_ref, rows, vec(batch_next, norm_row),
                           vec(batch_next, ada_row), vec(batch_next, ada_row + 1))

    pl.when(f == 0)(functools.partial(step, True))
    pl.when(f > 0)(functools.partial(step, False))

    if final_norm:
        @pl.when(f == nf - 1)
        def _():
            rinv_scr[...] = _row_rsqrt_mean_square(o_ref[...])
            final_w = vec(0, VEC_FINAL_NORM_ROW)

            def slab(rows):
                rinv = _across_lanes(rinv_scr[rows, :], o_ref.shape[1])
                o_ref[rows, :] = o_ref[rows, :] * rinv * final_w

            _for_row_chunks(tm, slab)


def _ffn(x2d, vec, wg, wu, wd, *, seq, norm_row, ada_row, final_norm):
    m, d = x2d.shape
    dff = wg.shape[1]
    tm, tf = 512, 512
    slab_rows = 64
    n_tiles, n_ff = m // tm, dff // tf
    assert n_ff >= tm // slab_rows
    row = lambda i, f: (i, 0)
    return pl.pallas_call(
        functools.partial(_ffn_kernel, final_norm=final_norm, slab_rows=slab_rows,
                          tiles_per_seq=seq // tm, norm_row=norm_row, ada_row=ada_row),
        grid=(n_tiles, n_ff),
        in_specs=[
            pl.BlockSpec((tm, d), row),
            pl.BlockSpec((tm, d), lambda i, f: (jnp.minimum(i + 1, n_tiles - 1), 0)),
            pl.BlockSpec(vec.shape, lambda i, f: (0, 0, 0)),
            pl.BlockSpec((d, tf), lambda i, f: (0, f)),
            pl.BlockSpec((d, tf), lambda i, f: (0, f)),
            pl.BlockSpec((tf, d), lambda i, f: (f, 0)),
        ],
        out_specs=pl.BlockSpec((tm, d), row),
        out_shape=jax.ShapeDtypeStruct((m, d), F32),
        scratch_shapes=[pltpu.VMEM((2, tm, d), BF16), pltpu.VMEM((tm, LANES), F32)],
        compiler_params=pltpu.CompilerParams(
            dimension_semantics=("arbitrary", "arbitrary"), vmem_limit_bytes=VMEM_LIMIT),
        name="ffn_final" if final_norm else "ffn",
    )(x2d, x2d, vec, wg, wu, wd)


def _rope128(x, cos, sin):
    return x * cos + pltpu.roll(x, 64, 1) * sin


def _rope64(x, cos, sin_lo, sin_hi):
    return x * cos + pltpu.roll(x, 96, 1) * sin_lo + pltpu.roll(x, 32, 1) * sin_hi


def _proj_kernel(x_ref, nw_ref, sh_ref, sc_ref, w_ref, wkv_ref,
                 cos_a_ref, sin_a_ref, cos_b_ref, sin_lo_ref, sin_hi_ref,
                 qa_ref, ka_ref, va_ref, qb_ref, kb_ref, vb_ref, ga_ref, gb_ref, h_scr, rinv_scr):
    n = pl.program_id(1)

    @pl.when(n == 0)
    def _():
        _rms_modulate_to(h_scr, x_ref, rinv_scr, nw_ref[...], sh_ref[...], sc_ref[...])

    h = h_scr[...]
    chunk = 2 * LANES
    identity = lambda x: x

    def project(dst_ref, epilogue, col0):
        for s in range(0, dst_ref.shape[1], chunk):
            acc = _dot(h, w_ref[:, col0 + s:col0 + s + chunk])
            for t in range(0, chunk, LANES):
                dst_ref[:, s + t:s + t + LANES] = epilogue(acc[:, t:t + LANES]).astype(BF16)

    rope_a = lambda x: _rope128(x, cos_a_ref[...], sin_a_ref[...])
    rope_b = lambda x: _rope64(x, cos_b_ref[...], sin_lo_ref[...], sin_hi_ref[...])

    @pl.when(n == 0)
    def _():
        project(qa_ref, rope_a, 0)
        project(ka_ref, rope_a, qa_ref.shape[1])

    @pl.when(n == 1)
    def _():
        project(va_ref, identity, 0)
        project(qb_ref, rope_b, va_ref.shape[1])
        kv = _dot(h, wkv_ref[...])
        kb_ref[...] = rope_b(kv[:, :LANES]).astype(BF16)
        vb_ref[...] = kv[:, LANES:].astype(BF16)

    @pl.when(n == 2)
    def _():
        project(ga_ref, identity, 0)

    @pl.when(n == 3)
    def _():
        project(gb_ref, identity, 0)


def _in_projection(x2d, norm_w, shift, scale, w_main, w_kv, tables, *, seq):
    m, d = x2d.shape
    tm, tn, half = 512, 2048, 1024
    n_tiles = w_main.shape[1] // tn
    assert n_tiles == 4 and d == tn
    tiles_per_seq = seq // tm
    per_batch = lambda i, n: (i // tiles_per_seq, 0, 0)
    pos = lambda i, n: (i % tiles_per_seq, 0)
    full = lambda i, n: (i, 0)
    wide = lambda cols: jax.ShapeDtypeStruct((m, cols), BF16)
    table_spec = pl.BlockSpec((tm, LANES), pos)
    return pl.pallas_call(
        _proj_kernel,
        grid=(m // tm, n_tiles),
        in_specs=[
            pl.BlockSpec((tm, d), full),
            pl.BlockSpec((1, d), lambda i, n: (0, 0)),
            pl.BlockSpec((None, 1, d), per_batch),
            pl.BlockSpec((None, 1, d), per_batch),
            pl.BlockSpec((d, tn), lambda i, n: (0, n)),
            pl.BlockSpec((d, 2 * LANES), lambda i, n: (0, 0)),
            table_spec, table_spec, table_spec, table_spec, table_spec,
        ],
        out_specs=[
            pl.BlockSpec((tm, half), full),
            pl.BlockSpec((tm, half), full),
            pl.BlockSpec((tm, half), full),
            pl.BlockSpec((tm, half), full),
            pl.BlockSpec((tm, LANES), full),
            pl.BlockSpec((tm, LANES), full),
            pl.BlockSpec((tm, tn), full),
            pl.BlockSpec((tm, tn), full),
        ],
        out_shape=[wide(half), wide(half), wide(half), wide(half), wide(LANES), wide(LANES),
                   wide(tn), wide(tn)],
        scratch_shapes=[pltpu.VMEM((tm, d), BF16), pltpu.VMEM((tm, LANES), F32)],
        compiler_params=pltpu.CompilerParams(
            dimension_semantics=("arbitrary", "arbitrary"), vmem_limit_bytes=VMEM_LIMIT),
        name="in_projection",
    )(x2d, norm_w.reshape(1, d), shift, scale, w_main, w_kv, *tables)


def _rope_tables(seq):
    pos = jnp.arange(seq, dtype=F32)[:, None]

    def cos_sin(half):
        inv = ROPE_THETA ** (-jnp.arange(half, dtype=F32) / half)
        ang = pos * inv[None, :]
        return jnp.cos(ang), jnp.sin(ang)

    cos, sin = cos_sin(MOBA_HEAD_DIM // 2)
    cos_a = jnp.concatenate([cos, cos], axis=1)
    sin_a = jnp.concatenate([-sin, sin], axis=1)
    cos, sin = cos_sin(SWA_HEAD_DIM // 2)
    zero = jnp.zeros_like(sin)
    cos_b = jnp.concatenate([cos, cos, cos, cos], axis=1)
    sin_lo = jnp.concatenate([-sin, zero, -sin, zero], axis=1)
    sin_hi = jnp.concatenate([zero, sin, zero, sin], axis=1)
    return cos_a, sin_a, cos_b, sin_lo, sin_hi


def _moba_kernel(q_ref, k_ref, v_ref, o_ref, kaug_scr, vt_scr, kmean_scr, s_scr, *, topk, group):
    i = pl.program_id(2)
    blk = q_ref.shape[0]
    heads, nb, dh, _ = vt_scr.shape
    exp2_scale = dh ** -0.5 * LOG2_E
    head_lanes = lambda h: slice(h * dh, (h + 1) * dh)

    @pl.when(i == 0)
    def _():
        col = lax.broadcasted_iota(jnp.int32, (blk, LANES), 1)
        for h in range(heads):
            for j in range(nb):
                k = k_ref[j * blk:(j + 1) * blk, head_lanes(h)]
                kmean_scr[h, j:j + 1, :] = jnp.mean(k.astype(F32), axis=0, keepdims=True)
                kaug_scr[h, j, :, :dh] = k
                kaug_scr[h, j, :, dh:] = jnp.where(col == j, 1.0, 0.0).astype(BF16)
                v = v_ref[j * blk:(j + 1) * blk, head_lanes(h)]
                vt_scr[h, j] = v.astype(F32).T.astype(BF16)

    def gate(h):
        q = q_ref[:, head_lanes(h)]
        gs = _dot_nt(kmean_scr[h].astype(BF16), q)
        blk_id = lax.broadcasted_iota(jnp.int32, gs.shape, 0)
        valid = blk_id < i
        vals = jnp.where(valid, gs, -jnp.inf)
        sel = jnp.zeros(gs.shape, dtype=jnp.bool_)
        for _ in range(topk):
            best = jnp.max(vals, axis=0, keepdims=True)
            first = jnp.min(jnp.where(vals == best, blk_id, nb), axis=0, keepdims=True)
            pick = blk_id == first
            sel = sel | pick
            vals = jnp.where(pick, -jnp.inf, vals)
        sel = sel & valid
        bias_t = jnp.where(sel, 0.0, MASK_VALUE).astype(F32)
        bias_t = jnp.concatenate([bias_t, jnp.zeros((LANES - nb, blk), F32)], axis=0)
        q_t = q.astype(F32).T.astype(BF16)
        return jnp.concatenate([q_t, bias_t.astype(BF16)], axis=0)

    q_aug_t = [gate(h) for h in range(heads)]

    def attend(n_past):
        own_slot = n_past
        r = lax.broadcasted_iota(jnp.int32, (blk, blk), 0)
        c = lax.broadcasted_iota(jnp.int32, (blk, blk), 1)
        m = []
        for h in range(heads):
            s = _dot(kaug_scr[h, i, :, :dh], q_aug_t[h][:dh]) * exp2_scale
            s = jnp.where(r <= c, s, MASK_VALUE)
            s_scr[h, own_slot] = s
            mh = jnp.max(s, axis=0, keepdims=True)
            for j in range(n_past):
                s = _dot(kaug_scr[h, j], q_aug_t[h]) * exp2_scale
                s_scr[h, j] = s
                mh = jnp.maximum(mh, jnp.max(s, axis=0, keepdims=True))
            m.append(mh)

        for h in range(heads):
            l = jnp.zeros_like(m[h])
            acc = jnp.zeros((dh, blk), F32)
            for slot in range(n_past + 1):
                p = jnp.exp2(s_scr[h, slot] - m[h])
                l = l + jnp.sum(p, axis=0, keepdims=True)
                vt = vt_scr[h, i] if slot == own_slot else vt_scr[h, slot]
                acc = acc + _dot(vt, p.astype(BF16))
            o_ref[:, head_lanes(h)] = (acc / l).T.astype(o_ref.dtype)

    n_groups = (i + group - 1) // group
    for n in range(nb // group + 1):
        pl.when(n_groups == n)(functools.partial(attend, n * group))


def _moba_attention(q, k, v, *, heads):
    b, seq, _ = q.shape
    dh, blk = MOBA_HEAD_DIM, MOBA_BLOCK
    nb = seq // blk
    hps = MOBA_HEADS_PER_STEP
    q_spec = pl.BlockSpec((None, blk, hps * dh), lambda bi, h, i: (bi, i, h))
    kv_spec = pl.BlockSpec((None, seq, hps * dh), lambda bi, h, i: (bi, 0, h))
    return pl.pallas_call(
        functools.partial(_moba_kernel, topk=min(MOBA_TOPK, nb), group=MOBA_LOOP_GROUP),
        grid=(b, heads // hps, nb),
        in_specs=[q_spec, kv_spec, kv_spec],
        out_specs=q_spec,
        out_shape=jax.ShapeDtypeStruct(q.shape, BF16),
        scratch_shapes=[pltpu.VMEM((hps, nb, blk, dh + LANES), BF16),
                        pltpu.VMEM((hps, nb, dh, blk), BF16),
                        pltpu.VMEM((hps, nb, dh), F32),
                        pltpu.VMEM((hps, nb + 1, blk, blk), F32)],
        compiler_params=pltpu.CompilerParams(
            dimension_semantics=("arbitrary", "arbitrary", "arbitrary"),
            vmem_limit_bytes=VMEM_LIMIT),
        name="moba_attention",
    )(q, k, v)


def _swa_kernel(sink_ref, q_ref, kp_ref, kc_ref, vp_ref, vc_ref, o_ref, k_scr, v_scr, *, q_lane_blocks):
    i = pl.program_id(1)
    w = kp_ref.shape[0]
    tq = q_ref.shape[0]
    sub_blocks = tq // w
    half = LANES // 2
    scale = half ** -0.5
    groups = q_lane_blocks // SWA_KV_HEADS

    k_scr[:w, :] = kp_ref[...]
    k_scr[w:, :] = kc_ref[...]
    v_scr[:w, :] = vp_ref[...]
    v_scr[w:, :] = vc_ref[...]

    rows = lax.broadcasted_iota(jnp.int32, (2 * w, 2 * w), 0)
    qi = jnp.where(rows >= w, rows - w, rows)
    kc = lax.broadcasted_iota(jnp.int32, (2 * w, 2 * w), 1)
    band = (kc > qi) & (kc <= qi + w)
    lane = lax.broadcasted_iota(jnp.int32, (w, LANES), 1)
    low_lanes = lane < half
    top_rows = lax.broadcasted_iota(jnp.int32, (2 * w, 1), 0) < w

    def sub_block(u, carry):
        start = pl.multiple_of(u * w, w)
        first_key = jnp.where((i * sub_blocks + u) == 0, w, 0)
        valid = band & (kc >= first_key)
        k2 = k_scr[pl.ds(start, 2 * w), :]
        v2 = v_scr[pl.ds(start, 2 * w), :]
        for kvh in range(SWA_KV_HEADS):
            kd = jnp.concatenate([k2[:, kvh * half:(kvh + 1) * half]] * 2, axis=1)
            vd = jnp.concatenate([v2[:, kvh * half:(kvh + 1) * half]] * 2, axis=1)
            for g in range(groups):
                jb = kvh * groups + g
                qblk = q_ref[pl.ds(start, w), jb * LANES:(jb + 1) * LANES]
                zero = jnp.zeros_like(qblk)
                q2 = jnp.concatenate([jnp.where(low_lanes, qblk, zero),
                                      jnp.where(low_lanes, zero, qblk)], axis=0)
                s = _dot_nt(q2, kd) * scale
                s = jnp.where(valid, s, MASK_VALUE)
                sink = jnp.where(top_rows, sink_ref[2 * jb], sink_ref[2 * jb + 1])
                m = jnp.maximum(jnp.max(s, axis=-1, keepdims=True), sink)
                p = jnp.exp(s - m)
                den = jnp.sum(p, axis=-1, keepdims=True) + jnp.exp(sink - m)
                o = _dot(p.astype(BF16), vd) / den
                o_ref[pl.ds(start, w), jb * LANES:(jb + 1) * LANES] = jnp.where(
                    low_lanes, o[:w], o[w:]).astype(o_ref.dtype)
        return carry

    lax.fori_loop(0, sub_blocks, sub_block, 0)


def _swa_attention(q, k, v, sinks):
    b, seq, qw = q.shape
    w = SWA_WINDOW
    tq = 512
    sub_blocks = tq // w
    q_spec = pl.BlockSpec((None, tq, qw), lambda bi, i: (bi, i, 0))
    cur = pl.BlockSpec((None, tq, LANES), lambda bi, i: (bi, i, 0))
    prev = pl.BlockSpec((None, w, LANES), lambda bi, i: (bi, jnp.maximum(i * sub_blocks - 1, 0), 0))
    return pl.pallas_call(
        functools.partial(_swa_kernel, q_lane_blocks=qw // LANES),
        grid=(b, seq // tq),
        in_specs=[pl.BlockSpec(memory_space=pltpu.SMEM), q_spec, prev, cur, prev, cur],
        out_specs=q_spec,
        out_shape=jax.ShapeDtypeStruct(q.shape, BF16),
        scratch_shapes=[pltpu.VMEM((tq + w, LANES), BF16), pltpu.VMEM((tq + w, LANES), BF16)],
        compiler_params=pltpu.CompilerParams(
            dimension_semantics=("arbitrary", "arbitrary"), vmem_limit_bytes=VMEM_LIMIT),
        name="swa_attention",
    )(sinks, q, k, k, v, v)


def _merge_kernel(ya_ref, yb_ref, ga_ref, gb_ref, x_ref, g_ref, wa_ref, wb_ref, wo_ref, o_ref):
    merged = (_sigmoid(ga_ref[...].astype(F32)) * _dot(ya_ref[...], wa_ref[...])
              + _sigmoid(gb_ref[...].astype(F32)) * _dot(yb_ref[...], wb_ref[...]))
    o_ref[...] = x_ref[...] + g_ref[...] * _dot(merged.astype(BF16), wo_ref[...])


def _merge(ya, yb, ga, gb, x2d, gate, wa, wb, wo, *, seq):
    m, d = x2d.shape
    tm = 256
    tiles_per_seq = seq // tm
    row = lambda i: (i, 0)
    const = lambda i: (0, 0)
    resident = lambda shape: pl.BlockSpec(shape, const, pipeline_mode=pl.Buffered(1))
    return pl.pallas_call(
        _merge_kernel,
        grid=(m // tm,),
        in_specs=[
            pl.BlockSpec((tm, ya.shape[1]), row),
            pl.BlockSpec((tm, yb.shape[1]), row),
            pl.BlockSpec((tm, d), row),
            pl.BlockSpec((tm, d), row),
            pl.BlockSpec((tm, d), row),
            pl.BlockSpec((None, 1, d), lambda i: (i // tiles_per_seq, 0, 0)),
            resident(wa.shape), resident(wb.shape), resident(wo.shape),
        ],
        out_specs=pl.BlockSpec((tm, d), row),
        out_shape=jax.ShapeDtypeStruct((m, d), F32),
        compiler_params=pltpu.CompilerParams(
            dimension_semantics=("arbitrary",), vmem_limit_bytes=VMEM_LIMIT),
        name="merge_out_projection",
    )(ya, yb, ga, gb, x2d, gate, wa, wb, wo)


def kernel(x, c, w_ada, b_ada, norm_ffn1, ffn1_gate, ffn1_up, ffn1_down, norm_mix, w_in, swa_sinks,
           w_branch_moba, w_branch_swa, w_out, norm_ffn2, ffn2_gate, ffn2_up, ffn2_down, norm_final):
    b, seq, d = x.shape
    depth = w_ada.shape[0]
    moba_w = w_branch_moba.shape[1]
    swa_qw = w_branch_swa.shape[1]
    kv_w = SWA_KV_HEADS * SWA_HEAD_DIM
    qkv_cols = 3 * moba_w + swa_qw
    assert w_in.shape[2] == qkv_cols + 2 * kv_w + 2 * d
    assert moba_w == swa_qw == 1024 and d == 2048 and kv_w == LANES

    tables = _rope_tables(seq)
    c_pad = jnp.pad(c, ((0, 8 - b), (0, 0)))
    x2d = x.reshape(b * seq, d)

    for l in range(depth):
        mod = _ada_modulation(c_pad, w_ada[l], b_ada[l])[:b].reshape(b, N_ADA, d)
        sh2, sc2, g2 = [mod[:, t:t + 1] for t in (3, 4, 5)]
        gains = jnp.stack([norm_ffn1[l], norm_mix[l], norm_ffn2[l], norm_final])
        vec = jnp.concatenate(
            [mod, jnp.broadcast_to(gains, (b,) + gains.shape),
             jnp.zeros((b, VEC_ROWS - N_ADA - gains.shape[0], d), F32)], axis=1)

        gate1, up1, gate2, up2 = _cast_bf16(
            [ffn1_gate[l], ffn1_up[l], ffn2_gate[l], ffn2_up[l]], block_rows=128)
        down1, down2 = _cast_bf16([ffn1_down[l], ffn2_down[l]], block_rows=512)
        wa, wb = _cast_bf16([w_branch_moba[l], w_branch_swa[l]], block_rows=512)
        wo, = _cast_bf16([w_out[l]], block_rows=512)
        w_main, w_kv = _split_w_in(w_in[l], qkv_cols, 2 * kv_w)

        x2d = _ffn(x2d, vec, gate1, up1, down1, seq=seq, norm_row=VEC_NORM_FFN1_ROW, ada_row=0,
                   final_norm=False)

        qa, ka, va, qb, kb, vb, ga, gb = _in_projection(
            x2d, norm_mix[l], sh2, sc2, w_main, w_kv, tables, seq=seq)

        rs = lambda t: t.reshape(b, seq, t.shape[-1])
        ya = _moba_attention(rs(qa), rs(ka), rs(va), heads=moba_w // MOBA_HEAD_DIM)
        yb = _swa_attention(rs(qb), rs(kb), rs(vb), swa_sinks[l])

        x2d = _merge(ya.reshape(b * seq, moba_w), yb.reshape(b * seq, swa_qw), ga, gb, x2d, g2,
                     wa, wb, wo, seq=seq)

        x2d = _ffn(x2d, vec, gate2, up2, down2, seq=seq, norm_row=VEC_NORM_FFN2_ROW, ada_row=6,
                   final_norm=(l == depth - 1))

    return x2d.reshape(b, seq, d)
```

```python
import functools

import jax
import jax.numpy as jnp
from jax import lax
from jax.experimental import pallas as pl
from jax.experimental.pallas import tpu as pltpu

MOBA_HEAD_DIM = 128
MOBA_BLOCK = 256
MOBA_TOPK = 3
SWA_HEAD_DIM = 64
SWA_KV_HEADS = 2
SWA_WINDOW = 128
ROPE_THETA = 10000.0
EPS = 1e-6
N_ADA = 9
VEC_NORM_FFN1_ROW, VEC_NORM_MIX_ROW, VEC_NORM_FFN2_ROW, VEC_FINAL_NORM_ROW = 9, 10, 11, 12
VEC_ROWS = 16

LANES = 128
ROW_CHUNK = 16
VMEM_LIMIT = 56 * 1024 * 1024
MASK_VALUE = -1e30
LOG2_E = 1.4426950408889634
MOBA_LOOP_GROUP = 2
MOBA_HEADS_PER_STEP = 4

F32 = jnp.float32
BF16 = jnp.bfloat16


def _sigmoid(x):
    return 1.0 / (1.0 + jnp.exp(-x))


def _row_rsqrt_mean_square(x):
    r = lax.rsqrt(jnp.mean(x * x, axis=-1, keepdims=True) + EPS)
    return jnp.broadcast_to(r, (x.shape[0], LANES))


def _across_lanes(r, width):
    return jnp.concatenate([r] * (width // LANES), axis=1)


def _for_row_chunks(rows, body):
    def step(c, carry):
        body(pl.ds(pl.multiple_of(c * ROW_CHUNK, ROW_CHUNK), ROW_CHUNK))
        return carry
    lax.fori_loop(0, rows // ROW_CHUNK, step, 0, unroll=4)


def _rms_modulate_to(h_ref, x_ref, rinv_scr, norm_w, shift, scale):
    rinv_scr[...] = _row_rsqrt_mean_square(x_ref[...])
    gain = norm_w * (1.0 + scale)

    def slab(rows):
        rinv = _across_lanes(rinv_scr[rows, :], x_ref.shape[1])
        h_ref[rows, :] = (x_ref[rows, :] * rinv * gain + shift).astype(BF16)

    _for_row_chunks(x_ref.shape[0], slab)


def _dot(a, b):
    return jnp.dot(a, b, preferred_element_type=F32)


def _dot_nt(a, b):
    return lax.dot_general(a, b, (((1,), (1,)), ((), ())), preferred_element_type=F32)


def _cast_kernel(*refs):
    n = len(refs) // 2
    for src, dst in zip(refs[:n], refs[n:]):
        dst[...] = src[...].astype(dst.dtype)


def _cast_bf16(arrays, block_rows):
    rows, cols = arrays[0].shape
    spec = pl.BlockSpec((block_rows, cols), lambda r: (r, 0))
    n = len(arrays)
    return pl.pallas_call(
        _cast_kernel,
        grid=(rows // block_rows,),
        in_specs=[spec] * n,
        out_specs=[spec] * n,
        out_shape=[jax.ShapeDtypeStruct((rows, cols), BF16)] * n,
        compiler_params=pltpu.CompilerParams(
            dimension_semantics=("arbitrary",), vmem_limit_bytes=VMEM_LIMIT),
        name="cast_bf16",
    )(*arrays)


def _split_w_in_kernel(w_ref, main_ref, kv_ref, *, kv_block):
    j = pl.program_id(0)

    @pl.when(j != kv_block)
    def _():
        main_ref[...] = w_ref[...].astype(BF16)

    @pl.when(j == kv_block)
    def _():
        kv_ref[...] = w_ref[...].astype(BF16)


def _split_w_in(w, kv_start, kv_width):
    d, cols = w.shape
    assert kv_start % kv_width == 0 and cols % kv_width == 0
    kv_block = kv_start // kv_width
    main_col = lambda j: (0, jnp.where(j > kv_block, j - 1, jnp.minimum(j, kv_block - 1)))
    return pl.pallas_call(
        functools.partial(_split_w_in_kernel, kv_block=kv_block),
        grid=(cols // kv_width,),
        in_specs=[pl.BlockSpec((d, kv_width), lambda j: (0, j))],
        out_specs=[pl.BlockSpec((d, kv_width), main_col),
                   pl.BlockSpec((d, kv_width), lambda j: (0, 0))],
        out_shape=[jax.ShapeDtypeStruct((d, cols - kv_width), BF16),
                   jax.ShapeDtypeStruct((d, kv_width), BF16)],
        compiler_params=pltpu.CompilerParams(
            dimension_semantics=("arbitrary",), vmem_limit_bytes=VMEM_LIMIT),
        name="split_w_in",
    )(w)


def _ada_kernel(c_ref, w_ref, b_ref, o_ref):
    c = c_ref[...]
    s = (c * _sigmoid(c)).astype(BF16)
    o_ref[...] = _dot(s, w_ref[...].astype(BF16)) + b_ref[...]


def _ada_modulation(c_pad, w, b):
    rows, d = c_pad.shape
    n = w.shape[1]
    tn = 1024
    return pl.pallas_call(
        _ada_kernel,
        grid=(n // tn,),
        in_specs=[
            pl.BlockSpec((rows, d), lambda j: (0, 0)),
            pl.BlockSpec((d, tn), lambda j: (0, j)),
            pl.BlockSpec((1, tn), lambda j: (0, j)),
        ],
        out_specs=pl.BlockSpec((rows, tn), lambda j: (0, j)),
        out_shape=jax.ShapeDtypeStruct((rows, n), F32),
        compiler_params=pltpu.CompilerParams(
            dimension_semantics=("arbitrary",), vmem_limit_bytes=VMEM_LIMIT),
        name="ada_modulation",
    )(c_pad, w, b.reshape(1, n))


def _ffn_kernel(x_ref, vec_ref, wg_ref, wu_ref, wd_ref, o_ref, h_scr, acc_scr, rinv_scr, *,
                final_norm, tiles_per_seq, norm_row, ada_row):
    f = pl.program_id(1)
    nf = pl.num_programs(1)
    batch = pl.program_id(0) // tiles_per_seq
    vec = lambda r: vec_ref[batch, r:r + 1, :]

    @pl.when(f == 0)
    def _():
        _rms_modulate_to(h_scr, x_ref, rinv_scr, vec(norm_row), vec(ada_row), vec(ada_row + 1))
        acc_scr[...] = jnp.zeros_like(acc_scr)

    h = h_scr[...]
    gate = _dot(h, wg_ref[...])
    up = _dot(h, wu_ref[...])
    act = (gate * _sigmoid(gate) * up).astype(BF16)
    acc_scr[...] += _dot(act, wd_ref[...])

    @pl.when(f == nf - 1)
    def _():
        y = x_ref[...] + (0.5 * vec(ada_row + 2)) * acc_scr[...]
        o_ref[...] = y
        if final_norm:
            rinv_scr[...] = _row_rsqrt_mean_square(y)
            final_w = vec(VEC_FINAL_NORM_ROW)

            def slab(rows):
                rinv = _across_lanes(rinv_scr[rows, :], o_ref.shape[1])
                o_ref[rows, :] = o_ref[rows, :] * rinv * final_w

            _for_row_chunks(o_ref.shape[0], slab)


def _ffn(x2d, vec, wg, wu, wd, *, seq, norm_row, ada_row, final_norm):
    m, d = x2d.shape
    dff = wg.shape[1]
    tm, tf = 512, 512
    row = lambda i, f: (i, 0)
    return pl.pallas_call(
        functools.partial(_ffn_kernel, final_norm=final_norm, tiles_per_seq=seq // tm,
                          norm_row=norm_row, ada_row=ada_row),
        grid=(m // tm, dff // tf),
        in_specs=[
            pl.BlockSpec((tm, d), row),
            pl.BlockSpec(vec.shape, lambda i, f: (0, 0, 0)),
            pl.BlockSpec((d, tf), lambda i, f: (0, f)),
            pl.BlockSpec((d, tf), lambda i, f: (0, f)),
            pl.BlockSpec((tf, d), lambda i, f: (f, 0)),
        ],
        out_specs=pl.BlockSpec((tm, d), row),
        out_shape=jax.ShapeDtypeStruct((m, d), F32),
        scratch_shapes=[pltpu.VMEM((tm, d), BF16), pltpu.VMEM((tm, d), F32),
                        pltpu.VMEM((tm, LANES), F32)],
        compiler_params=pltpu.CompilerParams(
            dimension_semantics=("arbitrary", "arbitrary"), vmem_limit_bytes=VMEM_LIMIT),
        name="ffn_final" if final_norm else "ffn",
    )(x2d, vec, wg, wu, wd)


def _rope128(x, cos, sin):
    return x * cos + pltpu.roll(x, 64, 1) * sin


def _rope64(x, cos, sin_lo, sin_hi):
    return x * cos + pltpu.roll(x, 96, 1) * sin_lo + pltpu.roll(x, 32, 1) * sin_hi


def _proj_kernel(x_ref, nw_ref, sh_ref, sc_ref, w_ref, wkv_ref,
                 cos_a_ref, sin_a_ref, cos_b_ref, sin_lo_ref, sin_hi_ref,
                 qa_ref, ka_ref, va_ref, qb_ref, kb_ref, vb_ref, ga_ref, gb_ref, h_scr, rinv_scr):
    n = pl.program_id(1)

    @pl.when(n == 0)
    def _():
        _rms_modulate_to(h_scr, x_ref, rinv_scr, nw_ref[...], sh_ref[...], sc_ref[...])

    h = h_scr[...]
    chunk = 2 * LANES
    identity = lambda x: x

    def project(dst_ref, epilogue, col0):
        for s in range(0, dst_ref.shape[1], chunk):
            acc = _dot(h, w_ref[:, col0 + s:col0 + s + chunk])
            for t in range(0, chunk, LANES):
                dst_ref[:, s + t:s + t + LANES] = epilogue(acc[:, t:t + LANES]).astype(BF16)

    rope_a = lambda x: _rope128(x, cos_a_ref[...], sin_a_ref[...])
    rope_b = lambda x: _rope64(x, cos_b_ref[...], sin_lo_ref[...], sin_hi_ref[...])

    @pl.when(n == 0)
    def _():
        project(qa_ref, rope_a, 0)
        project(ka_ref, rope_a, qa_ref.shape[1])

    @pl.when(n == 1)
    def _():
        project(va_ref, identity, 0)
        project(qb_ref, rope_b, va_ref.shape[1])
        kv = _dot(h, wkv_ref[...])
        kb_ref[...] = rope_b(kv[:, :LANES]).astype(BF16)
        vb_ref[...] = kv[:, LANES:].astype(BF16)

    @pl.when(n == 2)
    def _():
        project(ga_ref, identity, 0)

    @pl.when(n == 3)
    def _():
        project(gb_ref, identity, 0)


def _in_projection(x2d, norm_w, shift, scale, w_main, w_kv, tables, *, seq):
    m, d = x2d.shape
    tm, tn, half = 512, 2048, 1024
    n_tiles = w_main.shape[1] // tn
    assert n_tiles == 4 and d == tn
    tiles_per_seq = seq // tm
    per_batch = lambda i, n: (i // tiles_per_seq, 0, 0)
    pos = lambda i, n: (i % tiles_per_seq, 0)
    full = lambda i, n: (i, 0)
    wide = lambda cols: jax.ShapeDtypeStruct((m, cols), BF16)
    table_spec = pl.BlockSpec((tm, LANES), pos)
    return pl.pallas_call(
        _proj_kernel,
        grid=(m // tm, n_tiles),
        in_specs=[
            pl.BlockSpec((tm, d), full),
            pl.BlockSpec((1, d), lambda i, n: (0, 0)),
            pl.BlockSpec((None, 1, d), per_batch),
            pl.BlockSpec((None, 1, d), per_batch),
            pl.BlockSpec((d, tn), lambda i, n: (0, n)),
            pl.BlockSpec((d, 2 * LANES), lambda i, n: (0, 0)),
            table_spec, table_spec, table_spec, table_spec, table_spec,
        ],
        out_specs=[
            pl.BlockSpec((tm, half), full),
            pl.BlockSpec((tm, half), full),
            pl.BlockSpec((tm, half), full),
            pl.BlockSpec((tm, half), full),
            pl.BlockSpec((tm, LANES), full),
            pl.BlockSpec((tm, LANES), full),
            pl.BlockSpec((tm, tn), full),
            pl.BlockSpec((tm, tn), full),
        ],
        out_shape=[wide(half), wide(half), wide(half), wide(half), wide(LANES), wide(LANES),
                   wide(tn), wide(tn)],
        scratch_shapes=[pltpu.VMEM((tm, d), BF16), pltpu.VMEM((tm, LANES), F32)],
        compiler_params=pltpu.CompilerParams(
            dimension_semantics=("arbitrary", "arbitrary"), vmem_limit_bytes=VMEM_LIMIT),
        name="in_projection",
    )(x2d, norm_w.reshape(1, d), shift, scale, w_main, w_kv, *tables)


def _rope_tables(seq):
    pos = jnp.arange(seq, dtype=F32)[:, None]

    def cos_sin(half):
        inv = ROPE_THETA ** (-jnp.arange(half, dtype=F32) / half)
        ang = pos * inv[None, :]
        return jnp.cos(ang), jnp.sin(ang)

    cos, sin = cos_sin(MOBA_HEAD_DIM // 2)
    cos_a = jnp.concatenate([cos, cos], axis=1)
    sin_a = jnp.concatenate([-sin, sin], axis=1)
    cos, sin = cos_sin(SWA_HEAD_DIM // 2)
    zero = jnp.zeros_like(sin)
    cos_b = jnp.concatenate([cos, cos, cos, cos], axis=1)
    sin_lo = jnp.concatenate([-sin, zero, -sin, zero], axis=1)
    sin_hi = jnp.concatenate([zero, sin, zero, sin], axis=1)
    return cos_a, sin_a, cos_b, sin_lo, sin_hi


def _moba_kernel(q_ref, k_ref, v_ref, o_ref, kaug_scr, vt_scr, kmean_scr, s_scr, *, topk, group):
    i = pl.program_id(2)
    blk = q_ref.shape[0]
    heads, nb, dh, _ = vt_scr.shape
    exp2_scale = dh ** -0.5 * LOG2_E
    head_lanes = lambda h: slice(h * dh, (h + 1) * dh)

    @pl.when(i == 0)
    def _():
        col = lax.broadcasted_iota(jnp.int32, (blk, LANES), 1)
        for h in range(heads):
            for j in range(nb):
                k = k_ref[j * blk:(j + 1) * blk, head_lanes(h)]
                kmean_scr[h, j:j + 1, :] = jnp.mean(k.astype(F32), axis=0, keepdims=True)
                kaug_scr[h, j, :, :dh] = k
                kaug_scr[h, j, :, dh:] = jnp.where(col == j, 1.0, 0.0).astype(BF16)
                v = v_ref[j * blk:(j + 1) * blk, head_lanes(h)]
                vt_scr[h, j] = v.astype(F32).T.astype(BF16)

    def gate(h):
        q = q_ref[:, head_lanes(h)]
        gs = _dot_nt(kmean_scr[h].astype(BF16), q)
        blk_id = lax.broadcasted_iota(jnp.int32, gs.shape, 0)
        valid = blk_id < i
        vals = jnp.where(valid, gs, -jnp.inf)
        sel = jnp.zeros(gs.shape, dtype=jnp.bool_)
        for _ in range(topk):
            best = jnp.max(vals, axis=0, keepdims=True)
            first = jnp.min(jnp.where(vals == best, blk_id, nb), axis=0, keepdims=True)
            pick = blk_id == first
            sel = sel | pick
            vals = jnp.where(pick, -jnp.inf, vals)
        sel = sel & valid
        bias_t = jnp.where(sel, 0.0, MASK_VALUE).astype(F32)
        bias_t = jnp.concatenate([bias_t, jnp.zeros((LANES - nb, blk), F32)], axis=0)
        q_t = q.astype(F32).T.astype(BF16)
        return jnp.concatenate([q_t, bias_t.astype(BF16)], axis=0)

    q_aug_t = [gate(h) for h in range(heads)]

    def attend(n_past):
        own_slot = n_past
        r = lax.broadcasted_iota(jnp.int32, (blk, blk), 0)
        c = lax.broadcasted_iota(jnp.int32, (blk, blk), 1)
        m = []
        for h in range(heads):
            s = _dot(kaug_scr[h, i, :, :dh], q_aug_t[h][:dh]) * exp2_scale
            s = jnp.where(r <= c, s, MASK_VALUE)
            s_scr[h, own_slot] = s
            mh = jnp.max(s, axis=0, keepdims=True)
            for j in range(n_past):
                s = _dot(kaug_scr[h, j], q_aug_t[h]) * exp2_scale
                s_scr[h, j] = s
                mh = jnp.maximum(mh, jnp.max(s, axis=0, keepdims=True))
            m.append(mh)

        for h in range(heads):
            l = jnp.zeros_like(m[h])
            acc = jnp.zeros((dh, blk), F32)
            for slot in range(n_past + 1):
                p = jnp.exp2(s_scr[h, slot] - m[h])
                l = l + jnp.sum(p, axis=0, keepdims=True)
                vt = vt_scr[h, i] if slot == own_slot else vt_scr[h, slot]
                acc = acc + _dot(vt, p.astype(BF16))
            o_ref[:, head_lanes(h)] = (acc / l).T.astype(o_ref.dtype)

    n_groups = (i + group - 1) // group
    for n in range(nb // group + 1):
        pl.when(n_groups == n)(functools.partial(attend, n * group))


def _moba_attention(q, k, v, *, heads):
    b, seq, _ = q.shape
    dh, blk = MOBA_HEAD_DIM, MOBA_BLOCK
    nb = seq // blk
    hps = MOBA_HEADS_PER_STEP
    q_spec = pl.BlockSpec((None, blk, hps * dh), lambda bi, h, i: (bi, i, h))
    kv_spec = pl.BlockSpec((None, seq, hps * dh), lambda bi, h, i: (bi, 0, h))
    return pl.pallas_call(
        functools.partial(_moba_kernel, topk=min(MOBA_TOPK, nb), group=MOBA_LOOP_GROUP),
        grid=(b, heads // hps, nb),
        in_specs=[q_spec, kv_spec, kv_spec],
        out_specs=q_spec,
        out_shape=jax.ShapeDtypeStruct(q.shape, BF16),
        scratch_shapes=[pltpu.VMEM((hps, nb, blk, dh + LANES), BF16),
                        pltpu.VMEM((hps, nb, dh, blk), BF16),
                        pltpu.VMEM((hps, nb, dh), F32),
                        pltpu.VMEM((hps, nb + 1, blk, blk), F32)],
        compiler_params=pltpu.CompilerParams(
            dimension_semantics=("arbitrary", "arbitrary", "arbitrary"),
            vmem_limit_bytes=VMEM_LIMIT),
        name="moba_attention",
    )(q, k, v)


def _swa_kernel(sink_ref, q_ref, kp_ref, kc_ref, vp_ref, vc_ref, o_ref, k_scr, vt_scr, s_scr, *,
                q_lane_blocks):
    i = pl.program_id(1)
    w = kp_ref.shape[0]
    tq = q_ref.shape[0]
    sub_blocks = tq // w
    half = LANES // 2
    scale = half ** -0.5
    groups = q_lane_blocks // SWA_KV_HEADS

    k_scr[:w, :] = kp_ref[...]
    k_scr[w:, :] = kc_ref[...]
    transposed = lambda x: x.astype(F32).T.astype(BF16)
    vt_scr[0] = transposed(vp_ref[...])
    for c in range(sub_blocks):
        vt_scr[c + 1] = transposed(vc_ref[c * w:(c + 1) * w, :])

    key = lax.broadcasted_iota(jnp.int32, (2 * w, 2 * w), 0)
    col = lax.broadcasted_iota(jnp.int32, (2 * w, 2 * w), 1)
    qry = jnp.where(col >= w, col - w, col)
    band = (key > qry) & (key <= qry + w)
    band_bias = jnp.where(band, 0.0, MASK_VALUE)
    first_bias = jnp.where(band & (key >= w), 0.0, MASK_VALUE)
    odd_head = lax.broadcasted_iota(jnp.int32, (1, 2 * w), 1) >= w
    zeros = jnp.zeros((half, 2 * w), BF16)

    def sub_block(u, carry):
        start = pl.multiple_of(u * w, w)
        bias = jnp.where((i * sub_blocks + u) == 0, first_bias, band_bias)
        k2 = k_scr[pl.ds(start, 2 * w), :]
        vt2 = jnp.concatenate([vt_scr[u], vt_scr[u + 1]], axis=1)
        lanes = lambda jb: slice(jb * LANES, (jb + 1) * LANES)

        sinks, maxima = [], []
        for jb in range(q_lane_blocks):
            qt = (q_ref[pl.ds(start, w), lanes(jb)].astype(F32) * scale).T.astype(BF16)
            q2 = jnp.concatenate([qt[:half], qt[half:]], axis=1)
            rhs = jnp.concatenate([q2, zeros] if jb < groups else [zeros, q2], axis=0)
            s = _dot(k2, rhs) + bias
            s_scr[jb] = s
            sink = jnp.where(odd_head, sink_ref[2 * jb + 1], sink_ref[2 * jb])
            sinks.append(sink)
            maxima.append(jnp.maximum(jnp.max(s, axis=0, keepdims=True), sink))

        for jb in range(q_lane_blocks):
            kvh = jb // groups
            p = jnp.exp(s_scr[jb] - maxima[jb])
            den = jnp.sum(p, axis=0, keepdims=True) + jnp.exp(sinks[jb] - maxima[jb])
            o = _dot(vt2[kvh * half:(kvh + 1) * half], p.astype(BF16)) / den
            o_ref[pl.ds(start, w), lanes(jb)] = jnp.concatenate(
                [o[:, :w], o[:, w:]], axis=0).T.astype(o_ref.dtype)
        return carry

    lax.fori_loop(0, sub_blocks, sub_block, 0)


def _swa_attention(q, k, v, sinks):
    b, seq, qw = q.shape
    w = SWA_WINDOW
    tq = 512
    sub_blocks = tq // w
    q_spec = pl.BlockSpec((None, tq, qw), lambda bi, i: (bi, i, 0))
    cur = pl.BlockSpec((None, tq, LANES), lambda bi, i: (bi, i, 0))
    prev = pl.BlockSpec((None, w, LANES), lambda bi, i: (bi, jnp.maximum(i * sub_blocks - 1, 0), 0))
    return pl.pallas_call(
        functools.partial(_swa_kernel, q_lane_blocks=qw // LANES),
        grid=(b, seq // tq),
        in_specs=[pl.BlockSpec(memory_space=pltpu.SMEM), q_spec, prev, cur, prev, cur],
        out_specs=q_spec,
        out_shape=jax.ShapeDtypeStruct(q.shape, BF16),
        scratch_shapes=[pltpu.VMEM((tq + w, LANES), BF16),
                        pltpu.VMEM((sub_blocks + 1, LANES, w), BF16),
                        pltpu.VMEM((qw // LANES, 2 * w, 2 * w), F32)],
        compiler_params=pltpu.CompilerParams(
            dimension_semantics=("arbitrary", "arbitrary"), vmem_limit_bytes=VMEM_LIMIT),
        name="swa_attention",
    )(sinks, q, k, k, v, v)


def _merge_kernel(ya_ref, yb_ref, ga_ref, gb_ref, x_ref, g_ref, wa_ref, wb_ref, wo_ref, o_ref):
    merged = (_sigmoid(ga_ref[...].astype(F32)) * _dot(ya_ref[...], wa_ref[...])
              + _sigmoid(gb_ref[...].astype(F32)) * _dot(yb_ref[...], wb_ref[...]))
    o_ref[...] = x_ref[...] + g_ref[...] * _dot(merged.astype(BF16), wo_ref[...])


def _merge(ya, yb, ga, gb, x2d, gate, wa, wb, wo, *, seq):
    m, d = x2d.shape
    tm = 256
    tiles_per_seq = seq // tm
    row = lambda i: (i, 0)
    const = lambda i: (0, 0)
    resident = lambda shape: pl.BlockSpec(shape, const, pipeline_mode=pl.Buffered(1))
    return pl.pallas_call(
        _merge_kernel,
        grid=(m // tm,),
        in_specs=[
            pl.BlockSpec((tm, ya.shape[1]), row),
            pl.BlockSpec((tm, yb.shape[1]), row),
            pl.BlockSpec((tm, d), row),
            pl.BlockSpec((tm, d), row),
            pl.BlockSpec((tm, d), row),
            pl.BlockSpec((None, 1, d), lambda i: (i // tiles_per_seq, 0, 0)),
            resident(wa.shape), resident(wb.shape), resident(wo.shape),
        ],
        out_specs=pl.BlockSpec((tm, d), row),
        out_shape=jax.ShapeDtypeStruct((m, d), F32),
        compiler_params=pltpu.CompilerParams(
            dimension_semantics=("arbitrary",), vmem_limit_bytes=VMEM_LIMIT),
        name="merge_out_projection",
    )(ya, yb, ga, gb, x2d, gate, wa, wb, wo)


def kernel(x, c, w_ada, b_ada, norm_ffn1, ffn1_gate, ffn1_up, ffn1_down, norm_mix, w_in, swa_sinks,
           w_branch_moba, w_branch_swa, w_out, norm_ffn2, ffn2_gate, ffn2_up, ffn2_down, norm_final):
    b, seq, d = x.shape
    depth = w_ada.shape[0]
    moba_w = w_branch_moba.shape[1]
    swa_qw = w_branch_swa.shape[1]
    kv_w = SWA_KV_HEADS * SWA_HEAD_DIM
    qkv_cols = 3 * moba_w + swa_qw
    assert w_in.shape[2] == qkv_cols + 2 * kv_w + 2 * d
    assert moba_w == swa_qw == 1024 and d == 2048 and kv_w == LANES

    tables = _rope_tables(seq)
    c_pad = jnp.pad(c, ((0, 8 - b), (0, 0)))
    x2d = x.reshape(b * seq, d)

    for l in range(depth):
        mod = _ada_modulation(c_pad, w_ada[l], b_ada[l])[:b].reshape(b, N_ADA, d)
        sh2, sc2, g2 = [mod[:, t:t + 1] for t in (3, 4, 5)]
        gains = jnp.stack([norm_ffn1[l], norm_mix[l], norm_ffn2[l], norm_final])
        vec = jnp.concatenate(
            [mod, jnp.broadcast_to(gains, (b,) + gains.shape),
             jnp.zeros((b, VEC_ROWS - N_ADA - gains.shape[0], d), F32)], axis=1)

        gate1, up1, gate2, up2 = _cast_bf16(
            [ffn1_gate[l], ffn1_up[l], ffn2_gate[l], ffn2_up[l]], block_rows=128)
        down1, down2 = _cast_bf16([ffn1_down[l], ffn2_down[l]], block_rows=512)
        wa, wb = _cast_bf16([w_branch_moba[l], w_branch_swa[l]], block_rows=512)
        wo, = _cast_bf16([w_out[l]], block_rows=512)
        w_main, w_kv = _split_w_in(w_in[l], qkv_cols, 2 * kv_w)

        x2d = _ffn(x2d, vec, gate1, up1, down1, seq=seq, norm_row=VEC_NORM_FFN1_ROW, ada_row=0,
                   final_norm=False)

        qa, ka, va, qb, kb, vb, ga, gb = _in_projection(
            x2d, norm_mix[l], sh2, sc2, w_main, w_kv, tables, seq=seq)

        rs = lambda t: t.reshape(b, seq, t.shape[-1])
        ya = _moba_attention(rs(qa), rs(ka), rs(va), heads=moba_w // MOBA_HEAD_DIM)
        yb = _swa_attention(rs(qb), rs(kb), rs(vb), swa_sinks[l])

        x2d = _merge(ya.reshape(b * seq, moba_w), yb.reshape(b * seq, swa_qw), ga, gb, x2d, g2,
                     wa, wb, wo, seq=seq)

        x2d = _ffn(x2d, vec, gate2, up2, down2, seq=seq, norm_row=VEC_NORM_FFN2_ROW, ada_row=6,
                   final_norm=(l == depth - 1))

    return x2d.reshape(b, seq, d)
```

```python
import functools

import jax
import jax.numpy as jnp
from jax import lax
from jax.experimental import pallas as pl
from jax.experimental.pallas import tpu as pltpu

MOBA_HEAD_DIM = 128
MOBA_BLOCK = 256
MOBA_TOPK = 3
SWA_HEAD_DIM = 64
SWA_KV_HEADS = 2
SWA_WINDOW = 128
ROPE_THETA = 10000.0
EPS = 1e-6
N_ADA = 9
VEC_NORM_FFN1_ROW, VEC_NORM_MIX_ROW, VEC_NORM_FFN2_ROW, VEC_FINAL_NORM_ROW = 9, 10, 11, 12
VEC_ROWS = 16

LANES = 128
BF16_SUBLANES = 16
ROW_CHUNK = 16
VMEM_LIMIT = 56 * 1024 * 1024
MASK_VALUE = -1e30
LOG2_E = 1.4426950408889634
MOBA_LOOP_GROUP = 2
MOBA_HEADS_PER_STEP = 4

F32 = jnp.float32
BF16 = jnp.bfloat16


def _sigmoid(x):
    return 1.0 / (1.0 + jnp.exp(-x))


def _row_rsqrt_mean_square(x):
    r = lax.rsqrt(jnp.mean(x * x, axis=-1, keepdims=True) + EPS)
    return jnp.broadcast_to(r, (x.shape[0], LANES))


def _across_lanes(r, width):
    return jnp.concatenate([r] * (width // LANES), axis=1)


def _for_row_chunks(rows, body):
    def step(c, carry):
        body(pl.ds(pl.multiple_of(c * ROW_CHUNK, ROW_CHUNK), ROW_CHUNK))
        return carry
    lax.fori_loop(0, rows // ROW_CHUNK, step, 0, unroll=4)


def _rms_modulate_to(h_ref, x_ref, rinv_scr, norm_w, shift, scale):
    rinv_scr[...] = _row_rsqrt_mean_square(x_ref[...])
    gain = norm_w * (1.0 + scale)

    def slab(rows):
        rinv = _across_lanes(rinv_scr[rows, :], x_ref.shape[1])
        h_ref[rows, :] = (x_ref[rows, :] * rinv * gain + shift).astype(BF16)

    _for_row_chunks(x_ref.shape[0], slab)


def _dot(a, b):
    return jnp.dot(a, b, preferred_element_type=F32)


def _dot_nt(a, b):
    return lax.dot_general(a, b, (((1,), (1,)), ((), ())), preferred_element_type=F32)


def _cast_kernel(*refs):
    n = len(refs) // 2
    for src, dst in zip(refs[:n], refs[n:]):
        dst[...] = src[...].astype(dst.dtype)


def _cast_bf16(arrays, block_rows):
    rows, cols = arrays[0].shape
    spec = pl.BlockSpec((block_rows, cols), lambda r: (r, 0))
    n = len(arrays)
    return pl.pallas_call(
        _cast_kernel,
        grid=(rows // block_rows,),
        in_specs=[spec] * n,
        out_specs=[spec] * n,
        out_shape=[jax.ShapeDtypeStruct((rows, cols), BF16)] * n,
        compiler_params=pltpu.CompilerParams(
            dimension_semantics=("arbitrary",), vmem_limit_bytes=VMEM_LIMIT),
        name="cast_bf16",
    )(*arrays)


def _side_cast_specs(arrays, steps, step_index):
    specs = []
    for a in arrays:
        rows, cols = a.shape
        for col_blocks in (1, 2, 4, 8, 16):
            row_blocks = steps // col_blocks
            if rows % (row_blocks * BF16_SUBLANES) == 0 and cols % (col_blocks * LANES) == 0:
                break
        else:
            raise ValueError(f"no {steps}-block tiling for {a.shape}")
        index = lambda *g, cb=col_blocks: (step_index(*g) // cb, step_index(*g) % cb)
        specs.append(pl.BlockSpec((rows // row_blocks, cols // col_blocks), index))
    return specs


def _side_cast(src_refs, dst_refs):
    for src, dst in zip(src_refs, dst_refs):
        dst[...] = src[...].astype(dst.dtype)


def _bf16_like(arrays):
    return [jax.ShapeDtypeStruct(a.shape, BF16) for a in arrays]


def _split_w_in_kernel(w_ref, main_ref, kv_ref, *, kv_block):
    j = pl.program_id(0)

    @pl.when(j != kv_block)
    def _():
        main_ref[...] = w_ref[...].astype(BF16)

    @pl.when(j == kv_block)
    def _():
        kv_ref[...] = w_ref[...].astype(BF16)


def _split_w_in(w, kv_start, kv_width):
    d, cols = w.shape
    assert kv_start % kv_width == 0 and cols % kv_width == 0
    kv_block = kv_start // kv_width
    main_col = lambda j: (0, jnp.where(j > kv_block, j - 1, jnp.minimum(j, kv_block - 1)))
    return pl.pallas_call(
        functools.partial(_split_w_in_kernel, kv_block=kv_block),
        grid=(cols // kv_width,),
        in_specs=[pl.BlockSpec((d, kv_width), lambda j: (0, j))],
        out_specs=[pl.BlockSpec((d, kv_width), main_col),
                   pl.BlockSpec((d, kv_width), lambda j: (0, 0))],
        out_shape=[jax.ShapeDtypeStruct((d, cols - kv_width), BF16),
                   jax.ShapeDtypeStruct((d, kv_width), BF16)],
        compiler_params=pltpu.CompilerParams(
            dimension_semantics=("arbitrary",), vmem_limit_bytes=VMEM_LIMIT),
        name="split_w_in",
    )(w)


def _ada_kernel(c_ref, w_ref, b_ref, o_ref):
    c = c_ref[...]
    s = (c * _sigmoid(c)).astype(BF16)
    o_ref[...] = _dot(s, w_ref[...].astype(BF16)) + b_ref[...]


def _ada_modulation(c_pad, w, b):
    rows, d = c_pad.shape
    n = w.shape[1]
    tn = 1024
    return pl.pallas_call(
        _ada_kernel,
        grid=(n // tn,),
        in_specs=[
            pl.BlockSpec((rows, d), lambda j: (0, 0)),
            pl.BlockSpec((d, tn), lambda j: (0, j)),
            pl.BlockSpec((1, tn), lambda j: (0, j)),
        ],
        out_specs=pl.BlockSpec((rows, tn), lambda j: (0, j)),
        out_shape=jax.ShapeDtypeStruct((rows, n), F32),
        compiler_params=pltpu.CompilerParams(
            dimension_semantics=("arbitrary",), vmem_limit_bytes=VMEM_LIMIT),
        name="ada_modulation",
    )(c_pad, w, b.reshape(1, n))


def _ffn_kernel(x_ref, vec_ref, wg_ref, wu_ref, wd_ref, o_ref, h_scr, acc_scr, rinv_scr, *,
                final_norm, tiles_per_seq, norm_row, ada_row):
    f = pl.program_id(1)
    nf = pl.num_programs(1)
    batch = pl.program_id(0) // tiles_per_seq
    vec = lambda r: vec_ref[batch, r:r + 1, :]

    @pl.when(f == 0)
    def _():
        _rms_modulate_to(h_scr, x_ref, rinv_scr, vec(norm_row), vec(ada_row), vec(ada_row + 1))
        acc_scr[...] = jnp.zeros_like(acc_scr)

    h = h_scr[...]
    gate = _dot(h, wg_ref[...])
    up = _dot(h, wu_ref[...])
    act = (gate * _sigmoid(gate) * up).astype(BF16)
    acc_scr[...] += _dot(act, wd_ref[...])

    @pl.when(f == nf - 1)
    def _():
        y = x_ref[...] + (0.5 * vec(ada_row + 2)) * acc_scr[...]
        o_ref[...] = y
        if final_norm:
            rinv_scr[...] = _row_rsqrt_mean_square(y)
            final_w = vec(VEC_FINAL_NORM_ROW)

            def slab(rows):
                rinv = _across_lanes(rinv_scr[rows, :], o_ref.shape[1])
                o_ref[rows, :] = o_ref[rows, :] * rinv * final_w

            _for_row_chunks(o_ref.shape[0], slab)


def _ffn(x2d, vec, wg, wu, wd, *, seq, norm_row, ada_row, final_norm):
    m, d = x2d.shape
    dff = wg.shape[1]
    tm, tf = 512, 512
    row = lambda i, f: (i, 0)
    return pl.pallas_call(
        functools.partial(_ffn_kernel, final_norm=final_norm, tiles_per_seq=seq // tm,
                          norm_row=norm_row, ada_row=ada_row),
        grid=(m // tm, dff // tf),
        in_specs=[
            pl.BlockSpec((tm, d), row),
            pl.BlockSpec(vec.shape, lambda i, f: (0, 0, 0)),
            pl.BlockSpec((d, tf), lambda i, f: (0, f)),
            pl.BlockSpec((d, tf), lambda i, f: (0, f)),
            pl.BlockSpec((tf, d), lambda i, f: (f, 0)),
        ],
        out_specs=pl.BlockSpec((tm, d), row),
        out_shape=jax.ShapeDtypeStruct((m, d), F32),
        scratch_shapes=[pltpu.VMEM((tm, d), BF16), pltpu.VMEM((tm, d), F32),
                        pltpu.VMEM((tm, LANES), F32)],
        compiler_params=pltpu.CompilerParams(
            dimension_semantics=("arbitrary", "arbitrary"), vmem_limit_bytes=VMEM_LIMIT),
        name="ffn_final" if final_norm else "ffn",
    )(x2d, vec, wg, wu, wd)


def _rope128(x, cos, sin):
    return x * cos + pltpu.roll(x, 64, 1) * sin


def _rope64(x, cos, sin_lo, sin_hi):
    return x * cos + pltpu.roll(x, 96, 1) * sin_lo + pltpu.roll(x, 32, 1) * sin_hi


def _proj_kernel(x_ref, nw_ref, sh_ref, sc_ref, w_ref, wkv_ref,
                 cos_a_ref, sin_a_ref, cos_b_ref, sin_lo_ref, sin_hi_ref,
                 qa_ref, ka_ref, va_ref, qb_ref, kb_ref, vb_ref, ga_ref, gb_ref, h_scr, rinv_scr):
    n = pl.program_id(1)

    @pl.when(n == 0)
    def _():
        _rms_modulate_to(h_scr, x_ref, rinv_scr, nw_ref[...], sh_ref[...], sc_ref[...])

    h = h_scr[...]
    chunk = 2 * LANES
    identity = lambda x: x

    def project(dst_ref, epilogue, col0):
        for s in range(0, dst_ref.shape[1], chunk):
            acc = _dot(h, w_ref[:, col0 + s:col0 + s + chunk])
            for t in range(0, chunk, LANES):
                dst_ref[:, s + t:s + t + LANES] = epilogue(acc[:, t:t + LANES]).astype(BF16)

    rope_a = lambda x: _rope128(x, cos_a_ref[...], sin_a_ref[...])
    rope_b = lambda x: _rope64(x, cos_b_ref[...], sin_lo_ref[...], sin_hi_ref[...])

    @pl.when(n == 0)
    def _():
        project(qa_ref, rope_a, 0)
        project(ka_ref, rope_a, qa_ref.shape[1])

    @pl.when(n == 1)
    def _():
        project(va_ref, identity, 0)
        project(qb_ref, rope_b, va_ref.shape[1])
        kv = _dot(h, wkv_ref[...])
        kb_ref[...] = rope_b(kv[:, :LANES]).astype(BF16)
        vb_ref[...] = kv[:, LANES:].astype(BF16)

    @pl.when(n == 2)
    def _():
        project(ga_ref, identity, 0)

    @pl.when(n == 3)
    def _():
        project(gb_ref, identity, 0)


def _in_projection(x2d, norm_w, shift, scale, w_main, w_kv, tables, *, seq):
    m, d = x2d.shape
    tm, tn, half = 512, 2048, 1024
    n_tiles = w_main.shape[1] // tn
    assert n_tiles == 4 and d == tn
    tiles_per_seq = seq // tm
    per_batch = lambda i, n: (i // tiles_per_seq, 0, 0)
    pos = lambda i, n: (i % tiles_per_seq, 0)
    full = lambda i, n: (i, 0)
    wide = lambda cols: jax.ShapeDtypeStruct((m, cols), BF16)
    table_spec = pl.BlockSpec((tm, LANES), pos)
    return pl.pallas_call(
        _proj_kernel,
        grid=(m // tm, n_tiles),
        in_specs=[
            pl.BlockSpec((tm, d), full),
            pl.BlockSpec((1, d), lambda i, n: (0, 0)),
            pl.BlockSpec((None, 1, d), per_batch),
            pl.BlockSpec((None, 1, d), per_batch),
            pl.BlockSpec((d, tn), lambda i, n: (0, n)),
            pl.BlockSpec((d, 2 * LANES), lambda i, n: (0, 0)),
            table_spec, table_spec, table_spec, table_spec, table_spec,
        ],
        out_specs=[
            pl.BlockSpec((tm, half), full),
            pl.BlockSpec((tm, half), full),
            pl.BlockSpec((tm, half), full),
            pl.BlockSpec((tm, half), full),
            pl.BlockSpec((tm, LANES), full),
            pl.BlockSpec((tm, LANES), full),
            pl.BlockSpec((tm, tn), full),
            pl.BlockSpec((tm, tn), full),
        ],
        out_shape=[wide(half), wide(half), wide(half), wide(half), wide(LANES), wide(LANES),
                   wide(tn), wide(tn)],
        scratch_shapes=[pltpu.VMEM((tm, d), BF16), pltpu.VMEM((tm, LANES), F32)],
        compiler_params=pltpu.CompilerParams(
            dimension_semantics=("arbitrary", "arbitrary"), vmem_limit_bytes=VMEM_LIMIT),
        name="in_projection",
    )(x2d, norm_w.reshape(1, d), shift, scale, w_main, w_kv, *tables)


def _rope_tables(seq):
    pos = jnp.arange(seq, dtype=F32)[:, None]

    def cos_sin(half):
        inv = ROPE_THETA ** (-jnp.arange(half, dtype=F32) / half)
        ang = pos * inv[None, :]
        return jnp.cos(ang), jnp.sin(ang)

    cos, sin = cos_sin(MOBA_HEAD_DIM // 2)
    cos_a = jnp.concatenate([cos, cos], axis=1)
    sin_a = jnp.concatenate([-sin, sin], axis=1)
    cos, sin = cos_sin(SWA_HEAD_DIM // 2)
    zero = jnp.zeros_like(sin)
    cos_b = jnp.concatenate([cos, cos, cos, cos], axis=1)
    sin_lo = jnp.concatenate([-sin, zero, -sin, zero], axis=1)
    sin_hi = jnp.concatenate([zero, sin, zero, sin], axis=1)
    return cos_a, sin_a, cos_b, sin_lo, sin_hi


def _moba_kernel(*refs, topk, group, n_side):
    q_ref, k_ref, v_ref = refs[:3]
    o_ref = refs[3 + n_side]
    kaug_scr, vt_scr, kmean_scr, s_scr = refs[4 + 2 * n_side:]
    _side_cast(refs[3:3 + n_side], refs[4 + n_side:4 + 2 * n_side])

    i = pl.program_id(2)
    blk = q_ref.shape[0]
    heads, nb, dh, _ = vt_scr.shape
    exp2_scale = dh ** -0.5 * LOG2_E
    head_lanes = lambda h: slice(h * dh, (h + 1) * dh)

    @pl.when(i == 0)
    def _():
        col = lax.broadcasted_iota(jnp.int32, (blk, LANES), 1)
        for h in range(heads):
            for j in range(nb):
                k = k_ref[j * blk:(j + 1) * blk, head_lanes(h)]
                kmean_scr[h, j:j + 1, :] = jnp.mean(k.astype(F32), axis=0, keepdims=True)
                kaug_scr[h, j, :, :dh] = k
                kaug_scr[h, j, :, dh:] = jnp.where(col == j, 1.0, 0.0).astype(BF16)
                v = v_ref[j * blk:(j + 1) * blk, head_lanes(h)]
                vt_scr[h, j] = v.astype(F32).T.astype(BF16)

    def gate(h):
        q = q_ref[:, head_lanes(h)]
        gs = _dot_nt(kmean_scr[h].astype(BF16), q)
        blk_id = lax.broadcasted_iota(jnp.int32, gs.shape, 0)
        valid = blk_id < i
        vals = jnp.where(valid, gs, -jnp.inf)
        sel = jnp.zeros(gs.shape, dtype=jnp.bool_)
        for _ in range(topk):
            best = jnp.max(vals, axis=0, keepdims=True)
            first = jnp.min(jnp.where(vals == best, blk_id, nb), axis=0, keepdims=True)
            pick = blk_id == first
            sel = sel | pick
            vals = jnp.where(pick, -jnp.inf, vals)
        sel = sel & valid
        bias_t = jnp.where(sel, 0.0, MASK_VALUE).astype(F32)
        bias_t = jnp.concatenate([bias_t, jnp.zeros((LANES - nb, blk), F32)], axis=0)
        q_t = q.astype(F32).T.astype(BF16)
        return jnp.concatenate([q_t, bias_t.astype(BF16)], axis=0)

    q_aug_t = [gate(h) for h in range(heads)]

    def attend(n_past):
        own_slot = n_past
        r = lax.broadcasted_iota(jnp.int32, (blk, blk), 0)
        c = lax.broadcasted_iota(jnp.int32, (blk, blk), 1)
        m = []
        for h in range(heads):
            s = _dot(kaug_scr[h, i, :, :dh], q_aug_t[h][:dh]) * exp2_scale
            s = jnp.where(r <= c, s, MASK_VALUE)
            s_scr[h, own_slot] = s
            mh = jnp.max(s, axis=0, keepdims=True)
            for j in range(n_past):
                s = _dot(kaug_scr[h, j], q_aug_t[h]) * exp2_scale
                s_scr[h, j] = s
                mh = jnp.maximum(mh, jnp.max(s, axis=0, keepdims=True))
            m.append(mh)

        for h in range(heads):
            l = jnp.zeros_like(m[h])
            acc = jnp.zeros((dh, blk), F32)
            for slot in range(n_past + 1):
                p = jnp.exp2(s_scr[h, slot] - m[h])
                l = l + jnp.sum(p, axis=0, keepdims=True)
                vt = vt_scr[h, i] if slot == own_slot else vt_scr[h, slot]
                acc = acc + _dot(vt, p.astype(BF16))
            o_ref[:, head_lanes(h)] = (acc / l).T.astype(o_ref.dtype)

    n_groups = (i + group - 1) // group
    for n in range(nb // group + 1):
        pl.when(n_groups == n)(functools.partial(attend, n * group))


def _moba_attention(q, k, v, side, *, heads):
    b, seq, _ = q.shape
    dh, blk = MOBA_HEAD_DIM, MOBA_BLOCK
    nb = seq // blk
    hps = MOBA_HEADS_PER_STEP
    grid = (b, heads // hps, nb)
    q_spec = pl.BlockSpec((None, blk, hps * dh), lambda bi, h, i: (bi, i, h))
    kv_spec = pl.BlockSpec((None, seq, hps * dh), lambda bi, h, i: (bi, 0, h))
    side_specs = _side_cast_specs(side, grid[0] * grid[1] * grid[2],
                                  lambda bi, h, i: (bi * grid[1] + h) * grid[2] + i)
    out, *side_bf16 = pl.pallas_call(
        functools.partial(_moba_kernel, topk=min(MOBA_TOPK, nb), group=MOBA_LOOP_GROUP,
                          n_side=len(side)),
        grid=grid,
        in_specs=[q_spec, kv_spec, kv_spec] + side_specs,
        out_specs=[q_spec] + side_specs,
        out_shape=[jax.ShapeDtypeStruct(q.shape, BF16)] + _bf16_like(side),
        scratch_shapes=[pltpu.VMEM((hps, nb, blk, dh + LANES), BF16),
                        pltpu.VMEM((hps, nb, dh, blk), BF16),
                        pltpu.VMEM((hps, nb, dh), F32),
                        pltpu.VMEM((hps, nb + 1, blk, blk), F32)],
        compiler_params=pltpu.CompilerParams(
            dimension_semantics=("arbitrary", "arbitrary", "arbitrary"),
            vmem_limit_bytes=VMEM_LIMIT),
        name="moba_attention",
    )(q, k, v, *side)
    return out, side_bf16


def _swa_kernel(*refs, q_lane_blocks, n_side):
    sink_ref, q_ref, kp_ref, kc_ref, vp_ref, vc_ref = refs[:6]
    o_ref = refs[6 + n_side]
    k_scr, vt_scr, s_scr = refs[7 + 2 * n_side:]
    _side_cast(refs[6:6 + n_side], refs[7 + n_side:7 + 2 * n_side])

    i = pl.program_id(1)
    w = kp_ref.shape[0]
    tq = q_ref.shape[0]
    sub_blocks = tq // w
    half = LANES // 2
    scale = half ** -0.5
    groups = q_lane_blocks // SWA_KV_HEADS

    k_scr[:w, :] = kp_ref[...]
    k_scr[w:, :] = kc_ref[...]
    transposed = lambda x: x.astype(F32).T.astype(BF16)
    vt_scr[0] = transposed(vp_ref[...])
    for c in range(sub_blocks):
        vt_scr[c + 1] = transposed(vc_ref[c * w:(c + 1) * w, :])

    key = lax.broadcasted_iota(jnp.int32, (2 * w, 2 * w), 0)
    col = lax.broadcasted_iota(jnp.int32, (2 * w, 2 * w), 1)
    qry = jnp.where(col >= w, col - w, col)
    band = (key > qry) & (key <= qry + w)
    band_bias = jnp.where(band, 0.0, MASK_VALUE)
    first_bias = jnp.where(band & (key >= w), 0.0, MASK_VALUE)
    odd_head = lax.broadcasted_iota(jnp.int32, (1, 2 * w), 1) >= w
    zeros = jnp.zeros((half, 2 * w), BF16)

    def sub_block(u, carry):
        start = pl.multiple_of(u * w, w)
        bias = jnp.where((i * sub_blocks + u) == 0, first_bias, band_bias)
        k2 = k_scr[pl.ds(start, 2 * w), :]
        vt2 = jnp.concatenate([vt_scr[u], vt_scr[u + 1]], axis=1)
        lanes = lambda jb: slice(jb * LANES, (jb + 1) * LANES)

        sinks, maxima = [], []
        for jb in range(q_lane_blocks):
            qt = (q_ref[pl.ds(start, w), lanes(jb)].astype(F32) * scale).T.astype(BF16)
            q2 = jnp.concatenate([qt[:half], qt[half:]], axis=1)
            rhs = jnp.concatenate([q2, zeros] if jb < groups else [zeros, q2], axis=0)
            s = _dot(k2, rhs) + bias
            s_scr[jb] = s
            sink = jnp.where(odd_head, sink_ref[2 * jb + 1], sink_ref[2 * jb])
            sinks.append(sink)
            maxima.append(jnp.maximum(jnp.max(s, axis=0, keepdims=True), sink))

        for jb in range(q_lane_blocks):
            kvh = jb // groups
            p = jnp.exp(s_scr[jb] - maxima[jb])
            den = jnp.sum(p, axis=0, keepdims=True) + jnp.exp(sinks[jb] - maxima[jb])
            o = _dot(vt2[kvh * half:(kvh + 1) * half], p.astype(BF16)) / den
            o_ref[pl.ds(start, w), lanes(jb)] = jnp.concatenate(
                [o[:, :w], o[:, w:]], axis=0).T.astype(o_ref.dtype)
        return carry

    lax.fori_loop(0, sub_blocks, sub_block, 0)


def _swa_attention(q, k, v, sinks, side):
    b, seq, qw = q.shape
    w = SWA_WINDOW
    tq = 512
    sub_blocks = tq // w
    grid = (b, seq // tq)
    q_spec = pl.BlockSpec((None, tq, qw), lambda bi, i: (bi, i, 0))
    cur = pl.BlockSpec((None, tq, LANES), lambda bi, i: (bi, i, 0))
    prev = pl.BlockSpec((None, w, LANES), lambda bi, i: (bi, jnp.maximum(i * sub_blocks - 1, 0), 0))
    side_specs = _side_cast_specs(side, grid[0] * grid[1], lambda bi, i: bi * grid[1] + i)
    out, *side_bf16 = pl.pallas_call(
        functools.partial(_swa_kernel, q_lane_blocks=qw // LANES, n_side=len(side)),
        grid=grid,
        in_specs=[pl.BlockSpec(memory_space=pltpu.SMEM), q_spec, prev, cur, prev, cur] + side_specs,
        out_specs=[q_spec] + side_specs,
        out_shape=[jax.ShapeDtypeStruct(q.shape, BF16)] + _bf16_like(side),
        scratch_shapes=[pltpu.VMEM((tq + w, LANES), BF16),
                        pltpu.VMEM((sub_blocks + 1, LANES, w), BF16),
                        pltpu.VMEM((qw // LANES, 2 * w, 2 * w), F32)],
        compiler_params=pltpu.CompilerParams(
            dimension_semantics=("arbitrary", "arbitrary"), vmem_limit_bytes=VMEM_LIMIT),
        name="swa_attention",
    )(sinks, q, k, k, v, v, *side)
    return out, side_bf16


def _merge_kernel(ya_ref, yb_ref, ga_ref, gb_ref, x_ref, g_ref, wa_ref, wb_ref, wo_ref, o_ref):
    merged = (_sigmoid(ga_ref[...].astype(F32)) * _dot(ya_ref[...], wa_ref[...])
              + _sigmoid(gb_ref[...].astype(F32)) * _dot(yb_ref[...], wb_ref[...]))
    o_ref[...] = x_ref[...] + g_ref[...] * _dot(merged.astype(BF16), wo_ref[...])


def _merge(ya, yb, ga, gb, x2d, gate, wa, wb, wo, *, seq):
    m, d = x2d.shape
    tm = 256
    tiles_per_seq = seq // tm
    row = lambda i: (i, 0)
    const = lambda i: (0, 0)
    resident = lambda shape: pl.BlockSpec(shape, const, pipeline_mode=pl.Buffered(1))
    return pl.pallas_call(
        _merge_kernel,
        grid=(m // tm,),
        in_specs=[
            pl.BlockSpec((tm, ya.shape[1]), row),
            pl.BlockSpec((tm, yb.shape[1]), row),
            pl.BlockSpec((tm, d), row),
            pl.BlockSpec((tm, d), row),
            pl.BlockSpec((tm, d), row),
            pl.BlockSpec((None, 1, d), lambda i: (i // tiles_per_seq, 0, 0)),
            resident(wa.shape), resident(wb.shape), resident(wo.shape),
        ],
        out_specs=pl.BlockSpec((tm, d), row),
        out_shape=jax.ShapeDtypeStruct((m, d), F32),
        compiler_params=pltpu.CompilerParams(
            dimension_semantics=("arbitrary",), vmem_limit_bytes=VMEM_LIMIT),
        name="merge_out_projection",
    )(ya, yb, ga, gb, x2d, gate, wa, wb, wo)


def kernel(x, c, w_ada, b_ada, norm_ffn1, ffn1_gate, ffn1_up, ffn1_down, norm_mix, w_in, swa_sinks,
           w_branch_moba, w_branch_swa, w_out, norm_ffn2, ffn2_gate, ffn2_up, ffn2_down, norm_final):
    b, seq, d = x.shape
    depth = w_ada.shape[0]
    moba_w = w_branch_moba.shape[1]
    swa_qw = w_branch_swa.shape[1]
    kv_w = SWA_KV_HEADS * SWA_HEAD_DIM
    qkv_cols = 3 * moba_w + swa_qw
    assert w_in.shape[2] == qkv_cols + 2 * kv_w + 2 * d
    assert moba_w == swa_qw == 1024 and d == 2048 and kv_w == LANES

    tables = _rope_tables(seq)
    c_pad = jnp.pad(c, ((0, 8 - b), (0, 0)))
    x2d = x.reshape(b * seq, d)

    for l in range(depth):
        mod = _ada_modulation(c_pad, w_ada[l], b_ada[l])[:b].reshape(b, N_ADA, d)
        sh2, sc2, g2 = [mod[:, t:t + 1] for t in (3, 4, 5)]
        gains = jnp.stack([norm_ffn1[l], norm_mix[l], norm_ffn2[l], norm_final])
        vec = jnp.concatenate(
            [mod, jnp.broadcast_to(gains, (b,) + gains.shape),
             jnp.zeros((b, VEC_ROWS - N_ADA - gains.shape[0], d), F32)], axis=1)

        gate1, up1 = _cast_bf16([ffn1_gate[l], ffn1_up[l]], block_rows=256)
        down1, = _cast_bf16([ffn1_down[l]], block_rows=512)
        w_main, w_kv = _split_w_in(w_in[l], qkv_cols, 2 * kv_w)

        x2d = _ffn(x2d, vec, gate1, up1, down1, seq=seq, norm_row=VEC_NORM_FFN1_ROW, ada_row=0,
                   final_norm=False)

        qa, ka, va, qb, kb, vb, ga, gb = _in_projection(
            x2d, norm_mix[l], sh2, sc2, w_main, w_kv, tables, seq=seq)

        rs = lambda t: t.reshape(b, seq, t.shape[-1])
        ya, (gate2, up2, down2) = _moba_attention(
            rs(qa), rs(ka), rs(va), [ffn2_gate[l], ffn2_up[l], ffn2_down[l]],
            heads=moba_w // MOBA_HEAD_DIM)
        yb, (wa, wb, wo) = _swa_attention(
            rs(qb), rs(kb), rs(vb), swa_sinks[l], [w_branch_moba[l], w_branch_swa[l], w_out[l]])

        x2d = _merge(ya.reshape(b * seq, moba_w), yb.reshape(b * seq, swa_qw), ga, gb, x2d, g2,
                     wa, wb, wo, seq=seq)

        x2d = _ffn(x2d, vec, gate2, up2, down2, seq=seq, norm_row=VEC_NORM_FFN2_ROW, ada_row=6,
                   final_norm=(l == depth - 1))

    return x2d.reshape(b, seq, d)
```

```python
import functools

import jax
import jax.numpy as jnp
from jax import lax
from jax.experimental import pallas as pl
from jax.experimental.pallas import tpu as pltpu

MOBA_HEAD_DIM = 128
MOBA_BLOCK = 256
MOBA_TOPK = 3
SWA_HEAD_DIM = 64
SWA_KV_HEADS = 2
SWA_WINDOW = 128
ROPE_THETA = 10000.0
EPS = 1e-6
N_ADA = 9
VEC_NORM_FFN1_ROW, VEC_NORM_MIX_ROW, VEC_NORM_FFN2_ROW, VEC_FINAL_NORM_ROW = 9, 10, 11, 12
VEC_ROWS = 16

LANES = 128
BF16_SUBLANES = 16
ROW_CHUNK = 16
VMEM_LIMIT = 56 * 1024 * 1024
MASK_VALUE = -1e30
LOG2_E = 1.4426950408889634
MOBA_LOOP_GROUP = 2
MOBA_HEADS_PER_STEP = 4

F32 = jnp.float32
BF16 = jnp.bfloat16


def _sigmoid(x):
    return 1.0 / (1.0 + jnp.exp(-x))


def _row_rsqrt_mean_square(x):
    r = lax.rsqrt(jnp.mean(x * x, axis=-1, keepdims=True) + EPS)
    return jnp.broadcast_to(r, (x.shape[0], LANES))


def _across_lanes(r, width):
    return jnp.concatenate([r] * (width // LANES), axis=1)


def _for_row_chunks(rows, body):
    def step(c, carry):
        body(pl.ds(pl.multiple_of(c * ROW_CHUNK, ROW_CHUNK), ROW_CHUNK))
        return carry
    lax.fori_loop(0, rows // ROW_CHUNK, step, 0, unroll=4)


def _rms_modulate_to(h_ref, x_ref, rinv_scr, norm_w, shift, scale):
    rinv_scr[...] = _row_rsqrt_mean_square(x_ref[...])
    gain = norm_w * (1.0 + scale)

    def slab(rows):
        rinv = _across_lanes(rinv_scr[rows, :], x_ref.shape[1])
        h_ref[rows, :] = (x_ref[rows, :] * rinv * gain + shift).astype(BF16)

    _for_row_chunks(x_ref.shape[0], slab)


def _dot(a, b):
    return jnp.dot(a, b, preferred_element_type=F32)


def _dot_nt(a, b):
    return lax.dot_general(a, b, (((1,), (1,)), ((), ())), preferred_element_type=F32)


def _cast_kernel(*refs):
    n = len(refs) // 2
    for src, dst in zip(refs[:n], refs[n:]):
        dst[...] = src[...].astype(dst.dtype)


def _cast_bf16(arrays, block_rows):
    rows, cols = arrays[0].shape
    spec = pl.BlockSpec((block_rows, cols), lambda r: (r, 0))
    n = len(arrays)
    return pl.pallas_call(
        _cast_kernel,
        grid=(rows // block_rows,),
        in_specs=[spec] * n,
        out_specs=[spec] * n,
        out_shape=[jax.ShapeDtypeStruct((rows, cols), BF16)] * n,
        compiler_params=pltpu.CompilerParams(
            dimension_semantics=("arbitrary",), vmem_limit_bytes=VMEM_LIMIT),
        name="cast_bf16",
    )(*arrays)


def _side_cast_specs(arrays, steps, step_index):
    specs = []
    for a in arrays:
        rows, cols = a.shape
        for col_blocks in (1, 2, 4, 8, 16):
            row_blocks = steps // col_blocks
            if rows % (row_blocks * BF16_SUBLANES) == 0 and cols % (col_blocks * LANES) == 0:
                break
        else:
            raise ValueError(f"no {steps}-block tiling for {a.shape}")
        index = lambda *g, cb=col_blocks: (step_index(*g) // cb, step_index(*g) % cb)
        specs.append(pl.BlockSpec((rows // row_blocks, cols // col_blocks), index))
    return specs


def _side_cast(src_refs, dst_refs):
    for src, dst in zip(src_refs, dst_refs):
        dst[...] = src[...].astype(dst.dtype)


def _bf16_like(arrays):
    return [jax.ShapeDtypeStruct(a.shape, BF16) for a in arrays]


def _split_w_in_kernel(w_ref, main_ref, kv_ref, *, kv_block):
    j = pl.program_id(0)

    @pl.when(j != kv_block)
    def _():
        main_ref[...] = w_ref[...].astype(BF16)

    @pl.when(j == kv_block)
    def _():
        kv_ref[...] = w_ref[...].astype(BF16)


def _split_w_in(w, kv_start, kv_width):
    d, cols = w.shape
    assert kv_start % kv_width == 0 and cols % kv_width == 0
    kv_block = kv_start // kv_width
    main_col = lambda j: (0, jnp.where(j > kv_block, j - 1, jnp.minimum(j, kv_block - 1)))
    return pl.pallas_call(
        functools.partial(_split_w_in_kernel, kv_block=kv_block),
        grid=(cols // kv_width,),
        in_specs=[pl.BlockSpec((d, kv_width), lambda j: (0, j))],
        out_specs=[pl.BlockSpec((d, kv_width), main_col),
                   pl.BlockSpec((d, kv_width), lambda j: (0, 0))],
        out_shape=[jax.ShapeDtypeStruct((d, cols - kv_width), BF16),
                   jax.ShapeDtypeStruct((d, kv_width), BF16)],
        compiler_params=pltpu.CompilerParams(
            dimension_semantics=("arbitrary",), vmem_limit_bytes=VMEM_LIMIT),
        name="split_w_in",
    )(w)


def _ada_kernel(c_ref, w_ref, b_ref, o_ref):
    c = c_ref[...]
    s = (c * _sigmoid(c)).astype(BF16)
    o_ref[...] = _dot(s, w_ref[...].astype(BF16)) + b_ref[...]


def _ada_modulation(c_pad, w, b):
    rows, d = c_pad.shape
    n = w.shape[1]
    tn = 1024
    return pl.pallas_call(
        _ada_kernel,
        grid=(n // tn,),
        in_specs=[
            pl.BlockSpec((rows, d), lambda j: (0, 0)),
            pl.BlockSpec((d, tn), lambda j: (0, j)),
            pl.BlockSpec((1, tn), lambda j: (0, j)),
        ],
        out_specs=pl.BlockSpec((rows, tn), lambda j: (0, j)),
        out_shape=jax.ShapeDtypeStruct((rows, n), F32),
        compiler_params=pltpu.CompilerParams(
            dimension_semantics=("arbitrary",), vmem_limit_bytes=VMEM_LIMIT),
        name="ada_modulation",
    )(c_pad, w, b.reshape(1, n))


def _ffn_kernel(x_ref, vec_ref, wg_ref, wu_ref, wd_ref, o_ref, h_scr, acc_scr, rinv_scr, *,
                final_norm, tiles_per_seq, norm_row, ada_row):
    f = pl.program_id(1)
    nf = pl.num_programs(1)
    batch = pl.program_id(0) // tiles_per_seq
    vec = lambda r: vec_ref[batch, r:r + 1, :]

    @pl.when(f == 0)
    def _():
        _rms_modulate_to(h_scr, x_ref, rinv_scr, vec(norm_row), vec(ada_row), vec(ada_row + 1))
        acc_scr[...] = jnp.zeros_like(acc_scr)

    h = h_scr[...]
    gate = _dot(h, wg_ref[...])
    up = _dot(h, wu_ref[...])
    act = (gate * _sigmoid(gate) * up).astype(BF16)
    acc_scr[...] += _dot(act, wd_ref[...])

    @pl.when(f == nf - 1)
    def _():
        y = x_ref[...] + (0.5 * vec(ada_row + 2)) * acc_scr[...]
        o_ref[...] = y
        if final_norm:
            rinv_scr[...] = _row_rsqrt_mean_square(y)
            final_w = vec(VEC_FINAL_NORM_ROW)

            def slab(rows):
                rinv = _across_lanes(rinv_scr[rows, :], o_ref.shape[1])
                o_ref[rows, :] = o_ref[rows, :] * rinv * final_w

            _for_row_chunks(o_ref.shape[0], slab)


def _ffn(x2d, vec, wg, wu, wd, *, seq, norm_row, ada_row, final_norm):
    m, d = x2d.shape
    dff = wg.shape[1]
    tm, tf = 512, 512
    row = lambda i, f: (i, 0)
    return pl.pallas_call(
        functools.partial(_ffn_kernel, final_norm=final_norm, tiles_per_seq=seq // tm,
                          norm_row=norm_row, ada_row=ada_row),
        grid=(m // tm, dff // tf),
        in_specs=[
            pl.BlockSpec((tm, d), row),
            pl.BlockSpec(vec.shape, lambda i, f: (0, 0, 0)),
            pl.BlockSpec((d, tf), lambda i, f: (0, f)),
            pl.BlockSpec((d, tf), lambda i, f: (0, f)),
            pl.BlockSpec((tf, d), lambda i, f: (f, 0)),
        ],
        out_specs=pl.BlockSpec((tm, d), row),
        out_shape=jax.ShapeDtypeStruct((m, d), F32),
        scratch_shapes=[pltpu.VMEM((tm, d), BF16), pltpu.VMEM((tm, d), F32),
                        pltpu.VMEM((tm, LANES), F32)],
        compiler_params=pltpu.CompilerParams(
            dimension_semantics=("arbitrary", "arbitrary"), vmem_limit_bytes=VMEM_LIMIT),
        name="ffn_final" if final_norm else "ffn",
    )(x2d, vec, wg, wu, wd)


def _rope128(x, cos, sin):
    return x * cos + pltpu.roll(x, 64, 1) * sin


def _rope64(x, cos, sin_lo, sin_hi):
    return x * cos + pltpu.roll(x, 96, 1) * sin_lo + pltpu.roll(x, 32, 1) * sin_hi


def _proj_kernel(x_ref, nw_ref, sh_ref, sc_ref, w_ref, wkv_ref,
                 cos_a_ref, sin_a_ref, cos_b_ref, sin_lo_ref, sin_hi_ref,
                 qa_ref, ka_ref, va_ref, qb_ref, kb_ref, vb_ref, ga_ref, gb_ref, h_scr, rinv_scr):
    n = pl.program_id(1)

    @pl.when(n == 0)
    def _():
        _rms_modulate_to(h_scr, x_ref, rinv_scr, nw_ref[...], sh_ref[...], sc_ref[...])

    h = h_scr[...]
    chunk = 2 * LANES
    identity = lambda x: x

    def project(dst_ref, epilogue, col0):
        for s in range(0, dst_ref.shape[1], chunk):
            acc = _dot(h, w_ref[:, col0 + s:col0 + s + chunk])
            for t in range(0, chunk, LANES):
                dst_ref[:, s + t:s + t + LANES] = epilogue(acc[:, t:t + LANES]).astype(BF16)

    rope_a = lambda x: _rope128(x, cos_a_ref[...], sin_a_ref[...])
    rope_b = lambda x: _rope64(x, cos_b_ref[...], sin_lo_ref[...], sin_hi_ref[...])

    @pl.when(n == 0)
    def _():
        project(qa_ref, rope_a, 0)
        project(ka_ref, rope_a, qa_ref.shape[1])

    @pl.when(n == 1)
    def _():
        project(va_ref, identity, 0)
        project(qb_ref, rope_b, va_ref.shape[1])
        kv = _dot(h, wkv_ref[...])
        kb_ref[...] = rope_b(kv[:, :LANES]).astype(BF16)
        vb_ref[...] = kv[:, LANES:].astype(BF16)

    @pl.when(n == 2)
    def _():
        project(ga_ref, identity, 0)

    @pl.when(n == 3)
    def _():
        project(gb_ref, identity, 0)


def _in_projection(x2d, norm_w, shift, scale, w_main, w_kv, tables, *, seq):
    m, d = x2d.shape
    tm, tn, half = 512, 2048, 1024
    n_tiles = w_main.shape[1] // tn
    assert n_tiles == 4 and d == tn
    tiles_per_seq = seq // tm
    per_batch = lambda i, n: (i // tiles_per_seq, 0, 0)
    pos = lambda i, n: (i % tiles_per_seq, 0)
    full = lambda i, n: (i, 0)
    wide = lambda cols: jax.ShapeDtypeStruct((m, cols), BF16)
    table_spec = pl.BlockSpec((tm, LANES), pos)
    return pl.pallas_call(
        _proj_kernel,
        grid=(m // tm, n_tiles),
        in_specs=[
            pl.BlockSpec((tm, d), full),
            pl.BlockSpec((1, d), lambda i, n: (0, 0)),
            pl.BlockSpec((None, 1, d), per_batch),
            pl.BlockSpec((None, 1, d), per_batch),
            pl.BlockSpec((d, tn), lambda i, n: (0, n)),
            pl.BlockSpec((d, 2 * LANES), lambda i, n: (0, 0)),
            table_spec, table_spec, table_spec, table_spec, table_spec,
        ],
        out_specs=[
            pl.BlockSpec((tm, half), full),
            pl.BlockSpec((tm, half), full),
            pl.BlockSpec((tm, half), full),
            pl.BlockSpec((tm, half), full),
            pl.BlockSpec((tm, LANES), full),
            pl.BlockSpec((tm, LANES), full),
            pl.BlockSpec((tm, tn), full),
            pl.BlockSpec((tm, tn), full),
        ],
        out_shape=[wide(half), wide(half), wide(half), wide(half), wide(LANES), wide(LANES),
                   wide(tn), wide(tn)],
        scratch_shapes=[pltpu.VMEM((tm, d), BF16), pltpu.VMEM((tm, LANES), F32)],
        compiler_params=pltpu.CompilerParams(
            dimension_semantics=("arbitrary", "arbitrary"), vmem_limit_bytes=VMEM_LIMIT),
        name="in_projection",
    )(x2d, norm_w.reshape(1, d), shift, scale, w_main, w_kv, *tables)


def _rope_tables(seq):
    pos = jnp.arange(seq, dtype=F32)[:, None]

    def cos_sin(half):
        inv = ROPE_THETA ** (-jnp.arange(half, dtype=F32) / half)
        ang = pos * inv[None, :]
        return jnp.cos(ang), jnp.sin(ang)

    cos, sin = cos_sin(MOBA_HEAD_DIM // 2)
    cos_a = jnp.concatenate([cos, cos], axis=1)
    sin_a = jnp.concatenate([-sin, sin], axis=1)
    cos, sin = cos_sin(SWA_HEAD_DIM // 2)
    zero = jnp.zeros_like(sin)
    cos_b = jnp.concatenate([cos, cos, cos, cos], axis=1)
    sin_lo = jnp.concatenate([-sin, zero, -sin, zero], axis=1)
    sin_hi = jnp.concatenate([zero, sin, zero, sin], axis=1)
    return cos_a, sin_a, cos_b, sin_lo, sin_hi


def _moba_kernel(*refs, topk, group, n_side):
    q_ref, k_ref, v_ref = refs[:3]
    o_ref = refs[3 + n_side]
    kaug_scr, vt_scr, kmean_scr, s_scr = refs[4 + 2 * n_side:]
    _side_cast(refs[3:3 + n_side], refs[4 + n_side:4 + 2 * n_side])

    i = pl.program_id(2)
    blk = q_ref.shape[0]
    heads, nb, dh, _ = vt_scr.shape
    exp2_scale = dh ** -0.5 * LOG2_E
    head_lanes = lambda h: slice(h * dh, (h + 1) * dh)

    @pl.when(i == 0)
    def _():
        col = lax.broadcasted_iota(jnp.int32, (blk, LANES), 1)
        for h in range(heads):
            for j in range(nb):
                k = k_ref[j * blk:(j + 1) * blk, head_lanes(h)]
                kmean_scr[h, j:j + 1, :] = jnp.mean(k.astype(F32), axis=0, keepdims=True)
                kaug_scr[h, j, :, :dh] = k
                kaug_scr[h, j, :, dh:] = jnp.where(col == j, 1.0, 0.0).astype(BF16)
                v = v_ref[j * blk:(j + 1) * blk, head_lanes(h)]
                vt_scr[h, j] = v.astype(F32).T.astype(BF16)

    def gate(h):
        q = q_ref[:, head_lanes(h)]
        gs = _dot_nt(kmean_scr[h].astype(BF16), q)
        blk_id = lax.broadcasted_iota(jnp.int32, gs.shape, 0)
        valid = blk_id < i
        vals = jnp.where(valid, gs, -jnp.inf)
        sel = jnp.zeros(gs.shape, dtype=jnp.bool_)
        for _ in range(topk):
            best = jnp.max(vals, axis=0, keepdims=True)
            first = jnp.min(jnp.where(vals == best, blk_id, nb), axis=0, keepdims=True)
            pick = blk_id == first
            sel = sel | pick
            vals = jnp.where(pick, -jnp.inf, vals)
        sel = sel & valid
        bias_t = jnp.where(sel, 0.0, MASK_VALUE).astype(F32)
        bias_t = jnp.concatenate([bias_t, jnp.zeros((LANES - nb, blk), F32)], axis=0)
        q_t = q.astype(F32).T.astype(BF16)
        return jnp.concatenate([q_t, bias_t.astype(BF16)], axis=0)

    q_aug_t = [gate(h) for h in range(heads)]

    def attend(n_past):
        own_slot = n_past
        r = lax.broadcasted_iota(jnp.int32, (blk, blk), 0)
        c = lax.broadcasted_iota(jnp.int32, (blk, blk), 1)
        m = []
        for h in range(heads):
            s = _dot(kaug_scr[h, i, :, :dh], q_aug_t[h][:dh]) * exp2_scale
            s = jnp.where(r <= c, s, MASK_VALUE)
            s_scr[h, own_slot] = s
            mh = jnp.max(s, axis=0, keepdims=True)
            for j in range(n_past):
                s = _dot(kaug_scr[h, j], q_aug_t[h]) * exp2_scale
                s_scr[h, j] = s
                mh = jnp.maximum(mh, jnp.max(s, axis=0, keepdims=True))
            m.append(mh)

        for h in range(heads):
            l = jnp.zeros_like(m[h])
            acc = jnp.zeros((dh, blk), F32)
            for slot in range(n_past + 1):
                p = jnp.exp2(s_scr[h, slot] - m[h])
                l = l + jnp.sum(p, axis=0, keepdims=True)
                vt = vt_scr[h, i] if slot == own_slot else vt_scr[h, slot]
                acc = acc + _dot(vt, p.astype(BF16))
            o_ref[:, head_lanes(h)] = (acc / l).T.astype(o_ref.dtype)

    n_groups = (i + group - 1) // group
    for n in range(nb // group + 1):
        pl.when(n_groups == n)(functools.partial(attend, n * group))


def _moba_attention(q, k, v, side, *, heads):
    b, seq, _ = q.shape
    dh, blk = MOBA_HEAD_DIM, MOBA_BLOCK
    nb = seq // blk
    hps = MOBA_HEADS_PER_STEP
    grid = (b, heads // hps, nb)
    q_spec = pl.BlockSpec((None, blk, hps * dh), lambda bi, h, i: (bi, i, h))
    kv_spec = pl.BlockSpec((None, seq, hps * dh), lambda bi, h, i: (bi, 0, h))
    side_specs = _side_cast_specs(side, grid[0] * grid[1] * grid[2],
                                  lambda bi, h, i: (bi * grid[1] + h) * grid[2] + i)
    out, *side_bf16 = pl.pallas_call(
        functools.partial(_moba_kernel, topk=min(MOBA_TOPK, nb), group=MOBA_LOOP_GROUP,
                          n_side=len(side)),
        grid=grid,
        in_specs=[q_spec, kv_spec, kv_spec] + side_specs,
        out_specs=[q_spec] + side_specs,
        out_shape=[jax.ShapeDtypeStruct(q.shape, BF16)] + _bf16_like(side),
        scratch_shapes=[pltpu.VMEM((hps, nb, blk, dh + LANES), BF16),
                        pltpu.VMEM((hps, nb, dh, blk), BF16),
                        pltpu.VMEM((hps, nb, dh), F32),
                        pltpu.VMEM((hps, nb + 1, blk, blk), F32)],
        compiler_params=pltpu.CompilerParams(
            dimension_semantics=("arbitrary", "arbitrary", "arbitrary"),
            vmem_limit_bytes=VMEM_LIMIT),
        name="moba_attention",
    )(q, k, v, *side)
    return out, side_bf16


def _swa_kernel(*refs, q_lane_blocks, n_side):
    sink_ref, q_ref, kp_ref, kc_ref, vp_ref, vc_ref = refs[:6]
    o_ref = refs[6 + n_side]
    k_scr, vt_scr, s_scr, qt_scr, ot_scr = refs[7 + 2 * n_side:]
    _side_cast(refs[6:6 + n_side], refs[7 + n_side:7 + 2 * n_side])

    i = pl.program_id(1)
    w = kp_ref.shape[0]
    tq = q_ref.shape[0]
    sub_blocks = tq // w
    half = LANES // 2
    scale = half ** -0.5
    groups = q_lane_blocks // SWA_KV_HEADS
    lanes = lambda jb: slice(jb * LANES, (jb + 1) * LANES)
    zeros = jnp.zeros((half, 2 * w), BF16)

    k_scr[:w, :] = kp_ref[...]
    k_scr[w:, :] = kc_ref[...]
    transposed = lambda x: x.astype(F32).T.astype(BF16)
    vt_scr[0] = transposed(vp_ref[...])
    for c in range(sub_blocks):
        vt_scr[c + 1] = transposed(vc_ref[c * w:(c + 1) * w, :])
    for u in range(sub_blocks):
        for jb in range(q_lane_blocks):
            qt = (q_ref[u * w:(u + 1) * w, lanes(jb)].astype(F32) * scale).T.astype(BF16)
            q2 = jnp.concatenate([qt[:half], qt[half:]], axis=1)
            qt_scr[u * q_lane_blocks + jb] = jnp.concatenate(
                [q2, zeros] if jb < groups else [zeros, q2], axis=0)

    key = lax.broadcasted_iota(jnp.int32, (w, 2 * w), 0)
    col = lax.broadcasted_iota(jnp.int32, (w, 2 * w), 1)
    from_prev = key > jnp.where(col >= w, col - w, col)
    odd_head = lax.broadcasted_iota(jnp.int32, (1, 2 * w), 1) >= w

    def sub_block(u):
        prev_bias = jnp.where((i * sub_blocks + u) == 0, MASK_VALUE, 0.0)
        k2 = k_scr[u * w:(u + 2) * w, :]
        vt2 = jnp.concatenate([vt_scr[u], vt_scr[u + 1]], axis=1)

        sinks, maxima = [], []
        for jb in range(q_lane_blocks):
            s = _dot(k2, qt_scr[u * q_lane_blocks + jb])
            s = jnp.where(from_prev, s[:w] + prev_bias, s[w:])
            s_scr[u * q_lane_blocks + jb] = s
            sink = jnp.where(odd_head, sink_ref[2 * jb + 1], sink_ref[2 * jb])
            sinks.append(sink)
            maxima.append(jnp.maximum(jnp.max(s, axis=0, keepdims=True), sink))

        for jb in range(q_lane_blocks):
            kvh = jb // groups
            p = jnp.exp(s_scr[u * q_lane_blocks + jb] - maxima[jb])
            den = jnp.sum(p, axis=0, keepdims=True) + jnp.exp(sinks[jb] - maxima[jb])
            p2 = jnp.concatenate([jnp.where(from_prev, p, 0.0), jnp.where(from_prev, 0.0, p)],
                                 axis=0).astype(BF16)
            o = _dot(vt2[kvh * half:(kvh + 1) * half], p2) / den
            ot_scr[u * q_lane_blocks + jb] = jnp.concatenate([o[:, :w], o[:, w:]], axis=0)

    for u in range(sub_blocks):
        sub_block(u)

    for u in range(sub_blocks):
        for jb in range(q_lane_blocks):
            o_ref[u * w:(u + 1) * w, lanes(jb)] = ot_scr[u * q_lane_blocks + jb].T.astype(
                o_ref.dtype)


def _swa_attention(q, k, v, sinks, side):
    b, seq, qw = q.shape
    w = SWA_WINDOW
    tq = 512
    sub_blocks = tq // w
    grid = (b, seq // tq)
    q_spec = pl.BlockSpec((None, tq, qw), lambda bi, i: (bi, i, 0))
    cur = pl.BlockSpec((None, tq, LANES), lambda bi, i: (bi, i, 0))
    prev = pl.BlockSpec((None, w, LANES), lambda bi, i: (bi, jnp.maximum(i * sub_blocks - 1, 0), 0))
    side_specs = _side_cast_specs(side, grid[0] * grid[1], lambda bi, i: bi * grid[1] + i)
    out, *side_bf16 = pl.pallas_call(
        functools.partial(_swa_kernel, q_lane_blocks=qw // LANES, n_side=len(side)),
        grid=grid,
        in_specs=[pl.BlockSpec(memory_space=pltpu.SMEM), q_spec, prev, cur, prev, cur] + side_specs,
        out_specs=[q_spec] + side_specs,
        out_shape=[jax.ShapeDtypeStruct(q.shape, BF16)] + _bf16_like(side),
        scratch_shapes=[pltpu.VMEM((tq + w, LANES), BF16),
                        pltpu.VMEM((sub_blocks + 1, LANES, w), BF16),
                        pltpu.VMEM((sub_blocks * (qw // LANES), w, 2 * w), F32),
                        pltpu.VMEM((sub_blocks * (qw // LANES), LANES, 2 * w), BF16),
                        pltpu.VMEM((sub_blocks * (qw // LANES), LANES, w), F32)],
        compiler_params=pltpu.CompilerParams(
            dimension_semantics=("arbitrary", "arbitrary"), vmem_limit_bytes=VMEM_LIMIT),
        name="swa_attention",
    )(sinks, q, k, k, v, v, *side)
    return out, side_bf16


def _merge_kernel(ya_ref, yb_ref, ga_ref, gb_ref, x_ref, g_ref, wa_ref, wb_ref, wo_ref, o_ref):
    merged = (_sigmoid(ga_ref[...].astype(F32)) * _dot(ya_ref[...], wa_ref[...])
              + _sigmoid(gb_ref[...].astype(F32)) * _dot(yb_ref[...], wb_ref[...]))
    o_ref[...] = x_ref[...] + g_ref[...] * _dot(merged.astype(BF16), wo_ref[...])


def _merge(ya, yb, ga, gb, x2d, gate, wa, wb, wo, *, seq):
    m, d = x2d.shape
    tm = 256
    tiles_per_seq = seq // tm
    row = lambda i: (i, 0)
    const = lambda i: (0, 0)
    resident = lambda shape: pl.BlockSpec(shape, const, pipeline_mode=pl.Buffered(1))
    return pl.pallas_call(
        _merge_kernel,
        grid=(m // tm,),
        in_specs=[
            pl.BlockSpec((tm, ya.shape[1]), row),
            pl.BlockSpec((tm, yb.shape[1]), row),
            pl.BlockSpec((tm, d), row),
            pl.BlockSpec((tm, d), row),
            pl.BlockSpec((tm, d), row),
            pl.BlockSpec((None, 1, d), lambda i: (i // tiles_per_seq, 0, 0)),
            resident(wa.shape), resident(wb.shape), resident(wo.shape),
        ],
        out_specs=pl.BlockSpec((tm, d), row),
        out_shape=jax.ShapeDtypeStruct((m, d), F32),
        compiler_params=pltpu.CompilerParams(
            dimension_semantics=("arbitrary",), vmem_limit_bytes=VMEM_LIMIT),
        name="merge_out_projection",
    )(ya, yb, ga, gb, x2d, gate, wa, wb, wo)


def kernel(x, c, w_ada, b_ada, norm_ffn1, ffn1_gate, ffn1_up, ffn1_down, norm_mix, w_in, swa_sinks,
           w_branch_moba, w_branch_swa, w_out, norm_ffn2, ffn2_gate, ffn2_up, ffn2_down, norm_final):
    b, seq, d = x.shape
    depth = w_ada.shape[0]
    moba_w = w_branch_moba.shape[1]
    swa_qw = w_branch_swa.shape[1]
    kv_w = SWA_KV_HEADS * SWA_HEAD_DIM
    qkv_cols = 3 * moba_w + swa_qw
    assert w_in.shape[2] == qkv_cols + 2 * kv_w + 2 * d
    assert moba_w == swa_qw == 1024 and d == 2048 and kv_w == LANES

    tables = _rope_tables(seq)
    c_pad = jnp.pad(c, ((0, 8 - b), (0, 0)))
    x2d = x.reshape(b * seq, d)

    for l in range(depth):
        mod = _ada_modulation(c_pad, w_ada[l], b_ada[l])[:b].reshape(b, N_ADA, d)
        sh2, sc2, g2 = [mod[:, t:t + 1] for t in (3, 4, 5)]
        gains = jnp.stack([norm_ffn1[l], norm_mix[l], norm_ffn2[l], norm_final])
        vec = jnp.concatenate(
            [mod, jnp.broadcast_to(gains, (b,) + gains.shape),
             jnp.zeros((b, VEC_ROWS - N_ADA - gains.shape[0], d), F32)], axis=1)

        gate1, up1 = _cast_bf16([ffn1_gate[l], ffn1_up[l]], block_rows=256)
        down1, = _cast_bf16([ffn1_down[l]], block_rows=512)
        w_main, w_kv = _split_w_in(w_in[l], qkv_cols, 2 * kv_w)

        x2d = _ffn(x2d, vec, gate1, up1, down1, seq=seq, norm_row=VEC_NORM_FFN1_ROW, ada_row=0,
                   final_norm=False)

        qa, ka, va, qb, kb, vb, ga, gb = _in_projection(
            x2d, norm_mix[l], sh2, sc2, w_main, w_kv, tables, seq=seq)

        rs = lambda t: t.reshape(b, seq, t.shape[-1])
        ya, (gate2, up2, down2) = _moba_attention(
            rs(qa), rs(ka), rs(va), [ffn2_gate[l], ffn2_up[l], ffn2_down[l]],
            heads=moba_w // MOBA_HEAD_DIM)
        yb, (wa, wb, wo) = _swa_attention(
            rs(qb), rs(kb), rs(vb), swa_sinks[l], [w_branch_moba[l], w_branch_swa[l], w_out[l]])

        x2d = _merge(ya.reshape(b * seq, moba_w), yb.reshape(b * seq, swa_qw), ga, gb, x2d, g2,
                     wa, wb, wo, seq=seq)

        x2d = _ffn(x2d, vec, gate2, up2, down2, seq=seq, norm_row=VEC_NORM_FFN2_ROW, ada_row=6,
                   final_norm=(l == depth - 1))

    return x2d.reshape(b, seq, d)
```

```python
import functools

import jax
import jax.numpy as jnp
from jax import lax
from jax.experimental import pallas as pl
from jax.experimental.pallas import tpu as pltpu

MOBA_HEAD_DIM = 128
MOBA_BLOCK = 256
MOBA_TOPK = 3
SWA_HEAD_DIM = 64
SWA_KV_HEADS = 2
SWA_WINDOW = 128
ROPE_THETA = 10000.0
EPS = 1e-6
N_ADA = 9
VEC_NORM_FFN1_ROW, VEC_NORM_MIX_ROW, VEC_NORM_FFN2_ROW, VEC_FINAL_NORM_ROW = 9, 10, 11, 12
VEC_ROWS = 16

LANES = 128
BF16_SUBLANES = 16
ROW_CHUNK = 16
VMEM_LIMIT = 56 * 1024 * 1024
MASK_VALUE = -1e30
LOG2_E = 1.4426950408889634
MOBA_LOOP_GROUP = 2
MOBA_HEADS_PER_STEP = 4

F32 = jnp.float32
BF16 = jnp.bfloat16


def _sigmoid(x):
    return 1.0 / (1.0 + jnp.exp(-x))


def _row_rsqrt_mean_square(x):
    r = lax.rsqrt(jnp.mean(x * x, axis=-1, keepdims=True) + EPS)
    return jnp.broadcast_to(r, (x.shape[0], LANES))


def _across_lanes(r, width):
    return jnp.concatenate([r] * (width // LANES), axis=1)


def _for_row_chunks(rows, body):
    for c in range(rows // ROW_CHUNK):
        body(pl.ds(c * ROW_CHUNK, ROW_CHUNK))


def _rms_modulate_to(h_ref, x_ref, rinv_scr, norm_w, shift, scale):
    rinv_scr[...] = _row_rsqrt_mean_square(x_ref[...])
    gain = norm_w * (1.0 + scale)

    def slab(rows):
        rinv = _across_lanes(rinv_scr[rows, :], x_ref.shape[1])
        h_ref[rows, :] = (x_ref[rows, :] * rinv * gain + shift).astype(BF16)

    _for_row_chunks(x_ref.shape[0], slab)


def _dot(a, b):
    return jnp.dot(a, b, preferred_element_type=F32)


def _dot_nt(a, b):
    return lax.dot_general(a, b, (((1,), (1,)), ((), ())), preferred_element_type=F32)


def _cast_kernel(*refs):
    n = len(refs) // 2
    for src, dst in zip(refs[:n], refs[n:]):
        dst[...] = src[...].astype(dst.dtype)


def _cast_bf16(arrays, block_rows):
    rows, cols = arrays[0].shape
    spec = pl.BlockSpec((block_rows, cols), lambda r: (r, 0))
    n = len(arrays)
    return pl.pallas_call(
        _cast_kernel,
        grid=(rows // block_rows,),
        in_specs=[spec] * n,
        out_specs=[spec] * n,
        out_shape=[jax.ShapeDtypeStruct((rows, cols), BF16)] * n,
        compiler_params=pltpu.CompilerParams(
            dimension_semantics=("arbitrary",), vmem_limit_bytes=VMEM_LIMIT),
        name="cast_bf16",
    )(*arrays)


def _side_cast_specs(arrays, steps, step_index):
    specs = []
    for a in arrays:
        rows, cols = a.shape
        for col_blocks in (1, 2, 4, 8, 16):
            row_blocks = steps // col_blocks
            if rows % (row_blocks * BF16_SUBLANES) == 0 and cols % (col_blocks * LANES) == 0:
                break
        else:
            raise ValueError(f"no {steps}-block tiling for {a.shape}")
        index = lambda *g, cb=col_blocks: (step_index(*g) // cb, step_index(*g) % cb)
        specs.append(pl.BlockSpec((rows // row_blocks, cols // col_blocks), index))
    return specs


def _side_cast(src_refs, dst_refs):
    for src, dst in zip(src_refs, dst_refs):
        dst[...] = src[...].astype(dst.dtype)


def _bf16_like(arrays):
    return [jax.ShapeDtypeStruct(a.shape, BF16) for a in arrays]


def _split_w_in_kernel(w_ref, main_ref, kv_ref, *, kv_block):
    j = pl.program_id(0)

    @pl.when(j != kv_block)
    def _():
        main_ref[...] = w_ref[...].astype(BF16)

    @pl.when(j == kv_block)
    def _():
        kv_ref[...] = w_ref[...].astype(BF16)


def _split_w_in(w, kv_start, kv_width):
    d, cols = w.shape
    assert kv_start % kv_width == 0 and cols % kv_width == 0
    kv_block = kv_start // kv_width
    main_col = lambda j: (0, jnp.where(j > kv_block, j - 1, jnp.minimum(j, kv_block - 1)))
    return pl.pallas_call(
        functools.partial(_split_w_in_kernel, kv_block=kv_block),
        grid=(cols // kv_width,),
        in_specs=[pl.BlockSpec((d, kv_width), lambda j: (0, j))],
        out_specs=[pl.BlockSpec((d, kv_width), main_col),
                   pl.BlockSpec((d, kv_width), lambda j: (0, 0))],
        out_shape=[jax.ShapeDtypeStruct((d, cols - kv_width), BF16),
                   jax.ShapeDtypeStruct((d, kv_width), BF16)],
        compiler_params=pltpu.CompilerParams(
            dimension_semantics=("arbitrary",), vmem_limit_bytes=VMEM_LIMIT),
        name="split_w_in",
    )(w)


def _ada_kernel(c_ref, w_ref, b_ref, o_ref):
    c = c_ref[...]
    s = (c * _sigmoid(c)).astype(BF16)
    o_ref[...] = _dot(s, w_ref[...].astype(BF16)) + b_ref[...]


def _ada_modulation(c_pad, w, b):
    rows, d = c_pad.shape
    n = w.shape[1]
    tn = 1024
    return pl.pallas_call(
        _ada_kernel,
        grid=(n // tn,),
        in_specs=[
            pl.BlockSpec((rows, d), lambda j: (0, 0)),
            pl.BlockSpec((d, tn), lambda j: (0, j)),
            pl.BlockSpec((1, tn), lambda j: (0, j)),
        ],
        out_specs=pl.BlockSpec((rows, tn), lambda j: (0, j)),
        out_shape=jax.ShapeDtypeStruct((rows, n), F32),
        compiler_params=pltpu.CompilerParams(
            dimension_semantics=("arbitrary",), vmem_limit_bytes=VMEM_LIMIT),
        name="ada_modulation",
    )(c_pad, w, b.reshape(1, n))


def _ffn_kernel(x_ref, vec_ref, wg_ref, wu_ref, wd_ref, o_ref, h_scr, acc_scr, rinv_scr, *,
                final_norm, tiles_per_seq, norm_row, ada_row):
    f = pl.program_id(1)
    nf = pl.num_programs(1)
    batch = pl.program_id(0) // tiles_per_seq
    vec = lambda r: vec_ref[batch, r:r + 1, :]

    @pl.when(f == 0)
    def _():
        _rms_modulate_to(h_scr, x_ref, rinv_scr, vec(norm_row), vec(ada_row), vec(ada_row + 1))

    def down_projection():
        h = h_scr[...]
        gate = _dot(h, wg_ref[...])
        up = _dot(h, wu_ref[...])
        act = (gate * _sigmoid(gate) * up).astype(BF16)
        return _dot(act, wd_ref[...])

    @pl.when(f == 0)
    def _():
        acc_scr[...] = down_projection()

    @pl.when(f > 0)
    def _():
        acc_scr[...] += down_projection()

    @pl.when(f == nf - 1)
    def _():
        y = x_ref[...] + (0.5 * vec(ada_row + 2)) * acc_scr[...]
        o_ref[...] = y
        if final_norm:
            rinv_scr[...] = _row_rsqrt_mean_square(y)
            final_w = vec(VEC_FINAL_NORM_ROW)

            def slab(rows):
                rinv = _across_lanes(rinv_scr[rows, :], o_ref.shape[1])
                o_ref[rows, :] = o_ref[rows, :] * rinv * final_w

            _for_row_chunks(o_ref.shape[0], slab)


def _ffn(x2d, vec, wg, wu, wd, *, seq, norm_row, ada_row, final_norm):
    m, d = x2d.shape
    dff = wg.shape[1]
    tm, tf = 512, 512
    row = lambda i, f: (i, 0)
    return pl.pallas_call(
        functools.partial(_ffn_kernel, final_norm=final_norm, tiles_per_seq=seq // tm,
                          norm_row=norm_row, ada_row=ada_row),
        grid=(m // tm, dff // tf),
        in_specs=[
            pl.BlockSpec((tm, d), row),
            pl.BlockSpec(vec.shape, lambda i, f: (0, 0, 0)),
            pl.BlockSpec((d, tf), lambda i, f: (0, f)),
            pl.BlockSpec((d, tf), lambda i, f: (0, f)),
            pl.BlockSpec((tf, d), lambda i, f: (f, 0)),
        ],
        out_specs=pl.BlockSpec((tm, d), row),
        out_shape=jax.ShapeDtypeStruct((m, d), F32),
        scratch_shapes=[pltpu.VMEM((tm, d), BF16), pltpu.VMEM((tm, d), F32),
                        pltpu.VMEM((tm, LANES), F32)],
        compiler_params=pltpu.CompilerParams(
            dimension_semantics=("arbitrary", "arbitrary"), vmem_limit_bytes=VMEM_LIMIT),
        name="ffn_final" if final_norm else "ffn",
    )(x2d, vec, wg, wu, wd)


def _rope128(x, cos, sin):
    return x * cos + pltpu.roll(x, 64, 1) * sin


def _rope64(x, cos, sin_lo, sin_hi):
    return x * cos + pltpu.roll(x, 96, 1) * sin_lo + pltpu.roll(x, 32, 1) * sin_hi


def _proj_kernel(x_ref, nw_ref, sh_ref, sc_ref, w_ref, wkv_ref,
                 cos_a_ref, sin_a_ref, cos_b_ref, sin_lo_ref, sin_hi_ref,
                 qa_ref, ka_ref, va_ref, qb_ref, kb_ref, vb_ref, ga_ref, gb_ref, h_scr, rinv_scr):
    n = pl.program_id(1)
    chunk = 2 * LANES
    identity = lambda x: x

    def project(dst_ref, epilogue, col0):
        h = h_scr[...]
        for s in range(0, dst_ref.shape[1], chunk):
            acc = _dot(h, w_ref[:, col0 + s:col0 + s + chunk])
            for t in range(0, chunk, LANES):
                dst_ref[:, s + t:s + t + LANES] = epilogue(acc[:, t:t + LANES]).astype(BF16)

    rope_a = lambda x: _rope128(x, cos_a_ref[...], sin_a_ref[...])
    rope_b = lambda x: _rope64(x, cos_b_ref[...], sin_lo_ref[...], sin_hi_ref[...])

    @pl.when(n == 0)
    def _():
        _rms_modulate_to(h_scr, x_ref, rinv_scr, nw_ref[...], sh_ref[...], sc_ref[...])
        project(qa_ref, rope_a, 0)
        project(ka_ref, rope_a, qa_ref.shape[1])

    @pl.when(n == 1)
    def _():
        project(va_ref, identity, 0)
        project(qb_ref, rope_b, va_ref.shape[1])
        kv = _dot(h_scr[...], wkv_ref[...])
        kb_ref[...] = rope_b(kv[:, :LANES]).astype(BF16)
        vb_ref[...] = kv[:, LANES:].astype(BF16)

    @pl.when(n == 2)
    def _():
        project(ga_ref, identity, 0)

    @pl.when(n == 3)
    def _():
        project(gb_ref, identity, 0)


def _in_projection(x2d, norm_w, shift, scale, w_main, w_kv, tables, *, seq):
    m, d = x2d.shape
    tm, tn, half = 512, 2048, 1024
    n_tiles = w_main.shape[1] // tn
    assert n_tiles == 4 and d == tn
    tiles_per_seq = seq // tm
    per_batch = lambda i, n: (i // tiles_per_seq, 0, 0)
    pos = lambda i, n: (i % tiles_per_seq, 0)
    full = lambda i, n: (i, 0)
    wide = lambda cols: jax.ShapeDtypeStruct((m, cols), BF16)
    table_spec = pl.BlockSpec((tm, LANES), pos)
    return pl.pallas_call(
        _proj_kernel,
        grid=(m // tm, n_tiles),
        in_specs=[
            pl.BlockSpec((tm, d), full),
            pl.BlockSpec((1, d), lambda i, n: (0, 0)),
            pl.BlockSpec((None, 1, d), per_batch),
            pl.BlockSpec((None, 1, d), per_batch),
            pl.BlockSpec((d, tn), lambda i, n: (0, n)),
            pl.BlockSpec((d, 2 * LANES), lambda i, n: (0, 0)),
            table_spec, table_spec, table_spec, table_spec, table_spec,
        ],
        out_specs=[
            pl.BlockSpec((tm, half), full),
            pl.BlockSpec((tm, half), full),
            pl.BlockSpec((tm, half), full),
            pl.BlockSpec((tm, half), full),
            pl.BlockSpec((tm, LANES), full),
            pl.BlockSpec((tm, LANES), full),
            pl.BlockSpec((tm, tn), full),
            pl.BlockSpec((tm, tn), full),
        ],
        out_shape=[wide(half), wide(half), wide(half), wide(half), wide(LANES), wide(LANES),
                   wide(tn), wide(tn)],
        scratch_shapes=[pltpu.VMEM((tm, d), BF16), pltpu.VMEM((tm, LANES), F32)],
        compiler_params=pltpu.CompilerParams(
            dimension_semantics=("arbitrary", "arbitrary"), vmem_limit_bytes=VMEM_LIMIT),
        name="in_projection",
    )(x2d, norm_w.reshape(1, d), shift, scale, w_main, w_kv, *tables)


def _rope_tables(seq):
    pos = jnp.arange(seq, dtype=F32)[:, None]

    def cos_sin(half):
        inv = ROPE_THETA ** (-jnp.arange(half, dtype=F32) / half)
        ang = pos * inv[None, :]
        return jnp.cos(ang), jnp.sin(ang)

    cos, sin = cos_sin(MOBA_HEAD_DIM // 2)
    cos_a = jnp.concatenate([cos, cos], axis=1)
    sin_a = jnp.concatenate([-sin, sin], axis=1)
    cos, sin = cos_sin(SWA_HEAD_DIM // 2)
    zero = jnp.zeros_like(sin)
    cos_b = jnp.concatenate([cos, cos, cos, cos], axis=1)
    sin_lo = jnp.concatenate([-sin, zero, -sin, zero], axis=1)
    sin_hi = jnp.concatenate([zero, sin, zero, sin], axis=1)
    return cos_a, sin_a, cos_b, sin_lo, sin_hi


def _moba_kernel(*refs, topk, group, n_side):
    q_ref, k_ref, v_ref = refs[:3]
    o_ref = refs[3 + n_side]
    kaug_scr, vt_scr, kmean_scr, s_scr = refs[4 + 2 * n_side:]
    _side_cast(refs[3:3 + n_side], refs[4 + n_side:4 + 2 * n_side])

    i = pl.program_id(2)
    blk = q_ref.shape[0]
    heads, nb, dh, _ = vt_scr.shape
    exp2_scale = dh ** -0.5 * LOG2_E
    head_lanes = lambda h: slice(h * dh, (h + 1) * dh)

    @pl.when(i == 0)
    def _():
        col = lax.broadcasted_iota(jnp.int32, (blk, LANES), 1)
        for h in range(heads):
            for j in range(nb):
                k = k_ref[j * blk:(j + 1) * blk, head_lanes(h)]
                kmean_scr[h, j:j + 1, :] = jnp.mean(k.astype(F32), axis=0, keepdims=True)
                kaug_scr[h, j, :, :dh] = k
                kaug_scr[h, j, :, dh:] = jnp.where(col == j, 1.0, 0.0).astype(BF16)
                v = v_ref[j * blk:(j + 1) * blk, head_lanes(h)]
                vt_scr[h, j] = v.astype(F32).T.astype(BF16)

    def gate(h):
        q = q_ref[:, head_lanes(h)]
        gs = _dot_nt(kmean_scr[h].astype(BF16), q)
        blk_id = lax.broadcasted_iota(jnp.int32, gs.shape, 0)
        valid = blk_id < i
        vals = jnp.where(valid, gs, -jnp.inf)
        sel = jnp.zeros(gs.shape, dtype=jnp.bool_)
        for _ in range(topk):
            best = jnp.max(vals, axis=0, keepdims=True)
            first = jnp.min(jnp.where(vals == best, blk_id, nb), axis=0, keepdims=True)
            pick = blk_id == first
            sel = sel | pick
            vals = jnp.where(pick, -jnp.inf, vals)
        sel = sel & valid
        bias_t = jnp.where(sel, 0.0, MASK_VALUE).astype(F32)
        bias_t = jnp.concatenate([bias_t, jnp.zeros((LANES - nb, blk), F32)], axis=0)
        q_t = q.astype(F32).T.astype(BF16)
        return jnp.concatenate([q_t, bias_t.astype(BF16)], axis=0)

    q_aug_t = [gate(h) for h in range(heads)]

    def attend(n_past):
        own_slot = n_past
        r = lax.broadcasted_iota(jnp.int32, (blk, blk), 0)
        c = lax.broadcasted_iota(jnp.int32, (blk, blk), 1)
        m = []
        for h in range(heads):
            s = _dot(kaug_scr[h, i, :, :dh], q_aug_t[h][:dh]) * exp2_scale
            s = jnp.where(r <= c, s, MASK_VALUE)
            s_scr[h, own_slot] = s
            mh = jnp.max(s, axis=0, keepdims=True)
            for j in range(n_past):
                s = _dot(kaug_scr[h, j], q_aug_t[h]) * exp2_scale
                s_scr[h, j] = s
                mh = jnp.maximum(mh, jnp.max(s, axis=0, keepdims=True))
            m.append(mh)

        for h in range(heads):
            l = jnp.zeros_like(m[h])
            acc = jnp.zeros((dh, blk), F32)
            for slot in range(n_past + 1):
                p = jnp.exp2(s_scr[h, slot] - m[h])
                l = l + jnp.sum(p, axis=0, keepdims=True)
                vt = vt_scr[h, i] if slot == own_slot else vt_scr[h, slot]
                acc = acc + _dot(vt, p.astype(BF16))
            o_ref[:, head_lanes(h)] = (acc / l).T.astype(o_ref.dtype)

    n_groups = (i + group - 1) // group
    for n in range(nb // group + 1):
        pl.when(n_groups == n)(functools.partial(attend, n * group))


def _moba_attention(q, k, v, side, *, heads):
    b, seq, _ = q.shape
    dh, blk = MOBA_HEAD_DIM, MOBA_BLOCK
    nb = seq // blk
    hps = MOBA_HEADS_PER_STEP
    grid = (b, heads // hps, nb)
    q_spec = pl.BlockSpec((None, blk, hps * dh), lambda bi, h, i: (bi, i, h))
    kv_spec = pl.BlockSpec((None, seq, hps * dh), lambda bi, h, i: (bi, 0, h))
    side_specs = _side_cast_specs(side, grid[0] * grid[1] * grid[2],
                                  lambda bi, h, i: (bi * grid[1] + h) * grid[2] + i)
    out, *side_bf16 = pl.pallas_call(
        functools.partial(_moba_kernel, topk=min(MOBA_TOPK, nb), group=MOBA_LOOP_GROUP,
                          n_side=len(side)),
        grid=grid,
        in_specs=[q_spec, kv_spec, kv_spec] + side_specs,
        out_specs=[q_spec] + side_specs,
        out_shape=[jax.ShapeDtypeStruct(q.shape, BF16)] + _bf16_like(side),
        scratch_shapes=[pltpu.VMEM((hps, nb, blk, dh + LANES), BF16),
                        pltpu.VMEM((hps, nb, dh, blk), BF16),
                        pltpu.VMEM((hps, nb, dh), F32),
                        pltpu.VMEM((hps, nb + 1, blk, blk), F32)],
        compiler_params=pltpu.CompilerParams(
            dimension_semantics=("arbitrary", "arbitrary", "arbitrary"),
            vmem_limit_bytes=VMEM_LIMIT),
        name="moba_attention",
    )(q, k, v, *side)
    return out, side_bf16


def _swa_kernel(*refs, q_lane_blocks, n_side):
    sink_ref, q_ref, kp_ref, kc_ref, vp_ref, vc_ref = refs[:6]
    o_ref = refs[6 + n_side]
    k_scr, vt_scr, s_scr, qt_scr, ot_scr = refs[7 + 2 * n_side:]
    _side_cast(refs[6:6 + n_side], refs[7 + n_side:7 + 2 * n_side])

    i = pl.program_id(1)
    w = kp_ref.shape[0]
    tq = q_ref.shape[0]
    sub_blocks = tq // w
    half = LANES // 2
    scale = half ** -0.5
    groups = q_lane_blocks // SWA_KV_HEADS
    lanes = lambda jb: slice(jb * LANES, (jb + 1) * LANES)
    zeros = jnp.zeros((half, 2 * w), BF16)

    k_scr[:w, :] = kp_ref[...]
    k_scr[w:, :] = kc_ref[...]
    transposed = lambda x: x.astype(F32).T.astype(BF16)
    vt_scr[0] = transposed(vp_ref[...])
    for c in range(sub_blocks):
        vt_scr[c + 1] = transposed(vc_ref[c * w:(c + 1) * w, :])
    for u in range(sub_blocks):
        for jb in range(q_lane_blocks):
            qt = (q_ref[u * w:(u + 1) * w, lanes(jb)].astype(F32) * scale).T.astype(BF16)
            q2 = jnp.concatenate([qt[:half], qt[half:]], axis=1)
            qt_scr[u * q_lane_blocks + jb] = jnp.concatenate(
                [q2, zeros] if jb < groups else [zeros, q2], axis=0)

    key = lax.broadcasted_iota(jnp.int32, (w, 2 * w), 0)
    col = lax.broadcasted_iota(jnp.int32, (w, 2 * w), 1)
    from_prev = key > jnp.where(col >= w, col - w, col)
    odd_head = lax.broadcasted_iota(jnp.int32, (1, 2 * w), 1) >= w

    def sub_block(u):
        prev_bias = jnp.where((i * sub_blocks + u) == 0, MASK_VALUE, 0.0)
        k2 = k_scr[u * w:(u + 2) * w, :]
        vt2 = jnp.concatenate([vt_scr[u], vt_scr[u + 1]], axis=1)

        sinks, maxima = [], []
        for jb in range(q_lane_blocks):
            s = _dot(k2, qt_scr[u * q_lane_blocks + jb])
            s = jnp.where(from_prev, s[:w] + prev_bias, s[w:])
            s_scr[u * q_lane_blocks + jb] = s
            sink = jnp.where(odd_head, sink_ref[2 * jb + 1], sink_ref[2 * jb])
            sinks.append(sink)
            maxima.append(jnp.maximum(jnp.max(s, axis=0, keepdims=True), sink))

        for jb in range(q_lane_blocks):
            kvh = jb // groups
            p = jnp.exp(s_scr[u * q_lane_blocks + jb] - maxima[jb])
            den = jnp.sum(p, axis=0, keepdims=True) + jnp.exp(sinks[jb] - maxima[jb])
            p2 = jnp.concatenate([jnp.where(from_prev, p, 0.0), jnp.where(from_prev, 0.0, p)],
                                 axis=0).astype(BF16)
            o = _dot(vt2[kvh * half:(kvh + 1) * half], p2) / den
            ot_scr[u * q_lane_blocks + jb] = jnp.concatenate([o[:, :w], o[:, w:]], axis=0)

    for u in range(sub_blocks):
        sub_block(u)

    for u in range(sub_blocks):
        for jb in range(q_lane_blocks):
            o_ref[u * w:(u + 1) * w, lanes(jb)] = ot_scr[u * q_lane_blocks + jb].T.astype(
                o_ref.dtype)


def _swa_attention(q, k, v, sinks, side):
    b, seq, qw = q.shape
    w = SWA_WINDOW
    tq = 512
    sub_blocks = tq // w
    grid = (b, seq // tq)
    q_spec = pl.BlockSpec((None, tq, qw), lambda bi, i: (bi, i, 0))
    cur = pl.BlockSpec((None, tq, LANES), lambda bi, i: (bi, i, 0))
    prev = pl.BlockSpec((None, w, LANES), lambda bi, i: (bi, jnp.maximum(i * sub_blocks - 1, 0), 0))
    side_specs = _side_cast_specs(side, grid[0] * grid[1], lambda bi, i: bi * grid[1] + i)
    out, *side_bf16 = pl.pallas_call(
        functools.partial(_swa_kernel, q_lane_blocks=qw // LANES, n_side=len(side)),
        grid=grid,
        in_specs=[pl.BlockSpec(memory_space=pltpu.SMEM), q_spec, prev, cur, prev, cur] + side_specs,
        out_specs=[q_spec] + side_specs,
        out_shape=[jax.ShapeDtypeStruct(q.shape, BF16)] + _bf16_like(side),
        scratch_shapes=[pltpu.VMEM((tq + w, LANES), BF16),
                        pltpu.VMEM((sub_blocks + 1, LANES, w), BF16),
                        pltpu.VMEM((sub_blocks * (qw // LANES), w, 2 * w), F32),
                        pltpu.VMEM((sub_blocks * (qw // LANES), LANES, 2 * w), BF16),
                        pltpu.VMEM((sub_blocks * (qw // LANES), LANES, w), F32)],
        compiler_params=pltpu.CompilerParams(
            dimension_semantics=("arbitrary", "arbitrary"), vmem_limit_bytes=VMEM_LIMIT),
        name="swa_attention",
    )(sinks, q, k, k, v, v, *side)
    return out, side_bf16


def _merge_kernel(ya_ref, yb_ref, ga_ref, gb_ref, x_ref, g_ref, wa_ref, wb_ref, wo_ref, o_ref):
    merged = (_sigmoid(ga_ref[...].astype(F32)) * _dot(ya_ref[...], wa_ref[...])
              + _sigmoid(gb_ref[...].astype(F32)) * _dot(yb_ref[...], wb_ref[...]))
    o_ref[...] = x_ref[...] + g_ref[...] * _dot(merged.astype(BF16), wo_ref[...])


def _merge(ya, yb, ga, gb, x2d, gate, wa, wb, wo, *, seq):
    m, d = x2d.shape
    tm = 512
    tiles_per_seq = seq // tm
    row = lambda i: (i, 0)
    const = lambda i: (0, 0)
    resident = lambda shape: pl.BlockSpec(shape, const, pipeline_mode=pl.Buffered(1))
    return pl.pallas_call(
        _merge_kernel,
        grid=(m // tm,),
        in_specs=[
            pl.BlockSpec((tm, ya.shape[1]), row),
            pl.BlockSpec((tm, yb.shape[1]), row),
            pl.BlockSpec((tm, d), row),
            pl.BlockSpec((tm, d), row),
            pl.BlockSpec((tm, d), row),
            pl.BlockSpec((None, 1, d), lambda i: (i // tiles_per_seq, 0, 0)),
            resident(wa.shape), resident(wb.shape), resident(wo.shape),
        ],
        out_specs=pl.BlockSpec((tm, d), row),
        out_shape=jax.ShapeDtypeStruct((m, d), F32),
        compiler_params=pltpu.CompilerParams(
            dimension_semantics=("arbitrary",), vmem_limit_bytes=VMEM_LIMIT),
        name="merge_out_projection",
    )(ya, yb, ga, gb, x2d, gate, wa, wb, wo)


def kernel(x, c, w_ada, b_ada, norm_ffn1, ffn1_gate, ffn1_up, ffn1_down, norm_mix, w_in, swa_sinks,
           w_branch_moba, w_branch_swa, w_out, norm_ffn2, ffn2_gate, ffn2_up, ffn2_down, norm_final):
    b, seq, d = x.shape
    depth = w_ada.shape[0]
    moba_w = w_branch_moba.shape[1]
    swa_qw = w_branch_swa.shape[1]
    kv_w = SWA_KV_HEADS * SWA_HEAD_DIM
    qkv_cols = 3 * moba_w + swa_qw
    assert w_in.shape[2] == qkv_cols + 2 * kv_w + 2 * d
    assert moba_w == swa_qw == 1024 and d == 2048 and kv_w == LANES

    tables = _rope_tables(seq)
    c_pad = jnp.pad(c, ((0, 8 - b), (0, 0)))
    x2d = x.reshape(b * seq, d)

    for l in range(depth):
        mod = _ada_modulation(c_pad, w_ada[l], b_ada[l])[:b].reshape(b, N_ADA, d)
        sh2, sc2, g2 = [mod[:, t:t + 1] for t in (3, 4, 5)]
        gains = jnp.stack([norm_ffn1[l], norm_mix[l], norm_ffn2[l], norm_final])
        vec = jnp.concatenate(
            [mod, jnp.broadcast_to(gains, (b,) + gains.shape),
             jnp.zeros((b, VEC_ROWS - N_ADA - gains.shape[0], d), F32)], axis=1)

        gate1, up1 = _cast_bf16([ffn1_gate[l], ffn1_up[l]], block_rows=256)
        down1, = _cast_bf16([ffn1_down[l]], block_rows=512)
        w_main, w_kv = _split_w_in(w_in[l], qkv_cols, 2 * kv_w)

        x2d = _ffn(x2d, vec, gate1, up1, down1, seq=seq, norm_row=VEC_NORM_FFN1_ROW, ada_row=0,
                   final_norm=False)

        qa, ka, va, qb, kb, vb, ga, gb = _in_projection(
            x2d, norm_mix[l], sh2, sc2, w_main, w_kv, tables, seq=seq)

        rs = lambda t: t.reshape(b, seq, t.shape[-1])
        ya, (gate2, up2, down2) = _moba_attention(
            rs(qa), rs(ka), rs(va), [ffn2_gate[l], ffn2_up[l], ffn2_down[l]],
            heads=moba_w // MOBA_HEAD_DIM)
        yb, (wa, wb, wo) = _swa_attention(
            rs(qb), rs(kb), rs(vb), swa_sinks[l], [w_branch_moba[l], w_branch_swa[l], w_out[l]])

        x2d = _merge(ya.reshape(b * seq, moba_w), yb.reshape(b * seq, swa_qw), ga, gb, x2d, g2,
                     wa, wb, wo, seq=seq)

        x2d = _ffn(x2d, vec, gate2, up2, down2, seq=seq, norm_row=VEC_NORM_FFN2_ROW, ada_row=6,
                   final_norm=(l == depth - 1))

    return x2d.reshape(b, seq, d)
```

```python
import functools

import jax
import jax.numpy as jnp
from jax import lax
from jax.experimental import pallas as pl
from jax.experimental.pallas import tpu as pltpu

MOBA_HEAD_DIM = 128
MOBA_BLOCK = 256
MOBA_TOPK = 3
SWA_HEAD_DIM = 64
SWA_KV_HEADS = 2
SWA_WINDOW = 128
ROPE_THETA = 10000.0
EPS = 1e-6
N_ADA = 9
VEC_NORM_FFN1_ROW, VEC_NORM_MIX_ROW, VEC_NORM_FFN2_ROW, VEC_FINAL_NORM_ROW = 9, 10, 11, 12
VEC_ROWS = 16

LANES = 128
BF16_SUBLANES = 16
ROW_CHUNK = 16
VMEM_LIMIT = 56 * 1024 * 1024
MASK_VALUE = -1e30
LOG2_E = 1.4426950408889634
MOBA_LOOP_GROUP = 2
MOBA_HEADS_PER_STEP = 4

F32 = jnp.float32
BF16 = jnp.bfloat16


def _sigmoid(x):
    return 1.0 / (1.0 + jnp.exp(-x))


def _row_rsqrt_mean_square(x):
    r = lax.rsqrt(jnp.mean(x * x, axis=-1, keepdims=True) + EPS)
    return jnp.broadcast_to(r, (x.shape[0], LANES))


def _across_lanes(r, width):
    return jnp.concatenate([r] * (width // LANES), axis=1)


def _for_row_chunks(rows, body):
    for c in range(rows // ROW_CHUNK):
        body(pl.ds(c * ROW_CHUNK, ROW_CHUNK))


def _rms_modulate_to(h_ref, x_ref, rinv_scr, norm_w, shift, scale):
    rinv_scr[...] = _row_rsqrt_mean_square(x_ref[...])
    gain = norm_w * (1.0 + scale)

    def slab(rows):
        rinv = _across_lanes(rinv_scr[rows, :], x_ref.shape[1])
        h_ref[rows, :] = (x_ref[rows, :] * rinv * gain + shift).astype(BF16)

    _for_row_chunks(x_ref.shape[0], slab)


def _dot(a, b):
    return jnp.dot(a, b, preferred_element_type=F32)


def _dot_nt(a, b):
    return lax.dot_general(a, b, (((1,), (1,)), ((), ())), preferred_element_type=F32)


def _cast_kernel(*refs):
    n = len(refs) // 2
    for src, dst in zip(refs[:n], refs[n:]):
        dst[...] = src[...].astype(dst.dtype)


def _cast_bf16(arrays, block_rows):
    rows, cols = arrays[0].shape
    spec = pl.BlockSpec((block_rows, cols), lambda r: (r, 0))
    n = len(arrays)
    return pl.pallas_call(
        _cast_kernel,
        grid=(rows // block_rows,),
        in_specs=[spec] * n,
        out_specs=[spec] * n,
        out_shape=[jax.ShapeDtypeStruct((rows, cols), BF16)] * n,
        compiler_params=pltpu.CompilerParams(
            dimension_semantics=("arbitrary",), vmem_limit_bytes=VMEM_LIMIT),
        name="cast_bf16",
    )(*arrays)


def _side_cast_specs(arrays, steps, step_index):
    specs = []
    for a in arrays:
        rows, cols = a.shape
        for col_blocks in (1, 2, 4, 8, 16):
            row_blocks = steps // col_blocks
            if rows % (row_blocks * BF16_SUBLANES) == 0 and cols % (col_blocks * LANES) == 0:
                break
        else:
            raise ValueError(f"no {steps}-block tiling for {a.shape}")
        index = lambda *g, cb=col_blocks: (step_index(*g) // cb, step_index(*g) % cb)
        specs.append(pl.BlockSpec((rows // row_blocks, cols // col_blocks), index))
    return specs


def _side_cast(src_refs, dst_refs):
    for src, dst in zip(src_refs, dst_refs):
        dst[...] = src[...].astype(dst.dtype)


def _bf16_like(arrays):
    return [jax.ShapeDtypeStruct(a.shape, BF16) for a in arrays]


def _split_w_in_kernel(w_ref, main_ref, kv_ref, *, kv_block):
    j = pl.program_id(0)

    @pl.when(j != kv_block)
    def _():
        main_ref[...] = w_ref[...].astype(BF16)

    @pl.when(j == kv_block)
    def _():
        kv_ref[...] = w_ref[...].astype(BF16)


def _split_w_in(w, kv_start, kv_width):
    d, cols = w.shape
    assert kv_start % kv_width == 0 and cols % kv_width == 0
    kv_block = kv_start // kv_width
    main_col = lambda j: (0, jnp.where(j > kv_block, j - 1, jnp.minimum(j, kv_block - 1)))
    return pl.pallas_call(
        functools.partial(_split_w_in_kernel, kv_block=kv_block),
        grid=(cols // kv_width,),
        in_specs=[pl.BlockSpec((d, kv_width), lambda j: (0, j))],
        out_specs=[pl.BlockSpec((d, kv_width), main_col),
                   pl.BlockSpec((d, kv_width), lambda j: (0, 0))],
        out_shape=[jax.ShapeDtypeStruct((d, cols - kv_width), BF16),
                   jax.ShapeDtypeStruct((d, kv_width), BF16)],
        compiler_params=pltpu.CompilerParams(
            dimension_semantics=("arbitrary",), vmem_limit_bytes=VMEM_LIMIT),
        name="split_w_in",
    )(w)


def _ada_kernel(c_ref, w_ref, b_ref, o_ref):
    c = c_ref[...]
    s = (c * _sigmoid(c)).astype(BF16)
    o_ref[...] = _dot(s, w_ref[...].astype(BF16)) + b_ref[...]


def _ada_modulation(c_pad, w, b):
    rows, d = c_pad.shape
    n = w.shape[1]
    tn = 1024
    return pl.pallas_call(
        _ada_kernel,
        grid=(n // tn,),
        in_specs=[
            pl.BlockSpec((rows, d), lambda j: (0, 0)),
            pl.BlockSpec((d, tn), lambda j: (0, j)),
            pl.BlockSpec((1, tn), lambda j: (0, j)),
        ],
        out_specs=pl.BlockSpec((rows, tn), lambda j: (0, j)),
        out_shape=jax.ShapeDtypeStruct((rows, n), F32),
        compiler_params=pltpu.CompilerParams(
            dimension_semantics=("arbitrary",), vmem_limit_bytes=VMEM_LIMIT),
        name="ada_modulation",
    )(c_pad, w, b.reshape(1, n))


def _ffn_kernel(x_ref, vec_ref, wg_ref, wu_ref, wd_ref, o_ref, h_scr, acc_scr, rinv_scr, *,
                final_norm, tiles_per_seq, norm_row, ada_row):
    f = pl.program_id(1)
    nf = pl.num_programs(1)
    batch = pl.program_id(0) // tiles_per_seq
    vec = lambda r: vec_ref[batch, r:r + 1, :]

    def down_projection():
        h = h_scr[...]
        gate = _dot(h, wg_ref[...])
        up = _dot(h, wu_ref[...])
        act = (gate * _sigmoid(gate) * up).astype(BF16)
        return _dot(act, wd_ref[...])

    @pl.when(f == 0)
    def _():
        _rms_modulate_to(h_scr, x_ref, rinv_scr, vec(norm_row), vec(ada_row), vec(ada_row + 1))
        acc_scr[...] = down_projection()

    @pl.when((f > 0) & (f < nf - 1))
    def _():
        acc_scr[...] += down_projection()

    @pl.when(f == nf - 1)
    def _():
        y = x_ref[...] + (0.5 * vec(ada_row + 2)) * (acc_scr[...] + down_projection())
        o_ref[...] = y
        if final_norm:
            rinv_scr[...] = _row_rsqrt_mean_square(y)
            final_w = vec(VEC_FINAL_NORM_ROW)

            def slab(rows):
                rinv = _across_lanes(rinv_scr[rows, :], o_ref.shape[1])
                o_ref[rows, :] = o_ref[rows, :] * rinv * final_w

            _for_row_chunks(o_ref.shape[0], slab)


def _ffn(x2d, vec, wg, wu, wd, *, seq, norm_row, ada_row, final_norm):
    m, d = x2d.shape
    dff = wg.shape[1]
    tm, tf = 512, 512
    assert dff // tf >= 2
    row = lambda i, f: (i, 0)
    return pl.pallas_call(
        functools.partial(_ffn_kernel, final_norm=final_norm, tiles_per_seq=seq // tm,
                          norm_row=norm_row, ada_row=ada_row),
        grid=(m // tm, dff // tf),
        in_specs=[
            pl.BlockSpec((tm, d), row),
            pl.BlockSpec(vec.shape, lambda i, f: (0, 0, 0)),
            pl.BlockSpec((d, tf), lambda i, f: (0, f)),
            pl.BlockSpec((d, tf), lambda i, f: (0, f)),
            pl.BlockSpec((tf, d), lambda i, f: (f, 0)),
        ],
        out_specs=pl.BlockSpec((tm, d), row),
        out_shape=jax.ShapeDtypeStruct((m, d), F32),
        scratch_shapes=[pltpu.VMEM((tm, d), BF16), pltpu.VMEM((tm, d), F32),
                        pltpu.VMEM((tm, LANES), F32)],
        compiler_params=pltpu.CompilerParams(
            dimension_semantics=("arbitrary", "arbitrary"), vmem_limit_bytes=VMEM_LIMIT),
        name="ffn_final" if final_norm else "ffn",
    )(x2d, vec, wg, wu, wd)


def _rope128(x, cos, sin):
    return x * cos + pltpu.roll(x, 64, 1) * sin


def _rope64(x, cos, sin_lo, sin_hi):
    return x * cos + pltpu.roll(x, 96, 1) * sin_lo + pltpu.roll(x, 32, 1) * sin_hi


def _proj_kernel(x_ref, nw_ref, sh_ref, sc_ref, w_ref, wkv_ref,
                 cos_a_ref, sin_a_ref, cos_b_ref, sin_lo_ref, sin_hi_ref,
                 qa_ref, ka_ref, va_ref, qb_ref, kb_ref, vb_ref, ga_ref, gb_ref, h_scr, rinv_scr):
    n = pl.program_id(1)
    chunk = 2 * LANES
    identity = lambda x: x

    def project(dst_ref, epilogue, col0):
        h = h_scr[...]
        for s in range(0, dst_ref.shape[1], chunk):
            acc = _dot(h, w_ref[:, col0 + s:col0 + s + chunk])
            for t in range(0, chunk, LANES):
                dst_ref[:, s + t:s + t + LANES] = epilogue(acc[:, t:t + LANES]).astype(BF16)

    rope_a = lambda x: _rope128(x, cos_a_ref[...], sin_a_ref[...])
    rope_b = lambda x: _rope64(x, cos_b_ref[...], sin_lo_ref[...], sin_hi_ref[...])

    @pl.when(n == 0)
    def _():
        _rms_modulate_to(h_scr, x_ref, rinv_scr, nw_ref[...], sh_ref[...], sc_ref[...])
        project(qa_ref, rope_a, 0)
        project(ka_ref, rope_a, qa_ref.shape[1])

    @pl.when(n == 1)
    def _():
        project(va_ref, identity, 0)
        project(qb_ref, rope_b, va_ref.shape[1])
        kv = _dot(h_scr[...], wkv_ref[...])
        kb_ref[...] = rope_b(kv[:, :LANES]).astype(BF16)
        vb_ref[...] = kv[:, LANES:].astype(BF16)

    @pl.when(n == 2)
    def _():
        project(ga_ref, identity, 0)

    @pl.when(n == 3)
    def _():
        project(gb_ref, identity, 0)


def _in_projection(x2d, norm_w, shift, scale, w_main, w_kv, tables, *, seq):
    m, d = x2d.shape
    tm, tn, half = 512, 2048, 1024
    n_tiles = w_main.shape[1] // tn
    assert n_tiles == 4 and d == tn
    tiles_per_seq = seq // tm
    per_batch = lambda i, n: (i // tiles_per_seq, 0, 0)
    pos = lambda i, n: (i % tiles_per_seq, 0)
    full = lambda i, n: (i, 0)
    wide = lambda cols: jax.ShapeDtypeStruct((m, cols), BF16)
    table_spec = pl.BlockSpec((tm, LANES), pos)
    return pl.pallas_call(
        _proj_kernel,
        grid=(m // tm, n_tiles),
        in_specs=[
            pl.BlockSpec((tm, d), full),
            pl.BlockSpec((1, d), lambda i, n: (0, 0)),
            pl.BlockSpec((None, 1, d), per_batch),
            pl.BlockSpec((None, 1, d), per_batch),
            pl.BlockSpec((d, tn), lambda i, n: (0, n)),
            pl.BlockSpec((d, 2 * LANES), lambda i, n: (0, 0)),
            table_spec, table_spec, table_spec, table_spec, table_spec,
        ],
        out_specs=[
            pl.BlockSpec((tm, half), full),
            pl.BlockSpec((tm, half), full),
            pl.BlockSpec((tm, half), full),
            pl.BlockSpec((tm, half), full),
            pl.BlockSpec((tm, LANES), full),
            pl.BlockSpec((tm, LANES), full),
            pl.BlockSpec((tm, tn), full),
            pl.BlockSpec((tm, tn), full),
        ],
        out_shape=[wide(half), wide(half), wide(half), wide(half), wide(LANES), wide(LANES),
                   wide(tn), wide(tn)],
        scratch_shapes=[pltpu.VMEM((tm, d), BF16), pltpu.VMEM((tm, LANES), F32)],
        compiler_params=pltpu.CompilerParams(
            dimension_semantics=("arbitrary", "arbitrary"), vmem_limit_bytes=VMEM_LIMIT),
        name="in_projection",
    )(x2d, norm_w.reshape(1, d), shift, scale, w_main, w_kv, *tables)


def _rope_tables(seq):
    pos = jnp.arange(seq, dtype=F32)[:, None]

    def cos_sin(half):
        inv = ROPE_THETA ** (-jnp.arange(half, dtype=F32) / half)
        ang = pos * inv[None, :]
        return jnp.cos(ang), jnp.sin(ang)

    cos, sin = cos_sin(MOBA_HEAD_DIM // 2)
    cos_a = jnp.concatenate([cos, cos], axis=1)
    sin_a = jnp.concatenate([-sin, sin], axis=1)
    cos, sin = cos_sin(SWA_HEAD_DIM // 2)
    zero = jnp.zeros_like(sin)
    cos_b = jnp.concatenate([cos, cos, cos, cos], axis=1)
    sin_lo = jnp.concatenate([-sin, zero, -sin, zero], axis=1)
    sin_hi = jnp.concatenate([zero, sin, zero, sin], axis=1)
    return cos_a, sin_a, cos_b, sin_lo, sin_hi


def _moba_kernel(*refs, topk, group, n_side):
    q_ref, k_ref, v_ref = refs[:3]
    o_ref = refs[3 + n_side]
    kaug_scr, vt_scr, kmean_scr, s_scr = refs[4 + 2 * n_side:]
    _side_cast(refs[3:3 + n_side], refs[4 + n_side:4 + 2 * n_side])

    i = pl.program_id(2)
    blk = q_ref.shape[0]
    heads, nb, dh, _ = vt_scr.shape
    exp2_scale = dh ** -0.5 * LOG2_E
    head_lanes = lambda h: slice(h * dh, (h + 1) * dh)

    @pl.when(i == 0)
    def _():
        col = lax.broadcasted_iota(jnp.int32, (blk, LANES), 1)
        for h in range(heads):
            for j in range(nb):
                k = k_ref[j * blk:(j + 1) * blk, head_lanes(h)]
                kmean_scr[h, j:j + 1, :] = jnp.mean(k.astype(F32), axis=0, keepdims=True)
                kaug_scr[h, j, :, :dh] = k
                kaug_scr[h, j, :, dh:] = jnp.where(col == j, 1.0, 0.0).astype(BF16)
                v = v_ref[j * blk:(j + 1) * blk, head_lanes(h)]
                vt_scr[h, j] = v.astype(F32).T.astype(BF16)

    def gate(h):
        q = q_ref[:, head_lanes(h)]
        gs = _dot_nt(kmean_scr[h].astype(BF16), q)
        blk_id = lax.broadcasted_iota(jnp.int32, gs.shape, 0)
        valid = blk_id < i
        vals = jnp.where(valid, gs, -jnp.inf)
        sel = jnp.zeros(gs.shape, dtype=jnp.bool_)
        for _ in range(topk):
            best = jnp.max(vals, axis=0, keepdims=True)
            first = jnp.min(jnp.where(vals == best, blk_id, nb), axis=0, keepdims=True)
            pick = blk_id == first
            sel = sel | pick
            vals = jnp.where(pick, -jnp.inf, vals)
        sel = sel & valid
        bias_t = jnp.where(sel, 0.0, MASK_VALUE).astype(F32)
        bias_t = jnp.concatenate([bias_t, jnp.zeros((LANES - nb, blk), F32)], axis=0)
        q_t = q.astype(F32).T.astype(BF16)
        return jnp.concatenate([q_t, bias_t.astype(BF16)], axis=0)

    def attend(n_past):
        q_aug_t = [gate(h) for h in range(heads)]
        own_slot = n_past
        r = lax.broadcasted_iota(jnp.int32, (blk, blk), 0)
        c = lax.broadcasted_iota(jnp.int32, (blk, blk), 1)
        m = []
        for h in range(heads):
            s = _dot(kaug_scr[h, i, :, :dh], q_aug_t[h][:dh]) * exp2_scale
            s = jnp.where(r <= c, s, MASK_VALUE)
            s_scr[h, own_slot] = s
            mh = jnp.max(s, axis=0, keepdims=True)
            for j in range(n_past):
                s = _dot(kaug_scr[h, j], q_aug_t[h]) * exp2_scale
                s_scr[h, j] = s
                mh = jnp.maximum(mh, jnp.max(s, axis=0, keepdims=True))
            m.append(mh)

        for h in range(heads):
            l = jnp.zeros_like(m[h])
            acc = jnp.zeros((dh, blk), F32)
            for slot in range(n_past + 1):
                p = jnp.exp2(s_scr[h, slot] - m[h])
                l = l + jnp.sum(p, axis=0, keepdims=True)
                vt = vt_scr[h, i] if slot == own_slot else vt_scr[h, slot]
                acc = acc + _dot(vt, p.astype(BF16))
            o_ref[:, head_lanes(h)] = (acc / l).T.astype(o_ref.dtype)

    n_groups = (i + group - 1) // group
    for n in range(nb // group + 1):
        pl.when(n_groups == n)(functools.partial(attend, n * group))


def _moba_attention(q, k, v, side, *, heads):
    b, seq, _ = q.shape
    dh, blk = MOBA_HEAD_DIM, MOBA_BLOCK
    nb = seq // blk
    hps = MOBA_HEADS_PER_STEP
    grid = (b, heads // hps, nb)
    q_spec = pl.BlockSpec((None, blk, hps * dh), lambda bi, h, i: (bi, i, h))
    kv_spec = pl.BlockSpec((None, seq, hps * dh), lambda bi, h, i: (bi, 0, h))
    side_specs = _side_cast_specs(side, grid[0] * grid[1] * grid[2],
                                  lambda bi, h, i: (bi * grid[1] + h) * grid[2] + i)
    out, *side_bf16 = pl.pallas_call(
        functools.partial(_moba_kernel, topk=min(MOBA_TOPK, nb), group=MOBA_LOOP_GROUP,
                          n_side=len(side)),
        grid=grid,
        in_specs=[q_spec, kv_spec, kv_spec] + side_specs,
        out_specs=[q_spec] + side_specs,
        out_shape=[jax.ShapeDtypeStruct(q.shape, BF16)] + _bf16_like(side),
        scratch_shapes=[pltpu.VMEM((hps, nb, blk, dh + LANES), BF16),
                        pltpu.VMEM((hps, nb, dh, blk), BF16),
                        pltpu.VMEM((hps, nb, dh), F32),
                        pltpu.VMEM((hps, nb + 1, blk, blk), F32)],
        compiler_params=pltpu.CompilerParams(
            dimension_semantics=("arbitrary", "arbitrary", "arbitrary"),
            vmem_limit_bytes=VMEM_LIMIT),
        name="moba_attention",
    )(q, k, v, *side)
    return out, side_bf16


def _swa_kernel(*refs, q_lane_blocks, n_side):
    sink_ref, q_ref, kp_ref, kc_ref, vp_ref, vc_ref = refs[:6]
    o_ref = refs[6 + n_side]
    k_scr, vt_scr, s_scr, qt_scr, ot_scr = refs[7 + 2 * n_side:]
    _side_cast(refs[6:6 + n_side], refs[7 + n_side:7 + 2 * n_side])

    i = pl.program_id(1)
    w = kp_ref.shape[0]
    tq = q_ref.shape[0]
    sub_blocks = tq // w
    half = LANES // 2
    scale = half ** -0.5
    groups = q_lane_blocks // SWA_KV_HEADS
    lanes = lambda jb: slice(jb * LANES, (jb + 1) * LANES)
    zeros = jnp.zeros((half, 2 * w), BF16)

    k_scr[:w, :] = kp_ref[...]
    k_scr[w:, :] = kc_ref[...]
    transposed = lambda x: x.astype(F32).T.astype(BF16)
    vt_scr[0] = transposed(vp_ref[...])
    for c in range(sub_blocks):
        vt_scr[c + 1] = transposed(vc_ref[c * w:(c + 1) * w, :])
    for u in range(sub_blocks):
        for jb in range(q_lane_blocks):
            qt = (q_ref[u * w:(u + 1) * w, lanes(jb)].astype(F32) * scale).T.astype(BF16)
            q2 = jnp.concatenate([qt[:half], qt[half:]], axis=1)
            qt_scr[u * q_lane_blocks + jb] = jnp.concatenate(
                [q2, zeros] if jb < groups else [zeros, q2], axis=0)

    key = lax.broadcasted_iota(jnp.int32, (w, 2 * w), 0)
    col = lax.broadcasted_iota(jnp.int32, (w, 2 * w), 1)
    from_prev = key > jnp.where(col >= w, col - w, col)
    odd_head = lax.broadcasted_iota(jnp.int32, (1, 2 * w), 1) >= w

    def sub_block(u):
        prev_bias = jnp.where((i * sub_blocks + u) == 0, MASK_VALUE, 0.0)
        k2 = k_scr[u * w:(u + 2) * w, :]
        vt2 = jnp.concatenate([vt_scr[u], vt_scr[u + 1]], axis=1)

        sinks, maxima = [], []
        for jb in range(q_lane_blocks):
            s = _dot(k2, qt_scr[u * q_lane_blocks + jb])
            s = jnp.where(from_prev, s[:w] + prev_bias, s[w:])
            s_scr[u * q_lane_blocks + jb] = s
            sink = jnp.where(odd_head, sink_ref[2 * jb + 1], sink_ref[2 * jb])
            sinks.append(sink)
            maxima.append(jnp.maximum(jnp.max(s, axis=0, keepdims=True), sink))

        for jb in range(q_lane_blocks):
            kvh = jb // groups
            p = jnp.exp(s_scr[u * q_lane_blocks + jb] - maxima[jb])
            den = jnp.sum(p, axis=0, keepdims=True) + jnp.exp(sinks[jb] - maxima[jb])
            p2 = jnp.concatenate([jnp.where(from_prev, p, 0.0), jnp.where(from_prev, 0.0, p)],
                                 axis=0).astype(BF16)
            o = _dot(vt2[kvh * half:(kvh + 1) * half], p2) / den
            ot_scr[u * q_lane_blocks + jb] = jnp.concatenate([o[:, :w], o[:, w:]], axis=0)

    for u in range(sub_blocks):
        sub_block(u)

    for u in range(sub_blocks):
        for jb in range(q_lane_blocks):
            o_ref[u * w:(u + 1) * w, lanes(jb)] = ot_scr[u * q_lane_blocks + jb].T.astype(
                o_ref.dtype)


def _swa_attention(q, k, v, sinks, side):
    b, seq, qw = q.shape
    w = SWA_WINDOW
    tq = 512
    sub_blocks = tq // w
    grid = (b, seq // tq)
    q_spec = pl.BlockSpec((None, tq, qw), lambda bi, i: (bi, i, 0))
    cur = pl.BlockSpec((None, tq, LANES), lambda bi, i: (bi, i, 0))
    prev = pl.BlockSpec((None, w, LANES), lambda bi, i: (bi, jnp.maximum(i * sub_blocks - 1, 0), 0))
    side_specs = _side_cast_specs(side, grid[0] * grid[1], lambda bi, i: bi * grid[1] + i)
    out, *side_bf16 = pl.pallas_call(
        functools.partial(_swa_kernel, q_lane_blocks=qw // LANES, n_side=len(side)),
        grid=grid,
        in_specs=[pl.BlockSpec(memory_space=pltpu.SMEM), q_spec, prev, cur, prev, cur] + side_specs,
        out_specs=[q_spec] + side_specs,
        out_shape=[jax.ShapeDtypeStruct(q.shape, BF16)] + _bf16_like(side),
        scratch_shapes=[pltpu.VMEM((tq + w, LANES), BF16),
                        pltpu.VMEM((sub_blocks + 1, LANES, w), BF16),
                        pltpu.VMEM((sub_blocks * (qw // LANES), w, 2 * w), F32),
                        pltpu.VMEM((sub_blocks * (qw // LANES), LANES, 2 * w), BF16),
                        pltpu.VMEM((sub_blocks * (qw // LANES), LANES, w), F32)],
        compiler_params=pltpu.CompilerParams(
            dimension_semantics=("arbitrary", "arbitrary"), vmem_limit_bytes=VMEM_LIMIT),
        name="swa_attention",
    )(sinks, q, k, k, v, v, *side)
    return out, side_bf16


def _merge_kernel(ya_ref, yb_ref, ga_ref, gb_ref, x_ref, g_ref, wa_ref, wb_ref, wo_ref, o_ref):
    merged = (_sigmoid(ga_ref[...].astype(F32)) * _dot(ya_ref[...], wa_ref[...])
              + _sigmoid(gb_ref[...].astype(F32)) * _dot(yb_ref[...], wb_ref[...]))
    o_ref[...] = x_ref[...] + g_ref[...] * _dot(merged.astype(BF16), wo_ref[...])


def _merge(ya, yb, ga, gb, x2d, gate, wa, wb, wo, *, seq):
    m, d = x2d.shape
    tm = 512
    tiles_per_seq = seq // tm
    row = lambda i: (i, 0)
    const = lambda i: (0, 0)
    resident = lambda shape: pl.BlockSpec(shape, const, pipeline_mode=pl.Buffered(1))
    return pl.pallas_call(
        _merge_kernel,
        grid=(m // tm,),
        in_specs=[
            pl.BlockSpec((tm, ya.shape[1]), row),
            pl.BlockSpec((tm, yb.shape[1]), row),
            pl.BlockSpec((tm, d), row),
            pl.BlockSpec((tm, d), row),
            pl.BlockSpec((tm, d), row),
            pl.BlockSpec((None, 1, d), lambda i: (i // tiles_per_seq, 0, 0)),
            resident(wa.shape), resident(wb.shape), resident(wo.shape),
        ],
        out_specs=pl.BlockSpec((tm, d), row),
        out_shape=jax.ShapeDtypeStruct((m, d), F32),
        compiler_params=pltpu.CompilerParams(
            dimension_semantics=("arbitrary",), vmem_limit_bytes=VMEM_LIMIT),
        name="merge_out_projection",
    )(ya, yb, ga, gb, x2d, gate, wa, wb, wo)


def kernel(x, c, w_ada, b_ada, norm_ffn1, ffn1_gate, ffn1_up, ffn1_down, norm_mix, w_in, swa_sinks,
           w_branch_moba, w_branch_swa, w_out, norm_ffn2, ffn2_gate, ffn2_up, ffn2_down, norm_final):
    b, seq, d = x.shape
    depth = w_ada.shape[0]
    moba_w = w_branch_moba.shape[1]
    swa_qw = w_branch_swa.shape[1]
    kv_w = SWA_KV_HEADS * SWA_HEAD_DIM
    qkv_cols = 3 * moba_w + swa_qw
    assert w_in.shape[2] == qkv_cols + 2 * kv_w + 2 * d
    assert moba_w == swa_qw == 1024 and d == 2048 and kv_w == LANES

    tables = _rope_tables(seq)
    c_pad = jnp.pad(c, ((0, 8 - b), (0, 0)))
    x2d = x.reshape(b * seq, d)

    for l in range(depth):
        mod = _ada_modulation(c_pad, w_ada[l], b_ada[l])[:b].reshape(b, N_ADA, d)
        sh2, sc2, g2 = [mod[:, t:t + 1] for t in (3, 4, 5)]
        gains = jnp.stack([norm_ffn1[l], norm_mix[l], norm_ffn2[l], norm_final])
        vec = jnp.concatenate(
            [mod, jnp.broadcast_to(gains, (b,) + gains.shape),
             jnp.zeros((b, VEC_ROWS - N_ADA - gains.shape[0], d), F32)], axis=1)

        gate1, up1 = _cast_bf16([ffn1_gate[l], ffn1_up[l]], block_rows=256)
        down1, = _cast_bf16([ffn1_down[l]], block_rows=512)
        w_main, w_kv = _split_w_in(w_in[l], qkv_cols, 2 * kv_w)

        x2d = _ffn(x2d, vec, gate1, up1, down1, seq=seq, norm_row=VEC_NORM_FFN1_ROW, ada_row=0,
                   final_norm=False)

        qa, ka, va, qb, kb, vb, ga, gb = _in_projection(
            x2d, norm_mix[l], sh2, sc2, w_main, w_kv, tables, seq=seq)

        rs = lambda t: t.reshape(b, seq, t.shape[-1])
        ya, (gate2, up2, down2) = _moba_attention(
            rs(qa), rs(ka), rs(va), [ffn2_gate[l], ffn2_up[l], ffn2_down[l]],
            heads=moba_w // MOBA_HEAD_DIM)
        yb, (wa, wb, wo) = _swa_attention(
            rs(qb), rs(kb), rs(vb), swa_sinks[l], [w_branch_moba[l], w_branch_swa[l], w_out[l]])

        x2d = _merge(ya.reshape(b * seq, moba_w), yb.reshape(b * seq, swa_qw), ga, gb, x2d, g2,
                     wa, wb, wo, seq=seq)

        x2d = _ffn(x2d, vec, gate2, up2, down2, seq=seq, norm_row=VEC_NORM_FFN2_ROW, ada_row=6,
                   final_norm=(l == depth - 1))

    return x2d.reshape(b, seq, d)
```

```python
import functools

import jax
import jax.numpy as jnp
import numpy as np
from jax import lax
from jax.experimental import pallas as pl
from jax.experimental.pallas import tpu as pltpu

MOBA_HEAD_DIM = 128
MOBA_BLOCK = 256
MOBA_TOPK = 3
SWA_HEAD_DIM = 64
SWA_KV_HEADS = 2
SWA_WINDOW = 128
ROPE_THETA = 10000.0
EPS = 1e-6
N_ADA = 9
VEC_NORM_FFN1_ROW, VEC_NORM_MIX_ROW, VEC_NORM_FFN2_ROW, VEC_FINAL_NORM_ROW = 9, 10, 11, 12
VEC_ROWS = 16

LANES = 128
BF16_SUBLANES = 16
ROW_CHUNK = 16
VMEM_LIMIT = 56 * 1024 * 1024
MASK_VALUE = -1e30
LOG2_E = 1.4426950408889634
MOBA_LOOP_GROUP = 2
MOBA_HEADS_PER_STEP = 4

F32 = jnp.float32
BF16 = jnp.bfloat16


def _sigmoid(x):
    return 1.0 / (1.0 + jnp.exp(-x))


def _row_rsqrt_mean_square(x):
    r = lax.rsqrt(jnp.mean(x * x, axis=-1, keepdims=True) + EPS)
    return jnp.broadcast_to(r, (x.shape[0], LANES))


def _across_lanes(r, width):
    return jnp.concatenate([r] * (width // LANES), axis=1)


def _for_row_chunks(rows, body):
    for c in range(rows // ROW_CHUNK):
        body(pl.ds(c * ROW_CHUNK, ROW_CHUNK))


def _rms_modulate_to(h_ref, x_ref, rinv_scr, norm_w, shift, scale):
    rinv_scr[...] = _row_rsqrt_mean_square(x_ref[...])
    gain = norm_w * (1.0 + scale)

    def slab(rows):
        rinv = _across_lanes(rinv_scr[rows, :], x_ref.shape[1])
        h_ref[rows, :] = (x_ref[rows, :] * rinv * gain + shift).astype(BF16)

    _for_row_chunks(x_ref.shape[0], slab)


def _dot(a, b):
    return jnp.dot(a, b, preferred_element_type=F32)


def _dot_nt(a, b):
    return lax.dot_general(a, b, (((1,), (1,)), ((), ())), preferred_element_type=F32)


def _cast_kernel(*refs):
    n = len(refs) // 2
    for src, dst in zip(refs[:n], refs[n:]):
        dst[...] = src[...].astype(dst.dtype)


def _cast_bf16(arrays, block_rows):
    rows, cols = arrays[0].shape
    spec = pl.BlockSpec((block_rows, cols), lambda r: (r, 0))
    n = len(arrays)
    return pl.pallas_call(
        _cast_kernel,
        grid=(rows // block_rows,),
        in_specs=[spec] * n,
        out_specs=[spec] * n,
        out_shape=[jax.ShapeDtypeStruct((rows, cols), BF16)] * n,
        compiler_params=pltpu.CompilerParams(
            dimension_semantics=("arbitrary",), vmem_limit_bytes=VMEM_LIMIT),
        name="cast_bf16",
    )(*arrays)


def _side_cast_specs(arrays, steps, step_index):
    specs = []
    for a in arrays:
        rows, cols = a.shape
        for col_blocks in (1, 2, 4, 8, 16):
            row_blocks = steps // col_blocks
            if rows % (row_blocks * BF16_SUBLANES) == 0 and cols % (col_blocks * LANES) == 0:
                break
        else:
            raise ValueError(f"no {steps}-block tiling for {a.shape}")
        index = lambda *g, cb=col_blocks: (step_index(*g) // cb, step_index(*g) % cb)
        specs.append(pl.BlockSpec((rows // row_blocks, cols // col_blocks), index))
    return specs


def _side_cast(src_refs, dst_refs):
    for src, dst in zip(src_refs, dst_refs):
        dst[...] = src[...].astype(dst.dtype)


def _bf16_like(arrays):
    return [jax.ShapeDtypeStruct(a.shape, BF16) for a in arrays]


def _split_w_in_kernel(w_ref, main_ref, kv_ref, *, kv_block):
    j = pl.program_id(0)

    @pl.when(j != kv_block)
    def _():
        main_ref[...] = w_ref[...].astype(BF16)

    @pl.when(j == kv_block)
    def _():
        kv_ref[...] = w_ref[...].astype(BF16)


def _split_w_in(w, kv_start, kv_width):
    d, cols = w.shape
    assert kv_start % kv_width == 0 and cols % kv_width == 0
    kv_block = kv_start // kv_width
    main_col = lambda j: (0, jnp.where(j > kv_block, j - 1, jnp.minimum(j, kv_block - 1)))
    return pl.pallas_call(
        functools.partial(_split_w_in_kernel, kv_block=kv_block),
        grid=(cols // kv_width,),
        in_specs=[pl.BlockSpec((d, kv_width), lambda j: (0, j))],
        out_specs=[pl.BlockSpec((d, kv_width), main_col),
                   pl.BlockSpec((d, kv_width), lambda j: (0, 0))],
        out_shape=[jax.ShapeDtypeStruct((d, cols - kv_width), BF16),
                   jax.ShapeDtypeStruct((d, kv_width), BF16)],
        compiler_params=pltpu.CompilerParams(
            dimension_semantics=("arbitrary",), vmem_limit_bytes=VMEM_LIMIT),
        name="split_w_in",
    )(w)


def _ada_kernel(c_ref, w_ref, b_ref, o_ref):
    c = c_ref[...]
    s = (c * _sigmoid(c)).astype(BF16)
    o_ref[...] = _dot(s, w_ref[...].astype(BF16)) + b_ref[...]


def _ada_modulation(c_pad, w, b):
    rows, d = c_pad.shape
    n = w.shape[1]
    tn = 1024
    return pl.pallas_call(
        _ada_kernel,
        grid=(n // tn,),
        in_specs=[
            pl.BlockSpec((rows, d), lambda j: (0, 0)),
            pl.BlockSpec((d, tn), lambda j: (0, j)),
            pl.BlockSpec((1, tn), lambda j: (0, j)),
        ],
        out_specs=pl.BlockSpec((rows, tn), lambda j: (0, j)),
        out_shape=jax.ShapeDtypeStruct((rows, n), F32),
        compiler_params=pltpu.CompilerParams(
            dimension_semantics=("arbitrary",), vmem_limit_bytes=VMEM_LIMIT),
        name="ada_modulation",
    )(c_pad, w, b.reshape(1, n))


def _ffn_kernel(x_ref, vec_ref, wg_ref, wu_ref, wd_ref, o_ref, h_scr, acc_scr, rinv_scr, *,
                final_norm, tiles_per_seq, norm_row, ada_row):
    f = pl.program_id(1)
    nf = pl.num_programs(1)
    batch = pl.program_id(0) // tiles_per_seq
    vec = lambda r: vec_ref[batch, r:r + 1, :]

    def down_projection():
        h = h_scr[...]
        gate = _dot(h, wg_ref[...])
        up = _dot(h, wu_ref[...])
        act = (gate * _sigmoid(gate) * up).astype(BF16)
        return _dot(act, wd_ref[...])

    @pl.when(f == 0)
    def _():
        _rms_modulate_to(h_scr, x_ref, rinv_scr, vec(norm_row), vec(ada_row), vec(ada_row + 1))
        acc_scr[...] = down_projection()

    @pl.when((f > 0) & (f < nf - 1))
    def _():
        acc_scr[...] += down_projection()

    @pl.when(f == nf - 1)
    def _():
        y = x_ref[...] + (0.5 * vec(ada_row + 2)) * (acc_scr[...] + down_projection())
        o_ref[...] = y
        if final_norm:
            rinv_scr[...] = _row_rsqrt_mean_square(y)
            final_w = vec(VEC_FINAL_NORM_ROW)

            def slab(rows):
                rinv = _across_lanes(rinv_scr[rows, :], o_ref.shape[1])
                o_ref[rows, :] = o_ref[rows, :] * rinv * final_w

            _for_row_chunks(o_ref.shape[0], slab)


def _ffn(x2d, vec, wg, wu, wd, *, seq, norm_row, ada_row, final_norm):
    m, d = x2d.shape
    dff = wg.shape[1]
    tm, tf = 512, 512
    assert dff // tf >= 2
    row = lambda i, f: (i, 0)
    return pl.pallas_call(
        functools.partial(_ffn_kernel, final_norm=final_norm, tiles_per_seq=seq // tm,
                          norm_row=norm_row, ada_row=ada_row),
        grid=(m // tm, dff // tf),
        in_specs=[
            pl.BlockSpec((tm, d), row),
            pl.BlockSpec(vec.shape, lambda i, f: (0, 0, 0)),
            pl.BlockSpec((d, tf), lambda i, f: (0, f)),
            pl.BlockSpec((d, tf), lambda i, f: (0, f)),
            pl.BlockSpec((tf, d), lambda i, f: (f, 0)),
        ],
        out_specs=pl.BlockSpec((tm, d), row),
        out_shape=jax.ShapeDtypeStruct((m, d), F32),
        scratch_shapes=[pltpu.VMEM((tm, d), BF16), pltpu.VMEM((tm, d), F32),
                        pltpu.VMEM((tm, LANES), F32)],
        compiler_params=pltpu.CompilerParams(
            dimension_semantics=("arbitrary", "arbitrary"), vmem_limit_bytes=VMEM_LIMIT),
        name="ffn_final" if final_norm else "ffn",
    )(x2d, vec, wg, wu, wd)


def _rope128(x, cos, sin):
    return x * cos + pltpu.roll(x, 64, 1) * sin


def _rope64(x, cos, sin_lo, sin_hi):
    return x * cos + pltpu.roll(x, 96, 1) * sin_lo + pltpu.roll(x, 32, 1) * sin_hi


def _proj_kernel(x_ref, nw_ref, sh_ref, sc_ref, w_ref, wkv_ref,
                 cos_a_ref, sin_a_ref, cos_b_ref, sin_lo_ref, sin_hi_ref,
                 qa_ref, ka_ref, va_ref, qb_ref, kb_ref, vb_ref, ga_ref, gb_ref, h_scr, rinv_scr):
    n = pl.program_id(1)
    chunk = 2 * LANES
    identity = lambda x: x

    def project(dst_ref, epilogue, col0):
        h = h_scr[...]
        for s in range(0, dst_ref.shape[1], chunk):
            acc = _dot(h, w_ref[:, col0 + s:col0 + s + chunk])
            for t in range(0, chunk, LANES):
                dst_ref[:, s + t:s + t + LANES] = epilogue(acc[:, t:t + LANES]).astype(BF16)

    rope_a = lambda x: _rope128(x, cos_a_ref[...], sin_a_ref[...])
    rope_b = lambda x: _rope64(x, cos_b_ref[...], sin_lo_ref[...], sin_hi_ref[...])

    @pl.when(n == 0)
    def _():
        _rms_modulate_to(h_scr, x_ref, rinv_scr, nw_ref[...], sh_ref[...], sc_ref[...])
        project(qa_ref, rope_a, 0)
        project(ka_ref, rope_a, qa_ref.shape[1])

    @pl.when(n == 1)
    def _():
        project(va_ref, identity, 0)
        project(qb_ref, rope_b, va_ref.shape[1])
        kv = _dot(h_scr[...], wkv_ref[...])
        kb_ref[...] = rope_b(kv[:, :LANES]).astype(BF16)
        vb_ref[...] = kv[:, LANES:].astype(BF16)

    @pl.when(n == 2)
    def _():
        project(ga_ref, identity, 0)

    @pl.when(n == 3)
    def _():
        project(gb_ref, identity, 0)


def _in_projection(x2d, norm_w, shift, scale, w_main, w_kv, tables, *, seq):
    m, d = x2d.shape
    tm, tn, half = 512, 2048, 1024
    n_tiles = w_main.shape[1] // tn
    assert n_tiles == 4 and d == tn
    tiles_per_seq = seq // tm
    per_batch = lambda i, n: (i // tiles_per_seq, 0, 0)
    pos = lambda i, n: (i % tiles_per_seq, 0)
    full = lambda i, n: (i, 0)
    wide = lambda cols: jax.ShapeDtypeStruct((m, cols), BF16)
    table_spec = pl.BlockSpec((tm, LANES), pos)
    return pl.pallas_call(
        _proj_kernel,
        grid=(m // tm, n_tiles),
        in_specs=[
            pl.BlockSpec((tm, d), full),
            pl.BlockSpec((1, d), lambda i, n: (0, 0)),
            pl.BlockSpec((None, 1, d), per_batch),
            pl.BlockSpec((None, 1, d), per_batch),
            pl.BlockSpec((d, tn), lambda i, n: (0, n)),
            pl.BlockSpec((d, 2 * LANES), lambda i, n: (0, 0)),
            table_spec, table_spec, table_spec, table_spec, table_spec,
        ],
        out_specs=[
            pl.BlockSpec((tm, half), full),
            pl.BlockSpec((tm, half), full),
            pl.BlockSpec((tm, half), full),
            pl.BlockSpec((tm, half), full),
            pl.BlockSpec((tm, LANES), full),
            pl.BlockSpec((tm, LANES), full),
            pl.BlockSpec((tm, tn), full),
            pl.BlockSpec((tm, tn), full),
        ],
        out_shape=[wide(half), wide(half), wide(half), wide(half), wide(LANES), wide(LANES),
                   wide(tn), wide(tn)],
        scratch_shapes=[pltpu.VMEM((tm, d), BF16), pltpu.VMEM((tm, LANES), F32)],
        compiler_params=pltpu.CompilerParams(
            dimension_semantics=("arbitrary", "arbitrary"), vmem_limit_bytes=VMEM_LIMIT),
        name="in_projection",
    )(x2d, norm_w.reshape(1, d), shift, scale, w_main, w_kv, *tables)


def _rope_tables(seq):
    pos = np.arange(seq, dtype=np.float64)[:, None]

    def cos_sin(half):
        inv = ROPE_THETA ** (-np.arange(half, dtype=np.float64) / half)
        ang = pos * inv[None, :]
        return np.cos(ang), np.sin(ang)

    cos, sin = cos_sin(MOBA_HEAD_DIM // 2)
    cos_a = np.concatenate([cos, cos], axis=1)
    sin_a = np.concatenate([-sin, sin], axis=1)
    cos, sin = cos_sin(SWA_HEAD_DIM // 2)
    zero = np.zeros_like(sin)
    cos_b = np.concatenate([cos, cos, cos, cos], axis=1)
    sin_lo = np.concatenate([-sin, zero, -sin, zero], axis=1)
    sin_hi = np.concatenate([zero, sin, zero, sin], axis=1)
    return tuple(jnp.asarray(t, dtype=F32) for t in (cos_a, sin_a, cos_b, sin_lo, sin_hi))


def _moba_kernel(*refs, topk, group, n_side):
    q_ref, k_ref, v_ref = refs[:3]
    o_ref = refs[3 + n_side]
    kaug_scr, vt_scr, kmean_scr, s_scr = refs[4 + 2 * n_side:]
    _side_cast(refs[3:3 + n_side], refs[4 + n_side:4 + 2 * n_side])

    i = pl.program_id(2)
    blk = q_ref.shape[0]
    heads, nb, dh, _ = vt_scr.shape
    exp2_scale = dh ** -0.5 * LOG2_E
    head_lanes = lambda h: slice(h * dh, (h + 1) * dh)

    @pl.when(i == 0)
    def _():
        col = lax.broadcasted_iota(jnp.int32, (blk, LANES), 1)
        for h in range(heads):
            for j in range(nb):
                k = k_ref[j * blk:(j + 1) * blk, head_lanes(h)]
                kmean_scr[h, j:j + 1, :] = jnp.mean(k.astype(F32), axis=0, keepdims=True)
                kaug_scr[h, j, :, :dh] = k
                kaug_scr[h, j, :, dh:] = jnp.where(col == j, 1.0, 0.0).astype(BF16)
                v = v_ref[j * blk:(j + 1) * blk, head_lanes(h)]
                vt_scr[h, j] = v.astype(F32).T.astype(BF16)

    def gate(h):
        q = q_ref[:, head_lanes(h)]
        gs = _dot_nt(kmean_scr[h].astype(BF16), q)
        blk_id = lax.broadcasted_iota(jnp.int32, gs.shape, 0)
        valid = blk_id < i
        vals = jnp.where(valid, gs, -jnp.inf)
        sel = jnp.zeros(gs.shape, dtype=jnp.bool_)
        for _ in range(topk):
            best = jnp.max(vals, axis=0, keepdims=True)
            first = jnp.min(jnp.where(vals == best, blk_id, nb), axis=0, keepdims=True)
            pick = blk_id == first
            sel = sel | pick
            vals = jnp.where(pick, -jnp.inf, vals)
        sel = sel & valid
        bias_t = jnp.where(sel, 0.0, MASK_VALUE).astype(F32)
        bias_t = jnp.concatenate([bias_t, jnp.zeros((LANES - nb, blk), F32)], axis=0)
        q_t = q.astype(F32).T.astype(BF16)
        return jnp.concatenate([q_t, bias_t.astype(BF16)], axis=0)

    def attend(n_past):
        q_aug_t = [gate(h) for h in range(heads)]
        own_slot = n_past
        r = lax.broadcasted_iota(jnp.int32, (blk, blk), 0)
        c = lax.broadcasted_iota(jnp.int32, (blk, blk), 1)
        m = []
        for h in range(heads):
            s = _dot(kaug_scr[h, i, :, :dh], q_aug_t[h][:dh]) * exp2_scale
            s = jnp.where(r <= c, s, MASK_VALUE)
            s_scr[h, own_slot] = s
            mh = jnp.max(s, axis=0, keepdims=True)
            for j in range(n_past):
                s = _dot(kaug_scr[h, j], q_aug_t[h]) * exp2_scale
                s_scr[h, j] = s
                mh = jnp.maximum(mh, jnp.max(s, axis=0, keepdims=True))
            m.append(mh)

        for h in range(heads):
            l = jnp.zeros_like(m[h])
            acc = jnp.zeros((dh, blk), F32)
            for slot in range(n_past + 1):
                p = jnp.exp2(s_scr[h, slot] - m[h])
                l = l + jnp.sum(p, axis=0, keepdims=True)
                vt = vt_scr[h, i] if slot == own_slot else vt_scr[h, slot]
                acc = acc + _dot(vt, p.astype(BF16))
            o_ref[:, head_lanes(h)] = (acc / l).T.astype(o_ref.dtype)

    n_groups = (i + group - 1) // group
    for n in range(nb // group + 1):
        pl.when(n_groups == n)(functools.partial(attend, n * group))


def _moba_attention(q, k, v, side, *, heads):
    b, seq, _ = q.shape
    dh, blk = MOBA_HEAD_DIM, MOBA_BLOCK
    nb = seq // blk
    hps = MOBA_HEADS_PER_STEP
    grid = (b, heads // hps, nb)
    q_spec = pl.BlockSpec((None, blk, hps * dh), lambda bi, h, i: (bi, i, h))
    kv_spec = pl.BlockSpec((None, seq, hps * dh), lambda bi, h, i: (bi, 0, h))
    side_specs = _side_cast_specs(side, grid[0] * grid[1] * grid[2],
                                  lambda bi, h, i: (bi * grid[1] + h) * grid[2] + i)
    out, *side_bf16 = pl.pallas_call(
        functools.partial(_moba_kernel, topk=min(MOBA_TOPK, nb), group=MOBA_LOOP_GROUP,
                          n_side=len(side)),
        grid=grid,
        in_specs=[q_spec, kv_spec, kv_spec] + side_specs,
        out_specs=[q_spec] + side_specs,
        out_shape=[jax.ShapeDtypeStruct(q.shape, BF16)] + _bf16_like(side),
        scratch_shapes=[pltpu.VMEM((hps, nb, blk, dh + LANES), BF16),
                        pltpu.VMEM((hps, nb, dh, blk), BF16),
                        pltpu.VMEM((hps, nb, dh), F32),
                        pltpu.VMEM((hps, nb + 1, blk, blk), F32)],
        compiler_params=pltpu.CompilerParams(
            dimension_semantics=("arbitrary", "arbitrary", "arbitrary"),
            vmem_limit_bytes=VMEM_LIMIT),
        name="moba_attention",
    )(q, k, v, *side)
    return out, side_bf16


def _swa_kernel(*refs, q_lane_blocks, n_side):
    sink_ref, q_ref, kp_ref, kc_ref, vp_ref, vc_ref = refs[:6]
    o_ref = refs[6 + n_side]
    k_scr, vt_scr, s_scr, qt_scr, ot_scr = refs[7 + 2 * n_side:]
    _side_cast(refs[6:6 + n_side], refs[7 + n_side:7 + 2 * n_side])

    i = pl.program_id(1)
    w = kp_ref.shape[0]
    tq = q_ref.shape[0]
    sub_blocks = tq // w
    half = LANES // 2
    scale = half ** -0.5
    groups = q_lane_blocks // SWA_KV_HEADS
    lanes = lambda jb: slice(jb * LANES, (jb + 1) * LANES)
    zeros = jnp.zeros((half, 2 * w), BF16)

    k_scr[:w, :] = kp_ref[...]
    k_scr[w:, :] = kc_ref[...]
    transposed = lambda x: x.astype(F32).T.astype(BF16)
    vt_scr[0] = transposed(vp_ref[...])
    for c in range(sub_blocks):
        vt_scr[c + 1] = transposed(vc_ref[c * w:(c + 1) * w, :])
    for u in range(sub_blocks):
        for jb in range(q_lane_blocks):
            qt = (q_ref[u * w:(u + 1) * w, lanes(jb)].astype(F32) * scale).T.astype(BF16)
            q2 = jnp.concatenate([qt[:half], qt[half:]], axis=1)
            qt_scr[u * q_lane_blocks + jb] = jnp.concatenate(
                [q2, zeros] if jb < groups else [zeros, q2], axis=0)

    key = lax.broadcasted_iota(jnp.int32, (w, 2 * w), 0)
    col = lax.broadcasted_iota(jnp.int32, (w, 2 * w), 1)
    from_prev = key > jnp.where(col >= w, col - w, col)
    odd_head = lax.broadcasted_iota(jnp.int32, (1, 2 * w), 1) >= w

    def sub_block(u):
        prev_bias = jnp.where((i * sub_blocks + u) == 0, MASK_VALUE, 0.0)
        k2 = k_scr[u * w:(u + 2) * w, :]
        vt2 = jnp.concatenate([vt_scr[u], vt_scr[u + 1]], axis=1)

        sinks, maxima = [], []
        for jb in range(q_lane_blocks):
            s = _dot(k2, qt_scr[u * q_lane_blocks + jb])
            s = jnp.where(from_prev, s[:w] + prev_bias, s[w:])
            s_scr[u * q_lane_blocks + jb] = s
            sink = jnp.where(odd_head, sink_ref[2 * jb + 1], sink_ref[2 * jb])
            sinks.append(sink)
            maxima.append(jnp.maximum(jnp.max(s, axis=0, keepdims=True), sink))

        for jb in range(q_lane_blocks):
            kvh = jb // groups
            p = jnp.exp(s_scr[u * q_lane_blocks + jb] - maxima[jb])
            den = jnp.sum(p, axis=0, keepdims=True) + jnp.exp(sinks[jb] - maxima[jb])
            p2 = jnp.concatenate([jnp.where(from_prev, p, 0.0), jnp.where(from_prev, 0.0, p)],
                                 axis=0).astype(BF16)
            o = _dot(vt2[kvh * half:(kvh + 1) * half], p2) / den
            ot_scr[u * q_lane_blocks + jb] = jnp.concatenate([o[:, :w], o[:, w:]], axis=0)

    for u in range(sub_blocks):
        sub_block(u)

    for u in range(sub_blocks):
        for jb in range(q_lane_blocks):
            o_ref[u * w:(u + 1) * w, lanes(jb)] = ot_scr[u * q_lane_blocks + jb].T.astype(
                o_ref.dtype)


def _swa_attention(q, k, v, sinks, side):
    b, seq, qw = q.shape
    w = SWA_WINDOW
    tq = 512
    sub_blocks = tq // w
    grid = (b, seq // tq)
    q_spec = pl.BlockSpec((None, tq, qw), lambda bi, i: (bi, i, 0))
    cur = pl.BlockSpec((None, tq, LANES), lambda bi, i: (bi, i, 0))
    prev = pl.BlockSpec((None, w, LANES), lambda bi, i: (bi, jnp.maximum(i * sub_blocks - 1, 0), 0))
    side_specs = _side_cast_specs(side, grid[0] * grid[1], lambda bi, i: bi * grid[1] + i)
    out, *side_bf16 = pl.pallas_call(
        functools.partial(_swa_kernel, q_lane_blocks=qw // LANES, n_side=len(side)),
        grid=grid,
        in_specs=[pl.BlockSpec(memory_space=pltpu.SMEM), q_spec, prev, cur, prev, cur] + side_specs,
        out_specs=[q_spec] + side_specs,
        out_shape=[jax.ShapeDtypeStruct(q.shape, BF16)] + _bf16_like(side),
        scratch_shapes=[pltpu.VMEM((tq + w, LANES), BF16),
                        pltpu.VMEM((sub_blocks + 1, LANES, w), BF16),
                        pltpu.VMEM((sub_blocks * (qw // LANES), w, 2 * w), F32),
                        pltpu.VMEM((sub_blocks * (qw // LANES), LANES, 2 * w), BF16),
                        pltpu.VMEM((sub_blocks * (qw // LANES), LANES, w), F32)],
        compiler_params=pltpu.CompilerParams(
            dimension_semantics=("arbitrary", "arbitrary"), vmem_limit_bytes=VMEM_LIMIT),
        name="swa_attention",
    )(sinks, q, k, k, v, v, *side)
    return out, side_bf16


def _merge_kernel(ya_ref, yb_ref, ga_ref, gb_ref, x_ref, g_ref, wa_ref, wb_ref, wo_ref, o_ref):
    merged = (_sigmoid(ga_ref[...].astype(F32)) * _dot(ya_ref[...], wa_ref[...])
              + _sigmoid(gb_ref[...].astype(F32)) * _dot(yb_ref[...], wb_ref[...]))
    o_ref[...] = x_ref[...] + g_ref[...] * _dot(merged.astype(BF16), wo_ref[...])


def _merge(ya, yb, ga, gb, x2d, gate, wa, wb, wo, *, seq):
    m, d = x2d.shape
    tm = 512
    tiles_per_seq = seq // tm
    row = lambda i: (i, 0)
    const = lambda i: (0, 0)
    resident = lambda shape: pl.BlockSpec(shape, const, pipeline_mode=pl.Buffered(1))
    return pl.pallas_call(
        _merge_kernel,
        grid=(m // tm,),
        in_specs=[
            pl.BlockSpec((tm, ya.shape[1]), row),
            pl.BlockSpec((tm, yb.shape[1]), row),
            pl.BlockSpec((tm, d), row),
            pl.BlockSpec((tm, d), row),
            pl.BlockSpec((tm, d), row),
            pl.BlockSpec((None, 1, d), lambda i: (i // tiles_per_seq, 0, 0)),
            resident(wa.shape), resident(wb.shape), resident(wo.shape),
        ],
        out_specs=pl.BlockSpec((tm, d), row),
        out_shape=jax.ShapeDtypeStruct((m, d), F32),
        compiler_params=pltpu.CompilerParams(
            dimension_semantics=("arbitrary",), vmem_limit_bytes=VMEM_LIMIT),
        name="merge_out_projection",
    )(ya, yb, ga, gb, x2d, gate, wa, wb, wo)


def kernel(x, c, w_ada, b_ada, norm_ffn1, ffn1_gate, ffn1_up, ffn1_down, norm_mix, w_in, swa_sinks,
           w_branch_moba, w_branch_swa, w_out, norm_ffn2, ffn2_gate, ffn2_up, ffn2_down, norm_final):
    b, seq, d = x.shape
    depth = w_ada.shape[0]
    moba_w = w_branch_moba.shape[1]
    swa_qw = w_branch_swa.shape[1]
    kv_w = SWA_KV_HEADS * SWA_HEAD_DIM
    qkv_cols = 3 * moba_w + swa_qw
    assert w_in.shape[2] == qkv_cols + 2 * kv_w + 2 * d
    assert moba_w == swa_qw == 1024 and d == 2048 and kv_w == LANES

    tables = _rope_tables(seq)
    c_pad = jnp.pad(c, ((0, 8 - b), (0, 0)))
    x2d = x.reshape(b * seq, d)

    for l in range(depth):
        mod = _ada_modulation(c_pad, w_ada[l], b_ada[l])[:b].reshape(b, N_ADA, d)
        sh2, sc2, g2 = [mod[:, t:t + 1] for t in (3, 4, 5)]
        gains = jnp.stack([norm_ffn1[l], norm_mix[l], norm_ffn2[l], norm_final])
        vec = jnp.concatenate(
            [mod, jnp.broadcast_to(gains, (b,) + gains.shape),
             jnp.zeros((b, VEC_ROWS - N_ADA - gains.shape[0], d), F32)], axis=1)

        gate1, up1 = _cast_bf16([ffn1_gate[l], ffn1_up[l]], block_rows=256)
        down1, = _cast_bf16([ffn1_down[l]], block_rows=512)
        w_main, w_kv = _split_w_in(w_in[l], qkv_cols, 2 * kv_w)

        x2d = _ffn(x2d, vec, gate1, up1, down1, seq=seq, norm_row=VEC_NORM_FFN1_ROW, ada_row=0,
                   final_norm=False)

        qa, ka, va, qb, kb, vb, ga, gb = _in_projection(
            x2d, norm_mix[l], sh2, sc2, w_main, w_kv, tables, seq=seq)

        rs = lambda t: t.reshape(b, seq, t.shape[-1])
        ya, (gate2, up2, down2) = _moba_attention(
            rs(qa), rs(ka), rs(va), [ffn2_gate[l], ffn2_up[l], ffn2_down[l]],
            heads=moba_w // MOBA_HEAD_DIM)
        yb, (wa, wb, wo) = _swa_attention(
            rs(qb), rs(kb), rs(vb), swa_sinks[l], [w_branch_moba[l], w_branch_swa[l], w_out[l]])

        x2d = _merge(ya.reshape(b * seq, moba_w), yb.reshape(b * seq, swa_qw), ga, gb, x2d, g2,
                     wa, wb, wo, seq=seq)

        x2d = _ffn(x2d, vec, gate2, up2, down2, seq=seq, norm_row=VEC_NORM_FFN2_ROW, ada_row=6,
                   final_norm=(l == depth - 1))

    return x2d.reshape(b, seq, d)
```

```python
import functools

import jax
import jax.numpy as jnp
import numpy as np
from jax import lax
from jax.experimental import pallas as pl
from jax.experimental.pallas import tpu as pltpu

MOBA_HEAD_DIM = 128
MOBA_BLOCK = 256
MOBA_TOPK = 3
SWA_HEAD_DIM = 64
SWA_KV_HEADS = 2
SWA_WINDOW = 128
ROPE_THETA = 10000.0
EPS = 1e-6
N_ADA = 9
VEC_NORM_FFN1_ROW, VEC_NORM_MIX_ROW, VEC_NORM_FFN2_ROW, VEC_FINAL_NORM_ROW = 9, 10, 11, 12
VEC_ROWS = 16

LANES = 128
BF16_SUBLANES = 16
ROW_CHUNK = 16
VMEM_LIMIT = 56 * 1024 * 1024
MASK_VALUE = -1e30
LOG2_E = 1.4426950408889634
MOBA_LOOP_GROUP = 2
MOBA_HEADS_PER_STEP = 4

F32 = jnp.float32
BF16 = jnp.bfloat16


def _sigmoid(x):
    return 1.0 / (1.0 + jnp.exp(-x))


def _row_rsqrt_mean_square(x):
    r = lax.rsqrt(jnp.mean(x * x, axis=-1, keepdims=True) + EPS)
    return jnp.broadcast_to(r, (x.shape[0], LANES))


def _across_lanes(r, width):
    return jnp.concatenate([r] * (width // LANES), axis=1)


def _for_row_chunks(rows, body):
    for c in range(rows // ROW_CHUNK):
        body(pl.ds(c * ROW_CHUNK, ROW_CHUNK))


def _rms_modulate_to(h_ref, x_ref, rinv_scr, norm_w, shift, scale):
    rinv_scr[...] = _row_rsqrt_mean_square(x_ref[...])
    gain = norm_w * (1.0 + scale)

    def slab(rows):
        rinv = _across_lanes(rinv_scr[rows, :], x_ref.shape[1])
        h_ref[rows, :] = (x_ref[rows, :] * rinv * gain + shift).astype(BF16)

    _for_row_chunks(x_ref.shape[0], slab)


def _dot(a, b):
    return jnp.dot(a, b, preferred_element_type=F32)


def _dot_nt(a, b):
    return lax.dot_general(a, b, (((1,), (1,)), ((), ())), preferred_element_type=F32)


def _cast_kernel(*refs):
    n = len(refs) // 2
    for src, dst in zip(refs[:n], refs[n:]):
        dst[...] = src[...].astype(dst.dtype)


def _cast_bf16(arrays, block_rows):
    rows, cols = arrays[0].shape
    spec = pl.BlockSpec((block_rows, cols), lambda r: (r, 0))
    n = len(arrays)
    return pl.pallas_call(
        _cast_kernel,
        grid=(rows // block_rows,),
        in_specs=[spec] * n,
        out_specs=[spec] * n,
        out_shape=[jax.ShapeDtypeStruct((rows, cols), BF16)] * n,
        compiler_params=pltpu.CompilerParams(
            dimension_semantics=("arbitrary",), vmem_limit_bytes=VMEM_LIMIT),
        name="cast_bf16",
    )(*arrays)


def _side_cast_specs(arrays, steps, step_index):
    specs = []
    for a in arrays:
        rows, cols = a.shape
        for col_blocks in (1, 2, 4, 8, 16):
            row_blocks = steps // col_blocks
            if rows % (row_blocks * BF16_SUBLANES) == 0 and cols % (col_blocks * LANES) == 0:
                break
        else:
            raise ValueError(f"no {steps}-block tiling for {a.shape}")
        index = lambda *g, cb=col_blocks: (step_index(*g) // cb, step_index(*g) % cb)
        specs.append(pl.BlockSpec((rows // row_blocks, cols // col_blocks), index))
    return specs


def _side_cast(src_refs, dst_refs):
    for src, dst in zip(src_refs, dst_refs):
        dst[...] = src[...].astype(dst.dtype)


def _bf16_like(arrays):
    return [jax.ShapeDtypeStruct(a.shape, BF16) for a in arrays]


def _split_w_in_kernel(w_ref, main_ref, kv_ref, *, kv_block):
    j = pl.program_id(0)

    @pl.when(j != kv_block)
    def _():
        main_ref[...] = w_ref[...].astype(BF16)

    @pl.when(j == kv_block)
    def _():
        kv_ref[...] = w_ref[...].astype(BF16)


def _split_w_in(w, kv_start, kv_width):
    d, cols = w.shape
    assert kv_start % kv_width == 0 and cols % kv_width == 0
    kv_block = kv_start // kv_width
    main_col = lambda j: (0, jnp.where(j > kv_block, j - 1, jnp.minimum(j, kv_block - 1)))
    return pl.pallas_call(
        functools.partial(_split_w_in_kernel, kv_block=kv_block),
        grid=(cols // kv_width,),
        in_specs=[pl.BlockSpec((d, kv_width), lambda j: (0, j))],
        out_specs=[pl.BlockSpec((d, kv_width), main_col),
                   pl.BlockSpec((d, kv_width), lambda j: (0, 0))],
        out_shape=[jax.ShapeDtypeStruct((d, cols - kv_width), BF16),
                   jax.ShapeDtypeStruct((d, kv_width), BF16)],
        compiler_params=pltpu.CompilerParams(
            dimension_semantics=("arbitrary",), vmem_limit_bytes=VMEM_LIMIT),
        name="split_w_in",
    )(w)


def _ada_kernel(c_ref, w_ref, b_ref, o_ref):
    c = c_ref[...]
    s = (c * _sigmoid(c)).astype(BF16)
    o_ref[...] = _dot(s, w_ref[...].astype(BF16)) + b_ref[...]


def _ada_modulation(c_pad, w, b):
    rows, d = c_pad.shape
    n = w.shape[1]
    tn = 1024
    return pl.pallas_call(
        _ada_kernel,
        grid=(n // tn,),
        in_specs=[
            pl.BlockSpec((rows, d), lambda j: (0, 0)),
            pl.BlockSpec((d, tn), lambda j: (0, j)),
            pl.BlockSpec((1, tn), lambda j: (0, j)),
        ],
        out_specs=pl.BlockSpec((rows, tn), lambda j: (0, j)),
        out_shape=jax.ShapeDtypeStruct((rows, n), F32),
        compiler_params=pltpu.CompilerParams(
            dimension_semantics=("arbitrary",), vmem_limit_bytes=VMEM_LIMIT),
        name="ada_modulation",
    )(c_pad, w, b.reshape(1, n))


def _ffn_kernel(x_ref, vec_ref, wg_ref, wu_ref, wd_ref, o_ref, h_scr, acc_scr, rinv_scr, *,
                final_norm, tiles_per_seq, norm_row, ada_row):
    f = pl.program_id(1)
    nf = pl.num_programs(1)
    batch = pl.program_id(0) // tiles_per_seq
    vec = lambda r: vec_ref[batch, r:r + 1, :]

    def down_projection():
        h = h_scr[...]
        gate = _dot(h, wg_ref[...])
        up = _dot(h, wu_ref[...])
        act = (gate * _sigmoid(gate) * up).astype(BF16)
        return _dot(act, wd_ref[...])

    @pl.when(f == 0)
    def _():
        _rms_modulate_to(h_scr, x_ref, rinv_scr, vec(norm_row), vec(ada_row), vec(ada_row + 1))
        acc_scr[...] = down_projection()

    @pl.when((f > 0) & (f < nf - 1))
    def _():
        acc_scr[...] += down_projection()

    @pl.when(f == nf - 1)
    def _():
        y = x_ref[...] + (0.5 * vec(ada_row + 2)) * (acc_scr[...] + down_projection())
        o_ref[...] = y
        if final_norm:
            rinv_scr[...] = _row_rsqrt_mean_square(y)
            final_w = vec(VEC_FINAL_NORM_ROW)

            def slab(rows):
                rinv = _across_lanes(rinv_scr[rows, :], o_ref.shape[1])
                o_ref[rows, :] = o_ref[rows, :] * rinv * final_w

            _for_row_chunks(o_ref.shape[0], slab)


def _ffn(x2d, vec, wg, wu, wd, *, seq, norm_row, ada_row, final_norm):
    m, d = x2d.shape
    dff = wg.shape[1]
    tm, tf = 512, 512
    assert dff // tf >= 2
    row = lambda i, f: (i, 0)
    return pl.pallas_call(
        functools.partial(_ffn_kernel, final_norm=final_norm, tiles_per_seq=seq // tm,
                          norm_row=norm_row, ada_row=ada_row),
        grid=(m // tm, dff // tf),
        in_specs=[
            pl.BlockSpec((tm, d), row),
            pl.BlockSpec(vec.shape, lambda i, f: (0, 0, 0)),
            pl.BlockSpec((d, tf), lambda i, f: (0, f)),
            pl.BlockSpec((d, tf), lambda i, f: (0, f)),
            pl.BlockSpec((tf, d), lambda i, f: (f, 0)),
        ],
        out_specs=pl.BlockSpec((tm, d), row),
        out_shape=jax.ShapeDtypeStruct((m, d), F32),
        scratch_shapes=[pltpu.VMEM((tm, d), BF16), pltpu.VMEM((tm, d), F32),
                        pltpu.VMEM((tm, LANES), F32)],
        compiler_params=pltpu.CompilerParams(
            dimension_semantics=("arbitrary", "arbitrary"), vmem_limit_bytes=VMEM_LIMIT),
        name="ffn_final" if final_norm else "ffn",
    )(x2d, vec, wg, wu, wd)


def _rope128(x, cos, sin):
    return x * cos + pltpu.roll(x, 64, 1) * sin


def _rope64(x, cos, sin_lo, sin_hi):
    return x * cos + pltpu.roll(x, 96, 1) * sin_lo + pltpu.roll(x, 32, 1) * sin_hi


def _proj_kernel(x_ref, nw_ref, sh_ref, sc_ref, w_ref, wkv_ref,
                 cos_a_ref, sin_a_ref, cos_b_ref, sin_lo_ref, sin_hi_ref,
                 qa_ref, ka_ref, va_ref, qb_ref, kb_ref, vb_ref, ga_ref, gb_ref, h_scr, rinv_scr):
    n = pl.program_id(1)
    slot = pl.program_id(2)
    chunk = 2 * LANES
    identity = lambda x: x

    def project(dst_ref, epilogue, col0):
        h = h_scr[slot]
        for s in range(0, dst_ref.shape[1], chunk):
            acc = _dot(h, w_ref[:, col0 + s:col0 + s + chunk])
            for t in range(0, chunk, LANES):
                dst_ref[:, s + t:s + t + LANES] = epilogue(acc[:, t:t + LANES]).astype(BF16)

    rope_a = lambda x: _rope128(x, cos_a_ref[...], sin_a_ref[...])
    rope_b = lambda x: _rope64(x, cos_b_ref[...], sin_lo_ref[...], sin_hi_ref[...])

    @pl.when(n == 0)
    def _():
        _rms_modulate_to(h_scr.at[slot], x_ref, rinv_scr, nw_ref[...], sh_ref[...], sc_ref[...])
        project(qa_ref, rope_a, 0)
        project(ka_ref, rope_a, qa_ref.shape[1])

    @pl.when(n == 1)
    def _():
        project(va_ref, identity, 0)
        project(qb_ref, rope_b, va_ref.shape[1])
        kv = _dot(h_scr[slot], wkv_ref[...])
        kb_ref[...] = rope_b(kv[:, :LANES]).astype(BF16)
        vb_ref[...] = kv[:, LANES:].astype(BF16)

    @pl.when(n == 2)
    def _():
        project(ga_ref, identity, 0)

    @pl.when(n == 3)
    def _():
        project(gb_ref, identity, 0)


def _in_projection(x2d, norm_w, shift, scale, w_main, w_kv, tables, *, seq):
    m, d = x2d.shape
    tm, tn, half = 512, 2048, 1024
    n_tiles = w_main.shape[1] // tn
    assert n_tiles == 4 and d == tn
    tiles_per_seq = seq // tm
    group = 2
    assert (m // tm) % group == 0
    tile = lambda g, n, t: g * group + t
    per_batch = lambda g, n, t: (tile(g, n, t) // tiles_per_seq, 0, 0)
    x_rows = lambda g, n, t: (jnp.where(n == 0, tile(g, n, t), g * group + group - 1), 0)

    def out_rows(segment):
        def index(g, n, t):
            first, last = g * group, g * group + group - 1
            return (jnp.where(n == segment, tile(g, n, t), jnp.where(n < segment, first, last)), 0)
        return index

    def table_spec(segment):
        rows = out_rows(segment)
        return pl.BlockSpec((tm, LANES), lambda g, n, t: (rows(g, n, t)[0] % tiles_per_seq, 0))

    wide = lambda cols: jax.ShapeDtypeStruct((m, cols), BF16)
    const2 = lambda g, n, t: (0, 0)
    return pl.pallas_call(
        _proj_kernel,
        grid=(m // tm // group, n_tiles, group),
        in_specs=[
            pl.BlockSpec((tm, d), x_rows),
            pl.BlockSpec((1, d), const2),
            pl.BlockSpec((None, 1, d), per_batch),
            pl.BlockSpec((None, 1, d), per_batch),
            pl.BlockSpec((d, tn), lambda g, n, t: (0, n)),
            pl.BlockSpec((d, 2 * LANES), const2),
            table_spec(0), table_spec(0), table_spec(1), table_spec(1), table_spec(1),
        ],
        out_specs=[
            pl.BlockSpec((tm, half), out_rows(0)),
            pl.BlockSpec((tm, half), out_rows(0)),
            pl.BlockSpec((tm, half), out_rows(1)),
            pl.BlockSpec((tm, half), out_rows(1)),
            pl.BlockSpec((tm, LANES), out_rows(1)),
            pl.BlockSpec((tm, LANES), out_rows(1)),
            pl.BlockSpec((tm, tn), out_rows(2)),
            pl.BlockSpec((tm, tn), out_rows(3)),
        ],
        out_shape=[wide(half), wide(half), wide(half), wide(half), wide(LANES), wide(LANES),
                   wide(tn), wide(tn)],
        scratch_shapes=[pltpu.VMEM((group, tm, d), BF16), pltpu.VMEM((tm, LANES), F32)],
        compiler_params=pltpu.CompilerParams(
            dimension_semantics=("arbitrary", "arbitrary", "arbitrary"),
            vmem_limit_bytes=VMEM_LIMIT),
        name="in_projection",
    )(x2d, norm_w.reshape(1, d), shift, scale, w_main, w_kv, *tables)


def _rope_tables(seq):
    pos = np.arange(seq, dtype=np.float64)[:, None]

    def cos_sin(half):
        inv = ROPE_THETA ** (-np.arange(half, dtype=np.float64) / half)
        ang = pos * inv[None, :]
        return np.cos(ang), np.sin(ang)

    cos, sin = cos_sin(MOBA_HEAD_DIM // 2)
    cos_a = np.concatenate([cos, cos], axis=1)
    sin_a = np.concatenate([-sin, sin], axis=1)
    cos, sin = cos_sin(SWA_HEAD_DIM // 2)
    zero = np.zeros_like(sin)
    cos_b = np.concatenate([cos, cos, cos, cos], axis=1)
    sin_lo = np.concatenate([-sin, zero, -sin, zero], axis=1)
    sin_hi = np.concatenate([zero, sin, zero, sin], axis=1)
    return tuple(jnp.asarray(t, dtype=F32) for t in (cos_a, sin_a, cos_b, sin_lo, sin_hi))


def _moba_kernel(*refs, topk, group, n_side):
    q_ref, k_ref, v_ref = refs[:3]
    o_ref = refs[3 + n_side]
    kaug_scr, vt_scr, kmean_scr, s_scr = refs[4 + 2 * n_side:]
    _side_cast(refs[3:3 + n_side], refs[4 + n_side:4 + 2 * n_side])

    i = pl.program_id(2)
    blk = q_ref.shape[0]
    heads, nb, dh, _ = vt_scr.shape
    exp2_scale = dh ** -0.5 * LOG2_E
    head_lanes = lambda h: slice(h * dh, (h + 1) * dh)

    @pl.when(i == 0)
    def _():
        col = lax.broadcasted_iota(jnp.int32, (blk, LANES), 1)
        for h in range(heads):
            for j in range(nb):
                k = k_ref[j * blk:(j + 1) * blk, head_lanes(h)]
                kmean_scr[h, j:j + 1, :] = jnp.mean(k.astype(F32), axis=0, keepdims=True)
                kaug_scr[h, j, :, :dh] = k
                kaug_scr[h, j, :, dh:] = jnp.where(col == j, 1.0, 0.0).astype(BF16)
                v = v_ref[j * blk:(j + 1) * blk, head_lanes(h)]
                vt_scr[h, j] = v.astype(F32).T.astype(BF16)

    def gate(h):
        q = q_ref[:, head_lanes(h)]
        gs = _dot_nt(kmean_scr[h].astype(BF16), q)
        blk_id = lax.broadcasted_iota(jnp.int32, gs.shape, 0)
        valid = blk_id < i
        vals = jnp.where(valid, gs, -jnp.inf)
        sel = jnp.zeros(gs.shape, dtype=jnp.bool_)
        for _ in range(topk):
            best = jnp.max(vals, axis=0, keepdims=True)
            first = jnp.min(jnp.where(vals == best, blk_id, nb), axis=0, keepdims=True)
            pick = blk_id == first
            sel = sel | pick
            vals = jnp.where(pick, -jnp.inf, vals)
        sel = sel & valid
        bias_t = jnp.where(sel, 0.0, MASK_VALUE).astype(F32)
        bias_t = jnp.concatenate([bias_t, jnp.zeros((LANES - nb, blk), F32)], axis=0)
        q_t = q.astype(F32).T.astype(BF16)
        return jnp.concatenate([q_t, bias_t.astype(BF16)], axis=0)

    def attend(n_past):
        q_aug_t = [gate(h) for h in range(heads)]
        own_slot = n_past
        r = lax.broadcasted_iota(jnp.int32, (blk, blk), 0)
        c = lax.broadcasted_iota(jnp.int32, (blk, blk), 1)
        m = []
        for h in range(heads):
            s = _dot(kaug_scr[h, i, :, :dh], q_aug_t[h][:dh]) * exp2_scale
            s = jnp.where(r <= c, s, MASK_VALUE)
            s_scr[h, own_slot] = s
            mh = jnp.max(s, axis=0, keepdims=True)
            for j in range(n_past):
                s = _dot(kaug_scr[h, j], q_aug_t[h]) * exp2_scale
                s_scr[h, j] = s
                mh = jnp.maximum(mh, jnp.max(s, axis=0, keepdims=True))
            m.append(mh)

        for h in range(heads):
            l = jnp.zeros_like(m[h])
            acc = jnp.zeros((dh, blk), F32)
            for slot in range(n_past + 1):
                p = jnp.exp2(s_scr[h, slot] - m[h])
                l = l + jnp.sum(p, axis=0, keepdims=True)
                vt = vt_scr[h, i] if slot == own_slot else vt_scr[h, slot]
                acc = acc + _dot(vt, p.astype(BF16))
            o_ref[:, head_lanes(h)] = (acc / l).T.astype(o_ref.dtype)

    n_groups = (i + group - 1) // group
    for n in range(nb // group + 1):
        pl.when(n_groups == n)(functools.partial(attend, n * group))


def _moba_attention(q, k, v, side, *, heads):
    b, seq, _ = q.shape
    dh, blk = MOBA_HEAD_DIM, MOBA_BLOCK
    nb = seq // blk
    hps = MOBA_HEADS_PER_STEP
    grid = (b, heads // hps, nb)
    q_spec = pl.BlockSpec((None, blk, hps * dh), lambda bi, h, i: (bi, i, h))
    kv_spec = pl.BlockSpec((None, seq, hps * dh), lambda bi, h, i: (bi, 0, h))
    side_specs = _side_cast_specs(side, grid[0] * grid[1] * grid[2],
                                  lambda bi, h, i: (bi * grid[1] + h) * grid[2] + i)
    out, *side_bf16 = pl.pallas_call(
        functools.partial(_moba_kernel, topk=min(MOBA_TOPK, nb), group=MOBA_LOOP_GROUP,
                          n_side=len(side)),
        grid=grid,
        in_specs=[q_spec, kv_spec, kv_spec] + side_specs,
        out_specs=[q_spec] + side_specs,
        out_shape=[jax.ShapeDtypeStruct(q.shape, BF16)] + _bf16_like(side),
        scratch_shapes=[pltpu.VMEM((hps, nb, blk, dh + LANES), BF16),
                        pltpu.VMEM((hps, nb, dh, blk), BF16),
                        pltpu.VMEM((hps, nb, dh), F32),
                        pltpu.VMEM((hps, nb + 1, blk, blk), F32)],
        compiler_params=pltpu.CompilerParams(
            dimension_semantics=("arbitrary", "arbitrary", "arbitrary"),
            vmem_limit_bytes=VMEM_LIMIT),
        name="moba_attention",
    )(q, k, v, *side)
    return out, side_bf16


def _swa_kernel(*refs, q_lane_blocks, n_side):
    sink_ref, q_ref, kp_ref, kc_ref, vp_ref, vc_ref = refs[:6]
    o_ref = refs[6 + n_side]
    k_scr, vt_scr, s_scr, qt_scr, ot_scr = refs[7 + 2 * n_side:]
    _side_cast(refs[6:6 + n_side], refs[7 + n_side:7 + 2 * n_side])

    i = pl.program_id(1)
    w = kp_ref.shape[0]
    tq = q_ref.shape[0]
    sub_blocks = tq // w
    half = LANES // 2
    scale = half ** -0.5
    groups = q_lane_blocks // SWA_KV_HEADS
    lanes = lambda jb: slice(jb * LANES, (jb + 1) * LANES)
    zeros = jnp.zeros((half, 2 * w), BF16)

    k_scr[:w, :] = kp_ref[...]
    k_scr[w:, :] = kc_ref[...]
    transposed = lambda x: x.astype(F32).T.astype(BF16)
    vt_scr[0] = transposed(vp_ref[...])
    for c in range(sub_blocks):
        vt_scr[c + 1] = transposed(vc_ref[c * w:(c + 1) * w, :])
    for u in range(sub_blocks):
        for jb in range(q_lane_blocks):
            qt = (q_ref[u * w:(u + 1) * w, lanes(jb)].astype(F32) * scale).T.astype(BF16)
            q2 = jnp.concatenate([qt[:half], qt[half:]], axis=1)
            qt_scr[u * q_lane_blocks + jb] = jnp.concatenate(
                [q2, zeros] if jb < groups else [zeros, q2], axis=0)

    key = lax.broadcasted_iota(jnp.int32, (w, 2 * w), 0)
    col = lax.broadcasted_iota(jnp.int32, (w, 2 * w), 1)
    from_prev = key > jnp.where(col >= w, col - w, col)
    odd_head = lax.broadcasted_iota(jnp.int32, (1, 2 * w), 1) >= w

    def sub_block(u):
        prev_bias = jnp.where((i * sub_blocks + u) == 0, MASK_VALUE, 0.0)
        k2 = k_scr[u * w:(u + 2) * w, :]
        vt2 = jnp.concatenate([vt_scr[u], vt_scr[u + 1]], axis=1)

        sinks, maxima = [], []
        for jb in range(q_lane_blocks):
            s = _dot(k2, qt_scr[u * q_lane_blocks + jb])
            s = jnp.where(from_prev, s[:w] + prev_bias, s[w:])
            s_scr[u * q_lane_blocks + jb] = s
            sink = jnp.where(odd_head, sink_ref[2 * jb + 1], sink_ref[2 * jb])
            sinks.append(sink)
            maxima.append(jnp.maximum(jnp.max(s, axis=0, keepdims=True), sink))

        for jb in range(q_lane_blocks):
            kvh = jb // groups
            p = jnp.exp(s_scr[u * q_lane_blocks + jb] - maxima[jb])
            den = jnp.sum(p, axis=0, keepdims=True) + jnp.exp(sinks[jb] - maxima[jb])
            p2 = jnp.concatenate([jnp.where(from_prev, p, 0.0), jnp.where(from_prev, 0.0, p)],
                                 axis=0).astype(BF16)
            o = _dot(vt2[kvh * half:(kvh + 1) * half], p2) / den
            ot_scr[u * q_lane_blocks + jb] = jnp.concatenate([o[:, :w], o[:, w:]], axis=0)

    for u in range(sub_blocks):
        sub_block(u)

    for u in range(sub_blocks):
        for jb in range(q_lane_blocks):
            o_ref[u * w:(u + 1) * w, lanes(jb)] = ot_scr[u * q_lane_blocks + jb].T.astype(
                o_ref.dtype)


def _swa_attention(q, k, v, sinks, side):
    b, seq, qw = q.shape
    w = SWA_WINDOW
    tq = 512
    sub_blocks = tq // w
    grid = (b, seq // tq)
    q_spec = pl.BlockSpec((None, tq, qw), lambda bi, i: (bi, i, 0))
    cur = pl.BlockSpec((None, tq, LANES), lambda bi, i: (bi, i, 0))
    prev = pl.BlockSpec((None, w, LANES), lambda bi, i: (bi, jnp.maximum(i * sub_blocks - 1, 0), 0))
    side_specs = _side_cast_specs(side, grid[0] * grid[1], lambda bi, i: bi * grid[1] + i)
    out, *side_bf16 = pl.pallas_call(
        functools.partial(_swa_kernel, q_lane_blocks=qw // LANES, n_side=len(side)),
        grid=grid,
        in_specs=[pl.BlockSpec(memory_space=pltpu.SMEM), q_spec, prev, cur, prev, cur] + side_specs,
        out_specs=[q_spec] + side_specs,
        out_shape=[jax.ShapeDtypeStruct(q.shape, BF16)] + _bf16_like(side),
        scratch_shapes=[pltpu.VMEM((tq + w, LANES), BF16),
                        pltpu.VMEM((sub_blocks + 1, LANES, w), BF16),
                        pltpu.VMEM((sub_blocks * (qw // LANES), w, 2 * w), F32),
                        pltpu.VMEM((sub_blocks * (qw // LANES), LANES, 2 * w), BF16),
                        pltpu.VMEM((sub_blocks * (qw // LANES), LANES, w), F32)],
        compiler_params=pltpu.CompilerParams(
            dimension_semantics=("arbitrary", "arbitrary"), vmem_limit_bytes=VMEM_LIMIT),
        name="swa_attention",
    )(sinks, q, k, k, v, v, *side)
    return out, side_bf16


def _merge_kernel(ya_ref, yb_ref, ga_ref, gb_ref, x_ref, g_ref, wa_ref, wb_ref, wo_ref, o_ref):
    merged = (_sigmoid(ga_ref[...].astype(F32)) * _dot(ya_ref[...], wa_ref[...])
              + _sigmoid(gb_ref[...].astype(F32)) * _dot(yb_ref[...], wb_ref[...]))
    o_ref[...] = x_ref[...] + g_ref[...] * _dot(merged.astype(BF16), wo_ref[...])


def _merge(ya, yb, ga, gb, x2d, gate, wa, wb, wo, *, seq):
    m, d = x2d.shape
    tm = 512
    tiles_per_seq = seq // tm
    row = lambda i: (i, 0)
    const = lambda i: (0, 0)
    resident = lambda shape: pl.BlockSpec(shape, const, pipeline_mode=pl.Buffered(1))
    return pl.pallas_call(
        _merge_kernel,
        grid=(m // tm,),
        in_specs=[
            pl.BlockSpec((tm, ya.shape[1]), row),
            pl.BlockSpec((tm, yb.shape[1]), row),
            pl.BlockSpec((tm, d), row),
            pl.BlockSpec((tm, d), row),
            pl.BlockSpec((tm, d), row),
            pl.BlockSpec((None, 1, d), lambda i: (i // tiles_per_seq, 0, 0)),
            resident(wa.shape), resident(wb.shape), resident(wo.shape),
        ],
        out_specs=pl.BlockSpec((tm, d), row),
        out_shape=jax.ShapeDtypeStruct((m, d), F32),
        compiler_params=pltpu.CompilerParams(
            dimension_semantics=("arbitrary",), vmem_limit_bytes=VMEM_LIMIT),
        name="merge_out_projection",
    )(ya, yb, ga, gb, x2d, gate, wa, wb, wo)


def kernel(x, c, w_ada, b_ada, norm_ffn1, ffn1_gate, ffn1_up, ffn1_down, norm_mix, w_in, swa_sinks,
           w_branch_moba, w_branch_swa, w_out, norm_ffn2, ffn2_gate, ffn2_up, ffn2_down, norm_final):
    b, seq, d = x.shape
    depth = w_ada.shape[0]
    moba_w = w_branch_moba.shape[1]
    swa_qw = w_branch_swa.shape[1]
    kv_w = SWA_KV_HEADS * SWA_HEAD_DIM
    qkv_cols = 3 * moba_w + swa_qw
    assert w_in.shape[2] == qkv_cols + 2 * kv_w + 2 * d
    assert moba_w == swa_qw == 1024 and d == 2048 and kv_w == LANES

    tables = _rope_tables(seq)
    c_pad = jnp.pad(c, ((0, 8 - b), (0, 0)))
    x2d = x.reshape(b * seq, d)

    for l in range(depth):
        mod = _ada_modulation(c_pad, w_ada[l], b_ada[l])[:b].reshape(b, N_ADA, d)
        sh2, sc2, g2 = [mod[:, t:t + 1] for t in (3, 4, 5)]
        gains = jnp.stack([norm_ffn1[l], norm_mix[l], norm_ffn2[l], norm_final])
        vec = jnp.concatenate(
            [mod, jnp.broadcast_to(gains, (b,) + gains.shape),
             jnp.zeros((b, VEC_ROWS - N_ADA - gains.shape[0], d), F32)], axis=1)

        gate1, up1 = _cast_bf16([ffn1_gate[l], ffn1_up[l]], block_rows=256)
        down1, = _cast_bf16([ffn1_down[l]], block_rows=512)
        w_main, w_kv = _split_w_in(w_in[l], qkv_cols, 2 * kv_w)

        x2d = _ffn(x2d, vec, gate1, up1, down1, seq=seq, norm_row=VEC_NORM_FFN1_ROW, ada_row=0,
                   final_norm=False)

        qa, ka, va, qb, kb, vb, ga, gb = _in_projection(
            x2d, norm_mix[l], sh2, sc2, w_main, w_kv, tables, seq=seq)

        rs = lambda t: t.reshape(b, seq, t.shape[-1])
        ya, (gate2, up2, down2) = _moba_attention(
            rs(qa), rs(ka), rs(va), [ffn2_gate[l], ffn2_up[l], ffn2_down[l]],
            heads=moba_w // MOBA_HEAD_DIM)
        yb, (wa, wb, wo) = _swa_attention(
            rs(qb), rs(kb), rs(vb), swa_sinks[l], [w_branch_moba[l], w_branch_swa[l], w_out[l]])

        x2d = _merge(ya.reshape(b * seq, moba_w), yb.reshape(b * seq, swa_qw), ga, gb, x2d, g2,
                     wa, wb, wo, seq=seq)

        x2d = _ffn(x2d, vec, gate2, up2, down2, seq=seq, norm_row=VEC_NORM_FFN2_ROW, ada_row=6,
                   final_norm=(l == depth - 1))

    return x2d.reshape(b, seq, d)
```

```python
import functools

import jax
import jax.numpy as jnp
import numpy as np
from jax import lax
from jax.experimental import pallas as pl
from jax.experimental.pallas import tpu as pltpu

MOBA_HEAD_DIM = 128
MOBA_BLOCK = 256
MOBA_TOPK = 3
SWA_HEAD_DIM = 64
SWA_KV_HEADS = 2
SWA_WINDOW = 128
ROPE_THETA = 10000.0
EPS = 1e-6
N_ADA = 9
VEC_NORM_FFN1_ROW, VEC_NORM_MIX_ROW, VEC_NORM_FFN2_ROW, VEC_FINAL_NORM_ROW = 9, 10, 11, 12
VEC_ROWS = 16

LANES = 128
BF16_SUBLANES = 16
ROW_CHUNK = 16
VMEM_LIMIT = 56 * 1024 * 1024
MASK_VALUE = -1e30
LOG2_E = 1.4426950408889634
MOBA_LOOP_GROUP = 2
MOBA_HEADS_PER_STEP = 4

F32 = jnp.float32
BF16 = jnp.bfloat16


def _sigmoid(x):
    return 1.0 / (1.0 + jnp.exp(-x))


def _row_rsqrt_mean_square(x):
    r = lax.rsqrt(jnp.mean(x * x, axis=-1, keepdims=True) + EPS)
    return jnp.broadcast_to(r, (x.shape[0], LANES))


def _across_lanes(r, width):
    return jnp.concatenate([r] * (width // LANES), axis=1)


def _for_row_chunks(rows, body):
    for c in range(rows // ROW_CHUNK):
        body(pl.ds(c * ROW_CHUNK, ROW_CHUNK))


def _rms_modulate_to(h_ref, x_ref, rinv_scr, norm_w, shift, scale):
    rinv_scr[...] = _row_rsqrt_mean_square(x_ref[...])
    gain = norm_w * (1.0 + scale)

    def slab(rows):
        rinv = _across_lanes(rinv_scr[rows, :], x_ref.shape[1])
        h_ref[rows, :] = (x_ref[rows, :] * rinv * gain + shift).astype(BF16)

    _for_row_chunks(x_ref.shape[0], slab)


def _dot(a, b):
    return jnp.dot(a, b, preferred_element_type=F32)


def _dot_nt(a, b):
    return lax.dot_general(a, b, (((1,), (1,)), ((), ())), preferred_element_type=F32)


def _cast_kernel(*refs):
    n = len(refs) // 2
    for src, dst in zip(refs[:n], refs[n:]):
        dst[...] = src[...].astype(dst.dtype)


def _cast_bf16(arrays, block_rows):
    rows, cols = arrays[0].shape
    spec = pl.BlockSpec((block_rows, cols), lambda r: (r, 0))
    n = len(arrays)
    return pl.pallas_call(
        _cast_kernel,
        grid=(rows // block_rows,),
        in_specs=[spec] * n,
        out_specs=[spec] * n,
        out_shape=[jax.ShapeDtypeStruct((rows, cols), BF16)] * n,
        compiler_params=pltpu.CompilerParams(
            dimension_semantics=("arbitrary",), vmem_limit_bytes=VMEM_LIMIT),
        name="cast_bf16",
    )(*arrays)


def _side_cast_specs(arrays, steps, step_index):
    specs = []
    for a in arrays:
        rows, cols = a.shape
        for col_blocks in (1, 2, 4, 8, 16):
            row_blocks = steps // col_blocks
            if rows % (row_blocks * BF16_SUBLANES) == 0 and cols % (col_blocks * LANES) == 0:
                break
        else:
            raise ValueError(f"no {steps}-block tiling for {a.shape}")
        index = lambda *g, cb=col_blocks: (step_index(*g) // cb, step_index(*g) % cb)
        specs.append(pl.BlockSpec((rows // row_blocks, cols // col_blocks), index))
    return specs


def _side_cast(src_refs, dst_refs):
    for src, dst in zip(src_refs, dst_refs):
        dst[...] = src[...].astype(dst.dtype)


def _bf16_like(arrays):
    return [jax.ShapeDtypeStruct(a.shape, BF16) for a in arrays]


def _split_w_in_kernel(w_ref, main_ref, kv_ref, *, kv_block):
    j = pl.program_id(0)

    @pl.when(j != kv_block)
    def _():
        main_ref[...] = w_ref[...].astype(BF16)

    @pl.when(j == kv_block)
    def _():
        kv_ref[...] = w_ref[...].astype(BF16)


def _split_w_in(w, kv_start, kv_width):
    d, cols = w.shape
    assert kv_start % kv_width == 0 and cols % kv_width == 0
    kv_block = kv_start // kv_width
    main_col = lambda j: (0, jnp.where(j > kv_block, j - 1, jnp.minimum(j, kv_block - 1)))
    return pl.pallas_call(
        functools.partial(_split_w_in_kernel, kv_block=kv_block),
        grid=(cols // kv_width,),
        in_specs=[pl.BlockSpec((d, kv_width), lambda j: (0, j))],
        out_specs=[pl.BlockSpec((d, kv_width), main_col),
                   pl.BlockSpec((d, kv_width), lambda j: (0, 0))],
        out_shape=[jax.ShapeDtypeStruct((d, cols - kv_width), BF16),
                   jax.ShapeDtypeStruct((d, kv_width), BF16)],
        compiler_params=pltpu.CompilerParams(
            dimension_semantics=("arbitrary",), vmem_limit_bytes=VMEM_LIMIT),
        name="split_w_in",
    )(w)


def _ada_kernel(c_ref, w_ref, b_ref, o_ref):
    c = c_ref[...]
    s = (c * _sigmoid(c)).astype(BF16)
    o_ref[...] = _dot(s, w_ref[...].astype(BF16)) + b_ref[...]


def _ada_modulation(c_pad, w, b):
    rows, d = c_pad.shape
    n = w.shape[1]
    tn = 1024
    return pl.pallas_call(
        _ada_kernel,
        grid=(n // tn,),
        in_specs=[
            pl.BlockSpec((rows, d), lambda j: (0, 0)),
            pl.BlockSpec((d, tn), lambda j: (0, j)),
            pl.BlockSpec((1, tn), lambda j: (0, j)),
        ],
        out_specs=pl.BlockSpec((rows, tn), lambda j: (0, j)),
        out_shape=jax.ShapeDtypeStruct((rows, n), F32),
        compiler_params=pltpu.CompilerParams(
            dimension_semantics=("arbitrary",), vmem_limit_bytes=VMEM_LIMIT),
        name="ada_modulation",
    )(c_pad, w, b.reshape(1, n))


def _ffn_kernel(x_ref, vec_ref, wg_ref, wu_ref, wd_ref, o_ref, h_scr, acc_scr, rinv_scr, *,
                final_norm, tiles_per_seq, norm_row, ada_row):
    f = pl.program_id(1)
    nf = pl.num_programs(1)
    batch = pl.program_id(0) // tiles_per_seq
    vec = lambda r: vec_ref[batch, r:r + 1, :]

    def down_projection():
        h = h_scr[...]
        gate = _dot(h, wg_ref[...])
        up = _dot(h, wu_ref[...])
        act = (gate * _sigmoid(gate) * up).astype(BF16)
        return _dot(act, wd_ref[...])

    @pl.when(f == 0)
    def _():
        _rms_modulate_to(h_scr, x_ref, rinv_scr, vec(norm_row), vec(ada_row), vec(ada_row + 1))
        acc_scr[...] = down_projection()

    @pl.when((f > 0) & (f < nf - 1))
    def _():
        acc_scr[...] += down_projection()

    @pl.when(f == nf - 1)
    def _():
        y = x_ref[...] + (0.5 * vec(ada_row + 2)) * (acc_scr[...] + down_projection())
        o_ref[...] = y
        if final_norm:
            rinv_scr[...] = _row_rsqrt_mean_square(y)
            final_w = vec(VEC_FINAL_NORM_ROW)

            def slab(rows):
                rinv = _across_lanes(rinv_scr[rows, :], o_ref.shape[1])
                o_ref[rows, :] = o_ref[rows, :] * rinv * final_w

            _for_row_chunks(o_ref.shape[0], slab)


def _ffn(x2d, vec, wg, wu, wd, *, seq, norm_row, ada_row, final_norm):
    m, d = x2d.shape
    dff = wg.shape[1]
    tm, tf = 512, 512
    assert dff // tf >= 2
    row = lambda i, f: (i, 0)
    return pl.pallas_call(
        functools.partial(_ffn_kernel, final_norm=final_norm, tiles_per_seq=seq // tm,
                          norm_row=norm_row, ada_row=ada_row),
        grid=(m // tm, dff // tf),
        in_specs=[
            pl.BlockSpec((tm, d), row),
            pl.BlockSpec(vec.shape, lambda i, f: (0, 0, 0)),
            pl.BlockSpec((d, tf), lambda i, f: (0, f)),
            pl.BlockSpec((d, tf), lambda i, f: (0, f)),
            pl.BlockSpec((tf, d), lambda i, f: (f, 0)),
        ],
        out_specs=pl.BlockSpec((tm, d), row),
        out_shape=jax.ShapeDtypeStruct((m, d), F32),
        scratch_shapes=[pltpu.VMEM((tm, d), BF16), pltpu.VMEM((tm, d), F32),
                        pltpu.VMEM((tm, LANES), F32)],
        compiler_params=pltpu.CompilerParams(
            dimension_semantics=("arbitrary", "arbitrary"), vmem_limit_bytes=VMEM_LIMIT),
        name="ffn_final" if final_norm else "ffn",
    )(x2d, vec, wg, wu, wd)


def _rope128(x, cos, sin):
    return x * cos + pltpu.roll(x, 64, 1) * sin


def _rope64(x, cos, sin_lo, sin_hi):
    return x * cos + pltpu.roll(x, 96, 1) * sin_lo + pltpu.roll(x, 32, 1) * sin_hi


def _proj_kernel(*refs, n_side):
    (x_ref, nw_ref, sh_ref, sc_ref, w_ref, wkv_ref,
     cos_a_ref, sin_a_ref, cos_b_ref, sin_lo_ref, sin_hi_ref) = refs[:11]
    outs = refs[11 + n_side:19 + n_side]
    qa_ref, ka_ref, va_ref, qb_ref, kb_ref, vb_ref, ga_ref, gb_ref = outs
    h_scr, rinv_scr = refs[19 + 2 * n_side:]
    _side_cast(refs[11:11 + n_side], refs[19 + n_side:19 + 2 * n_side])

    n = pl.program_id(1)
    slot = pl.program_id(2)
    chunk = 2 * LANES
    identity = lambda x: x

    def project(dst_ref, epilogue, col0):
        h = h_scr[slot]
        for s in range(0, dst_ref.shape[1], chunk):
            acc = _dot(h, w_ref[:, col0 + s:col0 + s + chunk])
            for t in range(0, chunk, LANES):
                dst_ref[:, s + t:s + t + LANES] = epilogue(acc[:, t:t + LANES]).astype(BF16)

    rope_a = lambda x: _rope128(x, cos_a_ref[...], sin_a_ref[...])
    rope_b = lambda x: _rope64(x, cos_b_ref[...], sin_lo_ref[...], sin_hi_ref[...])

    @pl.when(n == 0)
    def _():
        _rms_modulate_to(h_scr.at[slot], x_ref, rinv_scr, nw_ref[...], sh_ref[...], sc_ref[...])
        project(qa_ref, rope_a, 0)
        project(ka_ref, rope_a, qa_ref.shape[1])

    @pl.when(n == 1)
    def _():
        project(va_ref, identity, 0)
        project(qb_ref, rope_b, va_ref.shape[1])
        kv = _dot(h_scr[slot], wkv_ref[...])
        kb_ref[...] = rope_b(kv[:, :LANES]).astype(BF16)
        vb_ref[...] = kv[:, LANES:].astype(BF16)

    @pl.when(n == 2)
    def _():
        project(ga_ref, identity, 0)

    @pl.when(n == 3)
    def _():
        project(gb_ref, identity, 0)


def _in_projection(x2d, norm_w, shift, scale, w_main, w_kv, tables, side, *, seq):
    m, d = x2d.shape
    tm, tn, half = 512, 2048, 1024
    n_tiles = w_main.shape[1] // tn
    assert n_tiles == 4 and d == tn
    tiles_per_seq = seq // tm
    group = 2
    assert (m // tm) % group == 0
    tile = lambda g, n, t: g * group + t
    per_batch = lambda g, n, t: (tile(g, n, t) // tiles_per_seq, 0, 0)
    x_rows = lambda g, n, t: (jnp.where(n == 0, tile(g, n, t), g * group + group - 1), 0)

    def out_rows(segment):
        def index(g, n, t):
            first, last = g * group, g * group + group - 1
            return (jnp.where(n == segment, tile(g, n, t), jnp.where(n < segment, first, last)), 0)
        return index

    def table_spec(segment):
        rows = out_rows(segment)
        return pl.BlockSpec((tm, LANES), lambda g, n, t: (rows(g, n, t)[0] % tiles_per_seq, 0))

    wide = lambda cols: jax.ShapeDtypeStruct((m, cols), BF16)
    const2 = lambda g, n, t: (0, 0)
    grid = (m // tm // group, n_tiles, group)
    side_specs = _side_cast_specs(side, grid[0] * grid[1] * grid[2],
                                  lambda g, n, t: (g * grid[1] + n) * grid[2] + t)
    results = pl.pallas_call(
        functools.partial(_proj_kernel, n_side=len(side)),
        grid=grid,
        in_specs=[
            pl.BlockSpec((tm, d), x_rows),
            pl.BlockSpec((1, d), const2),
            pl.BlockSpec((None, 1, d), per_batch),
            pl.BlockSpec((None, 1, d), per_batch),
            pl.BlockSpec((d, tn), lambda g, n, t: (0, n)),
            pl.BlockSpec((d, 2 * LANES), const2),
            table_spec(0), table_spec(0), table_spec(1), table_spec(1), table_spec(1),
        ] + side_specs,
        out_specs=[
            pl.BlockSpec((tm, half), out_rows(0)),
            pl.BlockSpec((tm, half), out_rows(0)),
            pl.BlockSpec((tm, half), out_rows(1)),
            pl.BlockSpec((tm, half), out_rows(1)),
            pl.BlockSpec((tm, LANES), out_rows(1)),
            pl.BlockSpec((tm, LANES), out_rows(1)),
            pl.BlockSpec((tm, tn), out_rows(2)),
            pl.BlockSpec((tm, tn), out_rows(3)),
        ] + side_specs,
        out_shape=[wide(half), wide(half), wide(half), wide(half), wide(LANES), wide(LANES),
                   wide(tn), wide(tn)] + _bf16_like(side),
        scratch_shapes=[pltpu.VMEM((group, tm, d), BF16), pltpu.VMEM((tm, LANES), F32)],
        compiler_params=pltpu.CompilerParams(
            dimension_semantics=("arbitrary", "arbitrary", "arbitrary"),
            vmem_limit_bytes=VMEM_LIMIT),
        name="in_projection",
    )(x2d, norm_w.reshape(1, d), shift, scale, w_main, w_kv, *tables, *side)
    return results[:8], results[8:]


def _rope_tables(seq):
    pos = np.arange(seq, dtype=np.float64)[:, None]

    def cos_sin(half):
        inv = ROPE_THETA ** (-np.arange(half, dtype=np.float64) / half)
        ang = pos * inv[None, :]
        return np.cos(ang), np.sin(ang)

    cos, sin = cos_sin(MOBA_HEAD_DIM // 2)
    cos_a = np.concatenate([cos, cos], axis=1)
    sin_a = np.concatenate([-sin, sin], axis=1)
    cos, sin = cos_sin(SWA_HEAD_DIM // 2)
    zero = np.zeros_like(sin)
    cos_b = np.concatenate([cos, cos, cos, cos], axis=1)
    sin_lo = np.concatenate([-sin, zero, -sin, zero], axis=1)
    sin_hi = np.concatenate([zero, sin, zero, sin], axis=1)
    return tuple(jnp.asarray(t, dtype=F32) for t in (cos_a, sin_a, cos_b, sin_lo, sin_hi))


def _moba_kernel(*refs, topk, group, n_side):
    q_ref, k_ref, v_ref = refs[:3]
    o_ref = refs[3 + n_side]
    kaug_scr, vt_scr, kmean_scr, s_scr = refs[4 + 2 * n_side:]
    _side_cast(refs[3:3 + n_side], refs[4 + n_side:4 + 2 * n_side])

    i = pl.program_id(2)
    blk = q_ref.shape[0]
    heads, nb, dh, _ = vt_scr.shape
    exp2_scale = dh ** -0.5 * LOG2_E
    head_lanes = lambda h: slice(h * dh, (h + 1) * dh)

    @pl.when(i == 0)
    def _():
        col = lax.broadcasted_iota(jnp.int32, (blk, LANES), 1)
        for h in range(heads):
            for j in range(nb):
                k = k_ref[j * blk:(j + 1) * blk, head_lanes(h)]
                kmean_scr[h, j:j + 1, :] = jnp.mean(k.astype(F32), axis=0, keepdims=True)
                kaug_scr[h, j, :, :dh] = k
                kaug_scr[h, j, :, dh:] = jnp.where(col == j, 1.0, 0.0).astype(BF16)
                v = v_ref[j * blk:(j + 1) * blk, head_lanes(h)]
                vt_scr[h, j] = v.astype(F32).T.astype(BF16)

    def gate(h):
        q = q_ref[:, head_lanes(h)]
        gs = _dot_nt(kmean_scr[h].astype(BF16), q)
        blk_id = lax.broadcasted_iota(jnp.int32, gs.shape, 0)
        valid = blk_id < i
        vals = jnp.where(valid, gs, -jnp.inf)
        sel = jnp.zeros(gs.shape, dtype=jnp.bool_)
        for _ in range(topk):
            best = jnp.max(vals, axis=0, keepdims=True)
            first = jnp.min(jnp.where(vals == best, blk_id, nb), axis=0, keepdims=True)
            pick = blk_id == first
            sel = sel | pick
            vals = jnp.where(pick, -jnp.inf, vals)
        sel = sel & valid
        bias_t = jnp.where(sel, 0.0, MASK_VALUE).astype(F32)
        bias_t = jnp.concatenate([bias_t, jnp.zeros((LANES - nb, blk), F32)], axis=0)
        q_t = q.astype(F32).T.astype(BF16)
        return jnp.concatenate([q_t, bias_t.astype(BF16)], axis=0)

    def attend(n_past):
        q_aug_t = [gate(h) for h in range(heads)]
        own_slot = n_past
        r = lax.broadcasted_iota(jnp.int32, (blk, blk), 0)
        c = lax.broadcasted_iota(jnp.int32, (blk, blk), 1)
        m = []
        for h in range(heads):
            s = _dot(kaug_scr[h, i, :, :dh], q_aug_t[h][:dh]) * exp2_scale
            s = jnp.where(r <= c, s, MASK_VALUE)
            s_scr[h, own_slot] = s
            mh = jnp.max(s, axis=0, keepdims=True)
            for j in range(n_past):
                s = _dot(kaug_scr[h, j], q_aug_t[h]) * exp2_scale
                s_scr[h, j] = s
                mh = jnp.maximum(mh, jnp.max(s, axis=0, keepdims=True))
            m.append(mh)

        for h in range(heads):
            l = jnp.zeros_like(m[h])
            acc = jnp.zeros((dh, blk), F32)
            for slot in range(n_past + 1):
                p = jnp.exp2(s_scr[h, slot] - m[h])
                l = l + jnp.sum(p, axis=0, keepdims=True)
                vt = vt_scr[h, i] if slot == own_slot else vt_scr[h, slot]
                acc = acc + _dot(vt, p.astype(BF16))
            o_ref[:, head_lanes(h)] = (acc / l).T.astype(o_ref.dtype)

    n_groups = (i + group - 1) // group
    for n in range(nb // group + 1):
        pl.when(n_groups == n)(functools.partial(attend, n * group))


def _moba_attention(q, k, v, side, *, heads):
    b, seq, _ = q.shape
    dh, blk = MOBA_HEAD_DIM, MOBA_BLOCK
    nb = seq // blk
    hps = MOBA_HEADS_PER_STEP
    grid = (b, heads // hps, nb)
    q_spec = pl.BlockSpec((None, blk, hps * dh), lambda bi, h, i: (bi, i, h))
    kv_spec = pl.BlockSpec((None, seq, hps * dh), lambda bi, h, i: (bi, 0, h))
    side_specs = _side_cast_specs(side, grid[0] * grid[1] * grid[2],
                                  lambda bi, h, i: (bi * grid[1] + h) * grid[2] + i)
    out, *side_bf16 = pl.pallas_call(
        functools.partial(_moba_kernel, topk=min(MOBA_TOPK, nb), group=MOBA_LOOP_GROUP,
                          n_side=len(side)),
        grid=grid,
        in_specs=[q_spec, kv_spec, kv_spec] + side_specs,
        out_specs=[q_spec] + side_specs,
        out_shape=[jax.ShapeDtypeStruct(q.shape, BF16)] + _bf16_like(side),
        scratch_shapes=[pltpu.VMEM((hps, nb, blk, dh + LANES), BF16),
                        pltpu.VMEM((hps, nb, dh, blk), BF16),
                        pltpu.VMEM((hps, nb, dh), F32),
                        pltpu.VMEM((hps, nb + 1, blk, blk), F32)],
        compiler_params=pltpu.CompilerParams(
            dimension_semantics=("arbitrary", "arbitrary", "arbitrary"),
            vmem_limit_bytes=VMEM_LIMIT),
        name="moba_attention",
    )(q, k, v, *side)
    return out, side_bf16


def _swa_kernel(*refs, q_lane_blocks, n_side):
    sink_ref, q_ref, kp_ref, kc_ref, vp_ref, vc_ref = refs[:6]
    o_ref = refs[6 + n_side]
    k_scr, vt_scr, s_scr, qt_scr, ot_scr = refs[7 + 2 * n_side:]
    _side_cast(refs[6:6 + n_side], refs[7 + n_side:7 + 2 * n_side])

    i = pl.program_id(1)
    w = kp_ref.shape[0]
    tq = q_ref.shape[0]
    sub_blocks = tq // w
    half = LANES // 2
    scale = half ** -0.5
    groups = q_lane_blocks // SWA_KV_HEADS
    lanes = lambda jb: slice(jb * LANES, (jb + 1) * LANES)
    zeros = jnp.zeros((half, 2 * w), BF16)

    k_scr[:w, :] = kp_ref[...]
    k_scr[w:, :] = kc_ref[...]
    transposed = lambda x: x.astype(F32).T.astype(BF16)
    vt_scr[0] = transposed(vp_ref[...])
    for c in range(sub_blocks):
        vt_scr[c + 1] = transposed(vc_ref[c * w:(c + 1) * w, :])
    for u in range(sub_blocks):
        for jb in range(q_lane_blocks):
            qt = (q_ref[u * w:(u + 1) * w, lanes(jb)].astype(F32) * scale).T.astype(BF16)
            q2 = jnp.concatenate([qt[:half], qt[half:]], axis=1)
            qt_scr[u * q_lane_blocks + jb] = jnp.concatenate(
                [q2, zeros] if jb < groups else [zeros, q2], axis=0)

    key = lax.broadcasted_iota(jnp.int32, (w, 2 * w), 0)
    col = lax.broadcasted_iota(jnp.int32, (w, 2 * w), 1)
    from_prev = key > jnp.where(col >= w, col - w, col)
    odd_head = lax.broadcasted_iota(jnp.int32, (1, 2 * w), 1) >= w

    def sub_block(u):
        prev_bias = jnp.where((i * sub_blocks + u) == 0, MASK_VALUE, 0.0)
        k2 = k_scr[u * w:(u + 2) * w, :]
        vt2 = jnp.concatenate([vt_scr[u], vt_scr[u + 1]], axis=1)

        sinks, maxima = [], []
        for jb in range(q_lane_blocks):
            s = _dot(k2, qt_scr[u * q_lane_blocks + jb])
            s = jnp.where(from_prev, s[:w] + prev_bias, s[w:])
            s_scr[u * q_lane_blocks + jb] = s
            sink = jnp.where(odd_head, sink_ref[2 * jb + 1], sink_ref[2 * jb])
            sinks.append(sink)
            maxima.append(jnp.maximum(jnp.max(s, axis=0, keepdims=True), sink))

        for jb in range(q_lane_blocks):
            kvh = jb // groups
            p = jnp.exp(s_scr[u * q_lane_blocks + jb] - maxima[jb])
            den = jnp.sum(p, axis=0, keepdims=True) + jnp.exp(sinks[jb] - maxima[jb])
            p2 = jnp.concatenate([jnp.where(from_prev, p, 0.0), jnp.where(from_prev, 0.0, p)],
                                 axis=0).astype(BF16)
            o = _dot(vt2[kvh * half:(kvh + 1) * half], p2) / den
            ot_scr[u * q_lane_blocks + jb] = jnp.concatenate([o[:, :w], o[:, w:]], axis=0)

    for u in range(sub_blocks):
        sub_block(u)

    for u in range(sub_blocks):
        for jb in range(q_lane_blocks):
            o_ref[u * w:(u + 1) * w, lanes(jb)] = ot_scr[u * q_lane_blocks + jb].T.astype(
                o_ref.dtype)


def _swa_attention(q, k, v, sinks, side):
    b, seq, qw = q.shape
    w = SWA_WINDOW
    tq = 512
    sub_blocks = tq // w
    grid = (b, seq // tq)
    q_spec = pl.BlockSpec((None, tq, qw), lambda bi, i: (bi, i, 0))
    cur = pl.BlockSpec((None, tq, LANES), lambda bi, i: (bi, i, 0))
    prev = pl.BlockSpec((None, w, LANES), lambda bi, i: (bi, jnp.maximum(i * sub_blocks - 1, 0), 0))
    side_specs = _side_cast_specs(side, grid[0] * grid[1], lambda bi, i: bi * grid[1] + i)
    out, *side_bf16 = pl.pallas_call(
        functools.partial(_swa_kernel, q_lane_blocks=qw // LANES, n_side=len(side)),
        grid=grid,
        in_specs=[pl.BlockSpec(memory_space=pltpu.SMEM), q_spec, prev, cur, prev, cur] + side_specs,
        out_specs=[q_spec] + side_specs,
        out_shape=[jax.ShapeDtypeStruct(q.shape, BF16)] + _bf16_like(side),
        scratch_shapes=[pltpu.VMEM((tq + w, LANES), BF16),
                        pltpu.VMEM((sub_blocks + 1, LANES, w), BF16),
                        pltpu.VMEM((sub_blocks * (qw // LANES), w, 2 * w), F32),
                        pltpu.VMEM((sub_blocks * (qw // LANES), LANES, 2 * w), BF16),
                        pltpu.VMEM((sub_blocks * (qw // LANES), LANES, w), F32)],
        compiler_params=pltpu.CompilerParams(
            dimension_semantics=("arbitrary", "arbitrary"), vmem_limit_bytes=VMEM_LIMIT),
        name="swa_attention",
    )(sinks, q, k, k, v, v, *side)
    return out, side_bf16


def _merge_kernel(ya_ref, yb_ref, ga_ref, gb_ref, x_ref, g_ref, wa_ref, wb_ref, wo_ref, o_ref):
    merged = (_sigmoid(ga_ref[...].astype(F32)) * _dot(ya_ref[...], wa_ref[...])
              + _sigmoid(gb_ref[...].astype(F32)) * _dot(yb_ref[...], wb_ref[...]))
    o_ref[...] = x_ref[...] + g_ref[...] * _dot(merged.astype(BF16), wo_ref[...])


def _merge(ya, yb, ga, gb, x2d, gate, wa, wb, wo, *, seq):
    m, d = x2d.shape
    tm = 512
    tiles_per_seq = seq // tm
    row = lambda i: (i, 0)
    const = lambda i: (0, 0)
    resident = lambda shape: pl.BlockSpec(shape, const, pipeline_mode=pl.Buffered(1))
    return pl.pallas_call(
        _merge_kernel,
        grid=(m // tm,),
        in_specs=[
            pl.BlockSpec((tm, ya.shape[1]), row),
            pl.BlockSpec((tm, yb.shape[1]), row),
            pl.BlockSpec((tm, d), row),
            pl.BlockSpec((tm, d), row),
            pl.BlockSpec((tm, d), row),
            pl.BlockSpec((None, 1, d), lambda i: (i // tiles_per_seq, 0, 0)),
            resident(wa.shape), resident(wb.shape), resident(wo.shape),
        ],
        out_specs=pl.BlockSpec((tm, d), row),
        out_shape=jax.ShapeDtypeStruct((m, d), F32),
        compiler_params=pltpu.CompilerParams(
            dimension_semantics=("arbitrary",), vmem_limit_bytes=VMEM_LIMIT),
        name="merge_out_projection",
    )(ya, yb, ga, gb, x2d, gate, wa, wb, wo)


def kernel(x, c, w_ada, b_ada, norm_ffn1, ffn1_gate, ffn1_up, ffn1_down, norm_mix, w_in, swa_sinks,
           w_branch_moba, w_branch_swa, w_out, norm_ffn2, ffn2_gate, ffn2_up, ffn2_down, norm_final):
    b, seq, d = x.shape
    depth = w_ada.shape[0]
    moba_w = w_branch_moba.shape[1]
    swa_qw = w_branch_swa.shape[1]
    kv_w = SWA_KV_HEADS * SWA_HEAD_DIM
    qkv_cols = 3 * moba_w + swa_qw
    assert w_in.shape[2] == qkv_cols + 2 * kv_w + 2 * d
    assert moba_w == swa_qw == 1024 and d == 2048 and kv_w == LANES

    tables = _rope_tables(seq)
    c_pad = jnp.pad(c, ((0, 8 - b), (0, 0)))
    x2d = x.reshape(b * seq, d)

    for l in range(depth):
        mod = _ada_modulation(c_pad, w_ada[l], b_ada[l])[:b].reshape(b, N_ADA, d)
        sh2, sc2, g2 = [mod[:, t:t + 1] for t in (3, 4, 5)]
        gains = jnp.stack([norm_ffn1[l], norm_mix[l], norm_ffn2[l], norm_final])
        vec = jnp.concatenate(
            [mod, jnp.broadcast_to(gains, (b,) + gains.shape),
             jnp.zeros((b, VEC_ROWS - N_ADA - gains.shape[0], d), F32)], axis=1)

        gate1, up1 = _cast_bf16([ffn1_gate[l], ffn1_up[l]], block_rows=256)
        down1, = _cast_bf16([ffn1_down[l]], block_rows=512)
        w_main, w_kv = _split_w_in(w_in[l], qkv_cols, 2 * kv_w)

        x2d = _ffn(x2d, vec, gate1, up1, down1, seq=seq, norm_row=VEC_NORM_FFN1_ROW, ada_row=0,
                   final_norm=False)

        (qa, ka, va, qb, kb, vb, ga, gb), (gate2, up2, down2) = _in_projection(
            x2d, norm_mix[l], sh2, sc2, w_main, w_kv, tables,
            [ffn2_gate[l], ffn2_up[l], ffn2_down[l]], seq=seq)

        rs = lambda t: t.reshape(b, seq, t.shape[-1])
        ya, _ = _moba_attention(rs(qa), rs(ka), rs(va), [], heads=moba_w // MOBA_HEAD_DIM)
        yb, (wa, wb, wo) = _swa_attention(
            rs(qb), rs(kb), rs(vb), swa_sinks[l], [w_branch_moba[l], w_branch_swa[l], w_out[l]])

        x2d = _merge(ya.reshape(b * seq, moba_w), yb.reshape(b * seq, swa_qw), ga, gb, x2d, g2,
                     wa, wb, wo, seq=seq)

        x2d = _ffn(x2d, vec, gate2, up2, down2, seq=seq, norm_row=VEC_NORM_FFN2_ROW, ada_row=6,
                   final_norm=(l == depth - 1))

    return x2d.reshape(b, seq, d)
```

```python
import functools

import jax
import jax.numpy as jnp
import numpy as np
from jax import lax
from jax.experimental import pallas as pl
from jax.experimental.pallas import tpu as pltpu

MOBA_HEAD_DIM = 128
MOBA_BLOCK = 256
MOBA_TOPK = 3
SWA_HEAD_DIM = 64
SWA_KV_HEADS = 2
SWA_WINDOW = 128
ROPE_THETA = 10000.0
EPS = 1e-6
N_ADA = 9
VEC_NORM_FFN1_ROW, VEC_NORM_MIX_ROW, VEC_NORM_FFN2_ROW, VEC_FINAL_NORM_ROW = 9, 10, 11, 12
VEC_ROWS = 16

LANES = 128
BF16_SUBLANES = 16
ROW_CHUNK = 16
VMEM_LIMIT = 56 * 1024 * 1024
MASK_VALUE = -1e30
LOG2_E = 1.4426950408889634
MOBA_LOOP_GROUP = 2
MOBA_HEADS_PER_STEP = 4

F32 = jnp.float32
BF16 = jnp.bfloat16


def _sigmoid(x):
    return 1.0 / (1.0 + jnp.exp(-x))


def _row_rsqrt_mean_square(x):
    r = lax.rsqrt(jnp.mean(x * x, axis=-1, keepdims=True) + EPS)
    return jnp.broadcast_to(r, (x.shape[0], LANES))


def _across_lanes(r, width):
    return jnp.concatenate([r] * (width // LANES), axis=1)


def _for_row_chunks(rows, body):
    for c in range(rows // ROW_CHUNK):
        body(pl.ds(c * ROW_CHUNK, ROW_CHUNK))


def _rms_modulate_to(h_ref, x_ref, rinv_scr, norm_w, shift, scale):
    rinv_scr[...] = _row_rsqrt_mean_square(x_ref[...])
    gain = norm_w * (1.0 + scale)

    def slab(rows):
        rinv = _across_lanes(rinv_scr[rows, :], x_ref.shape[1])
        h_ref[rows, :] = (x_ref[rows, :] * rinv * gain + shift).astype(BF16)

    _for_row_chunks(x_ref.shape[0], slab)


def _dot(a, b):
    return jnp.dot(a, b, preferred_element_type=F32)


def _dot_nt(a, b):
    return lax.dot_general(a, b, (((1,), (1,)), ((), ())), preferred_element_type=F32)


def _cast_kernel(*refs):
    n = len(refs) // 2
    for src, dst in zip(refs[:n], refs[n:]):
        dst[...] = src[...].astype(dst.dtype)


def _cast_bf16(arrays, block_rows):
    rows, cols = arrays[0].shape
    spec = pl.BlockSpec((block_rows, cols), lambda r: (r, 0))
    n = len(arrays)
    return pl.pallas_call(
        _cast_kernel,
        grid=(rows // block_rows,),
        in_specs=[spec] * n,
        out_specs=[spec] * n,
        out_shape=[jax.ShapeDtypeStruct((rows, cols), BF16)] * n,
        compiler_params=pltpu.CompilerParams(
            dimension_semantics=("arbitrary",), vmem_limit_bytes=VMEM_LIMIT),
        name="cast_bf16",
    )(*arrays)


def _side_cast_specs(arrays, steps, step_index):
    specs = []
    for a in arrays:
        rows, cols = a.shape
        for col_blocks in (1, 2, 4, 8, 16):
            row_blocks = steps // col_blocks
            if rows % (row_blocks * BF16_SUBLANES) == 0 and cols % (col_blocks * LANES) == 0:
                break
        else:
            raise ValueError(f"no {steps}-block tiling for {a.shape}")
        index = lambda *g, cb=col_blocks: (step_index(*g) // cb, step_index(*g) % cb)
        specs.append(pl.BlockSpec((rows // row_blocks, cols // col_blocks), index))
    return specs


def _side_cast(src_refs, dst_refs):
    for src, dst in zip(src_refs, dst_refs):
        dst[...] = src[...].astype(dst.dtype)


def _bf16_like(arrays):
    return [jax.ShapeDtypeStruct(a.shape, BF16) for a in arrays]


def _ada_kernel(c_ref, w_ref, b_ref, o_ref):
    c = c_ref[...]
    s = (c * _sigmoid(c)).astype(BF16)
    o_ref[...] = _dot(s, w_ref[...].astype(BF16)) + b_ref[...]


def _ada_modulation(c_pad, w, b):
    rows, d = c_pad.shape
    n = w.shape[1]
    tn = 1024
    return pl.pallas_call(
        _ada_kernel,
        grid=(n // tn,),
        in_specs=[
            pl.BlockSpec((rows, d), lambda j: (0, 0)),
            pl.BlockSpec((d, tn), lambda j: (0, j)),
            pl.BlockSpec((1, tn), lambda j: (0, j)),
        ],
        out_specs=pl.BlockSpec((rows, tn), lambda j: (0, j)),
        out_shape=jax.ShapeDtypeStruct((rows, n), F32),
        compiler_params=pltpu.CompilerParams(
            dimension_semantics=("arbitrary",), vmem_limit_bytes=VMEM_LIMIT),
        name="ada_modulation",
    )(c_pad, w, b.reshape(1, n))


def _ffn_kernel(*refs, final_norm, tiles_per_seq, norm_row, ada_row, split_at):
    if split_at is None:
        x_ref, vec_ref, wg_ref, wu_ref, wd_ref, o_ref, h_scr, acc_scr, rinv_scr = refs
    else:
        (x_ref, vec_ref, wg_ref, wu_ref, wd_ref, side_ref,
         o_ref, main_ref, cut_ref, h_scr, acc_scr, rinv_scr) = refs
    f = pl.program_id(1)
    nf = pl.num_programs(1)
    batch = pl.program_id(0) // tiles_per_seq
    vec = lambda r: vec_ref[batch, r:r + 1, :]

    def down_projection():
        h = h_scr[...]
        gate = _dot(h, wg_ref[...])
        up = _dot(h, wu_ref[...])
        act = (gate * _sigmoid(gate) * up).astype(BF16)
        return _dot(act, wd_ref[...])

    @pl.when(f == 0)
    def _():
        _rms_modulate_to(h_scr, x_ref, rinv_scr, vec(norm_row), vec(ada_row), vec(ada_row + 1))
        acc_scr[...] = down_projection()
        if split_at is not None:
            start, width = split_at
            main_ref[:, :start] = side_ref[:, :start].astype(BF16)
            main_ref[:, start:] = side_ref[:, start + width:].astype(BF16)
            cut_ref[...] = side_ref[:, start:start + width].astype(BF16)

    @pl.when((f > 0) & (f < nf - 1))
    def _():
        acc_scr[...] += down_projection()

    @pl.when(f == nf - 1)
    def _():
        y = x_ref[...] + (0.5 * vec(ada_row + 2)) * (acc_scr[...] + down_projection())
        o_ref[...] = y
        if final_norm:
            rinv_scr[...] = _row_rsqrt_mean_square(y)
            final_w = vec(VEC_FINAL_NORM_ROW)

            def slab(rows):
                rinv = _across_lanes(rinv_scr[rows, :], o_ref.shape[1])
                o_ref[rows, :] = o_ref[rows, :] * rinv * final_w

            _for_row_chunks(o_ref.shape[0], slab)


def _ffn(x2d, vec, wg, wu, wd, *, seq, norm_row, ada_row, final_norm, split=None):
    m, d = x2d.shape
    dff = wg.shape[1]
    tm, tf = 512, 512
    assert dff // tf >= 2
    row = lambda i, f: (i, 0)
    in_specs = [
        pl.BlockSpec((tm, d), row),
        pl.BlockSpec(vec.shape, lambda i, f: (0, 0, 0)),
        pl.BlockSpec((d, tf), lambda i, f: (0, f)),
        pl.BlockSpec((d, tf), lambda i, f: (0, f)),
        pl.BlockSpec((tf, d), lambda i, f: (f, 0)),
    ]
    out_specs = [pl.BlockSpec((tm, d), row)]
    out_shape = [jax.ShapeDtypeStruct((m, d), F32)]
    operands = [x2d, vec, wg, wu, wd]
    if split is not None:
        w, start, width = split
        rows, cols = w.shape
        block_rows = rows // (m // tm)
        assert block_rows % BF16_SUBLANES == 0 and start % LANES == 0 and width % LANES == 0
        in_specs.append(pl.BlockSpec((block_rows, cols), row))
        out_specs += [pl.BlockSpec((block_rows, cols - width), row),
                      pl.BlockSpec((block_rows, width), row)]
        out_shape += [jax.ShapeDtypeStruct((rows, cols - width), BF16),
                      jax.ShapeDtypeStruct((rows, width), BF16)]
        operands.append(w)
    results = pl.pallas_call(
        functools.partial(_ffn_kernel, final_norm=final_norm, tiles_per_seq=seq // tm,
                          norm_row=norm_row, ada_row=ada_row,
                          split_at=None if split is None else split[1:]),
        grid=(m // tm, dff // tf),
        in_specs=in_specs,
        out_specs=out_specs,
        out_shape=out_shape,
        scratch_shapes=[pltpu.VMEM((tm, d), BF16), pltpu.VMEM((tm, d), F32),
                        pltpu.VMEM((tm, LANES), F32)],
        compiler_params=pltpu.CompilerParams(
            dimension_semantics=("arbitrary", "arbitrary"), vmem_limit_bytes=VMEM_LIMIT),
        name="ffn_final" if final_norm else "ffn",
    )(*operands)
    return results[0] if split is None else tuple(results)


def _rope128(x, cos, sin):
    return x * cos + pltpu.roll(x, 64, 1) * sin


def _rope64(x, cos, sin_lo, sin_hi):
    return x * cos + pltpu.roll(x, 96, 1) * sin_lo + pltpu.roll(x, 32, 1) * sin_hi


def _proj_kernel(*refs, n_side):
    (x_ref, nw_ref, sh_ref, sc_ref, w_ref, wkv_ref,
     cos_a_ref, sin_a_ref, cos_b_ref, sin_lo_ref, sin_hi_ref) = refs[:11]
    outs = refs[11 + n_side:19 + n_side]
    qa_ref, ka_ref, va_ref, qb_ref, kb_ref, vb_ref, ga_ref, gb_ref = outs
    h_scr, rinv_scr = refs[19 + 2 * n_side:]
    _side_cast(refs[11:11 + n_side], refs[19 + n_side:19 + 2 * n_side])

    n = pl.program_id(1)
    slot = pl.program_id(2)
    chunk = 2 * LANES
    identity = lambda x: x

    def project(dst_ref, epilogue, col0):
        h = h_scr[slot]
        for s in range(0, dst_ref.shape[1], chunk):
            acc = _dot(h, w_ref[:, col0 + s:col0 + s + chunk])
            for t in range(0, chunk, LANES):
                dst_ref[:, s + t:s + t + LANES] = epilogue(acc[:, t:t + LANES]).astype(BF16)

    rope_a = lambda x: _rope128(x, cos_a_ref[...], sin_a_ref[...])
    rope_b = lambda x: _rope64(x, cos_b_ref[...], sin_lo_ref[...], sin_hi_ref[...])

    @pl.when(n == 0)
    def _():
        _rms_modulate_to(h_scr.at[slot], x_ref, rinv_scr, nw_ref[...], sh_ref[...], sc_ref[...])
        project(qa_ref, rope_a, 0)
        project(ka_ref, rope_a, qa_ref.shape[1])

    @pl.when(n == 1)
    def _():
        project(va_ref, identity, 0)
        project(qb_ref, rope_b, va_ref.shape[1])
        kv = _dot(h_scr[slot], wkv_ref[...])
        kb_ref[...] = rope_b(kv[:, :LANES]).astype(BF16)
        vb_ref[...] = kv[:, LANES:].astype(BF16)

    @pl.when(n == 2)
    def _():
        project(ga_ref, identity, 0)

    @pl.when(n == 3)
    def _():
        project(gb_ref, identity, 0)


def _in_projection(x2d, norm_w, shift, scale, w_main, w_kv, tables, side, *, seq):
    m, d = x2d.shape
    tm, tn, half = 512, 2048, 1024
    n_tiles = w_main.shape[1] // tn
    assert n_tiles == 4 and d == tn
    tiles_per_seq = seq // tm
    group = 2
    assert (m // tm) % group == 0
    tile = lambda g, n, t: g * group + t
    per_batch = lambda g, n, t: (tile(g, n, t) // tiles_per_seq, 0, 0)
    x_rows = lambda g, n, t: (jnp.where(n == 0, tile(g, n, t), g * group + group - 1), 0)

    def out_rows(segment):
        def index(g, n, t):
            first, last = g * group, g * group + group - 1
            return (jnp.where(n == segment, tile(g, n, t), jnp.where(n < segment, first, last)), 0)
        return index

    def table_spec(segment):
        rows = out_rows(segment)
        return pl.BlockSpec((tm, LANES), lambda g, n, t: (rows(g, n, t)[0] % tiles_per_seq, 0))

    wide = lambda cols: jax.ShapeDtypeStruct((m, cols), BF16)
    const2 = lambda g, n, t: (0, 0)
    grid = (m // tm // group, n_tiles, group)
    side_specs = _side_cast_specs(side, grid[0] * grid[1] * grid[2],
                                  lambda g, n, t: (g * grid[1] + n) * grid[2] + t)
    results = pl.pallas_call(
        functools.partial(_proj_kernel, n_side=len(side)),
        grid=grid,
        in_specs=[
            pl.BlockSpec((tm, d), x_rows),
            pl.BlockSpec((1, d), const2),
            pl.BlockSpec((None, 1, d), per_batch),
            pl.BlockSpec((None, 1, d), per_batch),
            pl.BlockSpec((d, tn), lambda g, n, t: (0, n)),
            pl.BlockSpec((d, 2 * LANES), const2),
            table_spec(0), table_spec(0), table_spec(1), table_spec(1), table_spec(1),
        ] + side_specs,
        out_specs=[
            pl.BlockSpec((tm, half), out_rows(0)),
            pl.BlockSpec((tm, half), out_rows(0)),
            pl.BlockSpec((tm, half), out_rows(1)),
            pl.BlockSpec((tm, half), out_rows(1)),
            pl.BlockSpec((tm, LANES), out_rows(1)),
            pl.BlockSpec((tm, LANES), out_rows(1)),
            pl.BlockSpec((tm, tn), out_rows(2)),
            pl.BlockSpec((tm, tn), out_rows(3)),
        ] + side_specs,
        out_shape=[wide(half), wide(half), wide(half), wide(half), wide(LANES), wide(LANES),
                   wide(tn), wide(tn)] + _bf16_like(side),
        scratch_shapes=[pltpu.VMEM((group, tm, d), BF16), pltpu.VMEM((tm, LANES), F32)],
        compiler_params=pltpu.CompilerParams(
            dimension_semantics=("arbitrary", "arbitrary", "arbitrary"),
            vmem_limit_bytes=VMEM_LIMIT),
        name="in_projection",
    )(x2d, norm_w.reshape(1, d), shift, scale, w_main, w_kv, *tables, *side)
    return results[:8], results[8:]


def _rope_tables(seq):
    pos = np.arange(seq, dtype=np.float64)[:, None]

    def cos_sin(half):
        inv = ROPE_THETA ** (-np.arange(half, dtype=np.float64) / half)
        ang = pos * inv[None, :]
        return np.cos(ang), np.sin(ang)

    cos, sin = cos_sin(MOBA_HEAD_DIM // 2)
    cos_a = np.concatenate([cos, cos], axis=1)
    sin_a = np.concatenate([-sin, sin], axis=1)
    cos, sin = cos_sin(SWA_HEAD_DIM // 2)
    zero = np.zeros_like(sin)
    cos_b = np.concatenate([cos, cos, cos, cos], axis=1)
    sin_lo = np.concatenate([-sin, zero, -sin, zero], axis=1)
    sin_hi = np.concatenate([zero, sin, zero, sin], axis=1)
    return tuple(jnp.asarray(t, dtype=F32) for t in (cos_a, sin_a, cos_b, sin_lo, sin_hi))


def _moba_kernel(*refs, topk, group, n_side):
    q_ref, k_ref, v_ref = refs[:3]
    o_ref = refs[3 + n_side]
    kaug_scr, vt_scr, kmean_scr, s_scr = refs[4 + 2 * n_side:]
    _side_cast(refs[3:3 + n_side], refs[4 + n_side:4 + 2 * n_side])

    i = pl.program_id(2)
    blk = q_ref.shape[0]
    heads, nb, dh, _ = vt_scr.shape
    exp2_scale = dh ** -0.5 * LOG2_E
    head_lanes = lambda h: slice(h * dh, (h + 1) * dh)

    @pl.when(i == 0)
    def _():
        col = lax.broadcasted_iota(jnp.int32, (blk, LANES), 1)
        for h in range(heads):
            for j in range(nb):
                k = k_ref[j * blk:(j + 1) * blk, head_lanes(h)]
                kmean_scr[h, j:j + 1, :] = jnp.mean(k.astype(F32), axis=0, keepdims=True)
                kaug_scr[h, j, :, :dh] = k
                kaug_scr[h, j, :, dh:] = jnp.where(col == j, 1.0, 0.0).astype(BF16)
                v = v_ref[j * blk:(j + 1) * blk, head_lanes(h)]
                vt_scr[h, j] = v.astype(F32).T.astype(BF16)

    def gate(h):
        q = q_ref[:, head_lanes(h)]
        gs = _dot_nt(kmean_scr[h].astype(BF16), q)
        blk_id = lax.broadcasted_iota(jnp.int32, gs.shape, 0)
        valid = blk_id < i
        vals = jnp.where(valid, gs, -jnp.inf)
        sel = jnp.zeros(gs.shape, dtype=jnp.bool_)
        for _ in range(topk):
            best = jnp.max(vals, axis=0, keepdims=True)
            first = jnp.min(jnp.where(vals == best, blk_id, nb), axis=0, keepdims=True)
            pick = blk_id == first
            sel = sel | pick
            vals = jnp.where(pick, -jnp.inf, vals)
        sel = sel & valid
        bias_t = jnp.where(sel, 0.0, MASK_VALUE).astype(F32)
        bias_t = jnp.concatenate([bias_t, jnp.zeros((LANES - nb, blk), F32)], axis=0)
        q_t = q.astype(F32).T.astype(BF16)
        return jnp.concatenate([q_t, bias_t.astype(BF16)], axis=0)

    def attend(n_past):
        q_aug_t = [gate(h) for h in range(heads)]
        own_slot = n_past
        r = lax.broadcasted_iota(jnp.int32, (blk, blk), 0)
        c = lax.broadcasted_iota(jnp.int32, (blk, blk), 1)
        m = []
        for h in range(heads):
            s = _dot(kaug_scr[h, i, :, :dh], q_aug_t[h][:dh]) * exp2_scale
            s = jnp.where(r <= c, s, MASK_VALUE)
            s_scr[h, own_slot] = s
            mh = jnp.max(s, axis=0, keepdims=True)
            for j in range(n_past):
                s = _dot(kaug_scr[h, j], q_aug_t[h]) * exp2_scale
                s_scr[h, j] = s
                mh = jnp.maximum(mh, jnp.max(s, axis=0, keepdims=True))
            m.append(mh)

        for h in range(heads):
            l = jnp.zeros_like(m[h])
            acc = jnp.zeros((dh, blk), F32)
            for slot in range(n_past + 1):
                p = jnp.exp2(s_scr[h, slot] - m[h])
                l = l + jnp.sum(p, axis=0, keepdims=True)
                vt = vt_scr[h, i] if slot == own_slot else vt_scr[h, slot]
                acc = acc + _dot(vt, p.astype(BF16))
            o_ref[:, head_lanes(h)] = (acc / l).T.astype(o_ref.dtype)

    n_groups = (i + group - 1) // group
    for n in range(nb // group + 1):
        pl.when(n_groups == n)(functools.partial(attend, n * group))


def _moba_attention(q, k, v, side, *, heads):
    b, seq, _ = q.shape
    dh, blk = MOBA_HEAD_DIM, MOBA_BLOCK
    nb = seq // blk
    hps = MOBA_HEADS_PER_STEP
    grid = (b, heads // hps, nb)
    q_spec = pl.BlockSpec((None, blk, hps * dh), lambda bi, h, i: (bi, i, h))
    kv_spec = pl.BlockSpec((None, seq, hps * dh), lambda bi, h, i: (bi, 0, h))
    side_specs = _side_cast_specs(side, grid[0] * grid[1] * grid[2],
                                  lambda bi, h, i: (bi * grid[1] + h) * grid[2] + i)
    out, *side_bf16 = pl.pallas_call(
        functools.partial(_moba_kernel, topk=min(MOBA_TOPK, nb), group=MOBA_LOOP_GROUP,
                          n_side=len(side)),
        grid=grid,
        in_specs=[q_spec, kv_spec, kv_spec] + side_specs,
        out_specs=[q_spec] + side_specs,
        out_shape=[jax.ShapeDtypeStruct(q.shape, BF16)] + _bf16_like(side),
        scratch_shapes=[pltpu.VMEM((hps, nb, blk, dh + LANES), BF16),
                        pltpu.VMEM((hps, nb, dh, blk), BF16),
                        pltpu.VMEM((hps, nb, dh), F32),
                        pltpu.VMEM((hps, nb + 1, blk, blk), F32)],
        compiler_params=pltpu.CompilerParams(
            dimension_semantics=("arbitrary", "arbitrary", "arbitrary"),
            vmem_limit_bytes=VMEM_LIMIT),
        name="moba_attention",
    )(q, k, v, *side)
    return out, side_bf16


def _swa_kernel(*refs, q_lane_blocks, n_side):
    sink_ref, q_ref, kp_ref, kc_ref, vp_ref, vc_ref = refs[:6]
    o_ref = refs[6 + n_side]
    k_scr, vt_scr, s_scr, qt_scr, ot_scr = refs[7 + 2 * n_side:]
    _side_cast(refs[6:6 + n_side], refs[7 + n_side:7 + 2 * n_side])

    i = pl.program_id(1)
    w = kp_ref.shape[0]
    tq = q_ref.shape[0]
    sub_blocks = tq // w
    half = LANES // 2
    scale = half ** -0.5
    groups = q_lane_blocks // SWA_KV_HEADS
    lanes = lambda jb: slice(jb * LANES, (jb + 1) * LANES)
    zeros = jnp.zeros((half, 2 * w), BF16)

    k_scr[:w, :] = kp_ref[...]
    k_scr[w:, :] = kc_ref[...]
    transposed = lambda x: x.astype(F32).T.astype(BF16)
    vt_scr[0] = transposed(vp_ref[...])
    for c in range(sub_blocks):
        vt_scr[c + 1] = transposed(vc_ref[c * w:(c + 1) * w, :])
    for u in range(sub_blocks):
        for jb in range(q_lane_blocks):
            qt = (q_ref[u * w:(u + 1) * w, lanes(jb)].astype(F32) * scale).T.astype(BF16)
            q2 = jnp.concatenate([qt[:half], qt[half:]], axis=1)
            qt_scr[u * q_lane_blocks + jb] = jnp.concatenate(
                [q2, zeros] if jb < groups else [zeros, q2], axis=0)

    key = lax.broadcasted_iota(jnp.int32, (w, 2 * w), 0)
    col = lax.broadcasted_iota(jnp.int32, (w, 2 * w), 1)
    from_prev = key > jnp.where(col >= w, col - w, col)
    odd_head = lax.broadcasted_iota(jnp.int32, (1, 2 * w), 1) >= w

    def sub_block(u):
        prev_bias = jnp.where((i * sub_blocks + u) == 0, MASK_VALUE, 0.0)
        k2 = k_scr[u * w:(u + 2) * w, :]
        vt2 = jnp.concatenate([vt_scr[u], vt_scr[u + 1]], axis=1)

        sinks, maxima = [], []
        for jb in range(q_lane_blocks):
            s = _dot(k2, qt_scr[u * q_lane_blocks + jb])
            s = jnp.where(from_prev, s[:w] + prev_bias, s[w:])
            s_scr[u * q_lane_blocks + jb] = s
            sink = jnp.where(odd_head, sink_ref[2 * jb + 1], sink_ref[2 * jb])
            sinks.append(sink)
            maxima.append(jnp.maximum(jnp.max(s, axis=0, keepdims=True), sink))

        for jb in range(q_lane_blocks):
            kvh = jb // groups
            p = jnp.exp(s_scr[u * q_lane_blocks + jb] - maxima[jb])
            den = jnp.sum(p, axis=0, keepdims=True) + jnp.exp(sinks[jb] - maxima[jb])
            p2 = jnp.concatenate([jnp.where(from_prev, p, 0.0), jnp.where(from_prev, 0.0, p)],
                                 axis=0).astype(BF16)
            o = _dot(vt2[kvh * half:(kvh + 1) * half], p2) / den
            ot_scr[u * q_lane_blocks + jb] = jnp.concatenate([o[:, :w], o[:, w:]], axis=0)

    for u in range(sub_blocks):
        sub_block(u)

    for u in range(sub_blocks):
        for jb in range(q_lane_blocks):
            o_ref[u * w:(u + 1) * w, lanes(jb)] = ot_scr[u * q_lane_blocks + jb].T.astype(
                o_ref.dtype)


def _swa_attention(q, k, v, sinks, side):
    b, seq, qw = q.shape
    w = SWA_WINDOW
    tq = 512
    sub_blocks = tq // w
    grid = (b, seq // tq)
    q_spec = pl.BlockSpec((None, tq, qw), lambda bi, i: (bi, i, 0))
    cur = pl.BlockSpec((None, tq, LANES), lambda bi, i: (bi, i, 0))
    prev = pl.BlockSpec((None, w, LANES), lambda bi, i: (bi, jnp.maximum(i * sub_blocks - 1, 0), 0))
    side_specs = _side_cast_specs(side, grid[0] * grid[1], lambda bi, i: bi * grid[1] + i)
    out, *side_bf16 = pl.pallas_call(
        functools.partial(_swa_kernel, q_lane_blocks=qw // LANES, n_side=len(side)),
        grid=grid,
        in_specs=[pl.BlockSpec(memory_space=pltpu.SMEM), q_spec, prev, cur, prev, cur] + side_specs,
        out_specs=[q_spec] + side_specs,
        out_shape=[jax.ShapeDtypeStruct(q.shape, BF16)] + _bf16_like(side),
        scratch_shapes=[pltpu.VMEM((tq + w, LANES), BF16),
                        pltpu.VMEM((sub_blocks + 1, LANES, w), BF16),
                        pltpu.VMEM((sub_blocks * (qw // LANES), w, 2 * w), F32),
                        pltpu.VMEM((sub_blocks * (qw // LANES), LANES, 2 * w), BF16),
                        pltpu.VMEM((sub_blocks * (qw // LANES), LANES, w), F32)],
        compiler_params=pltpu.CompilerParams(
            dimension_semantics=("arbitrary", "arbitrary"), vmem_limit_bytes=VMEM_LIMIT),
        name="swa_attention",
    )(sinks, q, k, k, v, v, *side)
    return out, side_bf16


def _merge_kernel(ya_ref, yb_ref, ga_ref, gb_ref, x_ref, g_ref, wa_ref, wb_ref, wo_ref, o_ref):
    merged = (_sigmoid(ga_ref[...].astype(F32)) * _dot(ya_ref[...], wa_ref[...])
              + _sigmoid(gb_ref[...].astype(F32)) * _dot(yb_ref[...], wb_ref[...]))
    o_ref[...] = x_ref[...] + g_ref[...] * _dot(merged.astype(BF16), wo_ref[...])


def _merge(ya, yb, ga, gb, x2d, gate, wa, wb, wo, *, seq):
    m, d = x2d.shape
    tm = 512
    tiles_per_seq = seq // tm
    row = lambda i: (i, 0)
    const = lambda i: (0, 0)
    resident = lambda shape: pl.BlockSpec(shape, const, pipeline_mode=pl.Buffered(1))
    return pl.pallas_call(
        _merge_kernel,
        grid=(m // tm,),
        in_specs=[
            pl.BlockSpec((tm, ya.shape[1]), row),
            pl.BlockSpec((tm, yb.shape[1]), row),
            pl.BlockSpec((tm, d), row),
            pl.BlockSpec((tm, d), row),
            pl.BlockSpec((tm, d), row),
            pl.BlockSpec((None, 1, d), lambda i: (i // tiles_per_seq, 0, 0)),
            resident(wa.shape), resident(wb.shape), resident(wo.shape),
        ],
        out_specs=pl.BlockSpec((tm, d), row),
        out_shape=jax.ShapeDtypeStruct((m, d), F32),
        compiler_params=pltpu.CompilerParams(
            dimension_semantics=("arbitrary",), vmem_limit_bytes=VMEM_LIMIT),
        name="merge_out_projection",
    )(ya, yb, ga, gb, x2d, gate, wa, wb, wo)


def kernel(x, c, w_ada, b_ada, norm_ffn1, ffn1_gate, ffn1_up, ffn1_down, norm_mix, w_in, swa_sinks,
           w_branch_moba, w_branch_swa, w_out, norm_ffn2, ffn2_gate, ffn2_up, ffn2_down, norm_final):
    b, seq, d = x.shape
    depth = w_ada.shape[0]
    moba_w = w_branch_moba.shape[1]
    swa_qw = w_branch_swa.shape[1]
    kv_w = SWA_KV_HEADS * SWA_HEAD_DIM
    qkv_cols = 3 * moba_w + swa_qw
    assert w_in.shape[2] == qkv_cols + 2 * kv_w + 2 * d
    assert moba_w == swa_qw == 1024 and d == 2048 and kv_w == LANES

    tables = _rope_tables(seq)
    c_pad = jnp.pad(c, ((0, 8 - b), (0, 0)))
    x2d = x.reshape(b * seq, d)

    for l in range(depth):
        mod = _ada_modulation(c_pad, w_ada[l], b_ada[l])[:b].reshape(b, N_ADA, d)
        sh2, sc2, g2 = [mod[:, t:t + 1] for t in (3, 4, 5)]
        gains = jnp.stack([norm_ffn1[l], norm_mix[l], norm_ffn2[l], norm_final])
        vec = jnp.concatenate(
            [mod, jnp.broadcast_to(gains, (b,) + gains.shape),
             jnp.zeros((b, VEC_ROWS - N_ADA - gains.shape[0], d), F32)], axis=1)

        gate1, up1 = _cast_bf16([ffn1_gate[l], ffn1_up[l]], block_rows=256)
        down1, = _cast_bf16([ffn1_down[l]], block_rows=512)
        x2d, w_main, w_kv = _ffn(x2d, vec, gate1, up1, down1, seq=seq,
                                 norm_row=VEC_NORM_FFN1_ROW, ada_row=0, final_norm=False,
                                 split=(w_in[l], qkv_cols, 2 * kv_w))

        (qa, ka, va, qb, kb, vb, ga, gb), (gate2, up2, down2) = _in_projection(
            x2d, norm_mix[l], sh2, sc2, w_main, w_kv, tables,
            [ffn2_gate[l], ffn2_up[l], ffn2_down[l]], seq=seq)

        rs = lambda t: t.reshape(b, seq, t.shape[-1])
        ya, _ = _moba_attention(rs(qa), rs(ka), rs(va), [], heads=moba_w // MOBA_HEAD_DIM)
        yb, (wa, wb, wo) = _swa_attention(
            rs(qb), rs(kb), rs(vb), swa_sinks[l], [w_branch_moba[l], w_branch_swa[l], w_out[l]])

        x2d = _merge(ya.reshape(b * seq, moba_w), yb.reshape(b * seq, swa_qw), ga, gb, x2d, g2,
                     wa, wb, wo, seq=seq)

        x2d = _ffn(x2d, vec, gate2, up2, down2, seq=seq, norm_row=VEC_NORM_FFN2_ROW, ada_row=6,
                   final_norm=(l == depth - 1))

    return x2d.reshape(b, seq, d)
```

```python
import functools

import jax
import jax.numpy as jnp
import numpy as np
from jax import lax
from jax.experimental import pallas as pl
from jax.experimental.pallas import tpu as pltpu

MOBA_HEAD_DIM = 128
MOBA_BLOCK = 256
MOBA_TOPK = 3
SWA_HEAD_DIM = 64
SWA_KV_HEADS = 2
SWA_WINDOW = 128
ROPE_THETA = 10000.0
EPS = 1e-6
N_ADA = 9
VEC_NORM_FFN1_ROW, VEC_NORM_MIX_ROW, VEC_NORM_FFN2_ROW, VEC_FINAL_NORM_ROW = 9, 10, 11, 12
VEC_ROWS = 16

LANES = 128
BF16_SUBLANES = 16
ROW_CHUNK = 16
VMEM_LIMIT = 56 * 1024 * 1024
MASK_VALUE = -1e30
LOG2_E = 1.4426950408889634
MOBA_LOOP_GROUP = 1
MOBA_HEADS_PER_STEP = 4

F32 = jnp.float32
BF16 = jnp.bfloat16


def _sigmoid(x):
    return 1.0 / (1.0 + jnp.exp(-x))


def _row_rsqrt_mean_square(x):
    r = lax.rsqrt(jnp.mean(x * x, axis=-1, keepdims=True) + EPS)
    return jnp.broadcast_to(r, (x.shape[0], LANES))


def _across_lanes(r, width):
    return jnp.concatenate([r] * (width // LANES), axis=1)


def _for_row_chunks(rows, body):
    for c in range(rows // ROW_CHUNK):
        body(pl.ds(c * ROW_CHUNK, ROW_CHUNK))


def _rms_modulate_to(h_ref, x_ref, rinv_scr, norm_w, shift, scale):
    rinv_scr[...] = _row_rsqrt_mean_square(x_ref[...])
    gain = norm_w * (1.0 + scale)

    def slab(rows):
        rinv = _across_lanes(rinv_scr[rows, :], x_ref.shape[1])
        h_ref[rows, :] = (x_ref[rows, :] * rinv * gain + shift).astype(BF16)

    _for_row_chunks(x_ref.shape[0], slab)


def _dot(a, b):
    return jnp.dot(a, b, preferred_element_type=F32)


def _dot_nt(a, b):
    return lax.dot_general(a, b, (((1,), (1,)), ((), ())), preferred_element_type=F32)


def _cast_kernel(*refs):
    n = len(refs) // 2
    for src, dst in zip(refs[:n], refs[n:]):
        dst[...] = src[...].astype(dst.dtype)


def _cast_bf16(arrays, block_rows):
    rows, cols = arrays[0].shape
    spec = pl.BlockSpec((block_rows, cols), lambda r: (r, 0))
    n = len(arrays)
    return pl.pallas_call(
        _cast_kernel,
        grid=(rows // block_rows,),
        in_specs=[spec] * n,
        out_specs=[spec] * n,
        out_shape=[jax.ShapeDtypeStruct((rows, cols), BF16)] * n,
        compiler_params=pltpu.CompilerParams(
            dimension_semantics=("arbitrary",), vmem_limit_bytes=VMEM_LIMIT),
        name="cast_bf16",
    )(*arrays)


def _side_cast_specs(arrays, steps, step_index):
    specs = []
    for a in arrays:
        rows, cols = a.shape
        for col_blocks in (1, 2, 4, 8, 16):
            row_blocks = steps // col_blocks
            if rows % (row_blocks * BF16_SUBLANES) == 0 and cols % (col_blocks * LANES) == 0:
                break
        else:
            raise ValueError(f"no {steps}-block tiling for {a.shape}")
        index = lambda *g, cb=col_blocks: (step_index(*g) // cb, step_index(*g) % cb)
        specs.append(pl.BlockSpec((rows // row_blocks, cols // col_blocks), index))
    return specs


def _side_cast(src_refs, dst_refs):
    for src, dst in zip(src_refs, dst_refs):
        dst[...] = src[...].astype(dst.dtype)


def _bf16_like(arrays):
    return [jax.ShapeDtypeStruct(a.shape, BF16) for a in arrays]


def _ada_kernel(c_ref, w_ref, b_ref, o_ref):
    c = c_ref[...]
    s = (c * _sigmoid(c)).astype(BF16)
    o_ref[...] = _dot(s, w_ref[...].astype(BF16)) + b_ref[...]


def _ada_modulation(c_pad, w, b):
    rows, d = c_pad.shape
    n = w.shape[1]
    tn = 1024
    return pl.pallas_call(
        _ada_kernel,
        grid=(n // tn,),
        in_specs=[
            pl.BlockSpec((rows, d), lambda j: (0, 0)),
            pl.BlockSpec((d, tn), lambda j: (0, j)),
            pl.BlockSpec((1, tn), lambda j: (0, j)),
        ],
        out_specs=pl.BlockSpec((rows, tn), lambda j: (0, j)),
        out_shape=jax.ShapeDtypeStruct((rows, n), F32),
        compiler_params=pltpu.CompilerParams(
            dimension_semantics=("arbitrary",), vmem_limit_bytes=VMEM_LIMIT),
        name="ada_modulation",
    )(c_pad, w, b.reshape(1, n))


def _ffn_kernel(*refs, final_norm, tiles_per_seq, norm_row, ada_row, split_at):
    if split_at is None:
        x_ref, vec_ref, wg_ref, wu_ref, wd_ref, o_ref, h_scr, acc_scr, rinv_scr = refs
    else:
        (x_ref, vec_ref, wg_ref, wu_ref, wd_ref, side_ref,
         o_ref, main_ref, cut_ref, h_scr, acc_scr, rinv_scr) = refs
    f = pl.program_id(1)
    nf = pl.num_programs(1)
    batch = pl.program_id(0) // tiles_per_seq
    vec = lambda r: vec_ref[batch, r:r + 1, :]

    def down_projection():
        h = h_scr[...]
        gate = _dot(h, wg_ref[...])
        up = _dot(h, wu_ref[...])
        act = (gate * _sigmoid(gate) * up).astype(BF16)
        return _dot(act, wd_ref[...])

    @pl.when(f == 0)
    def _():
        _rms_modulate_to(h_scr, x_ref, rinv_scr, vec(norm_row), vec(ada_row), vec(ada_row + 1))
        acc_scr[...] = down_projection()
        if split_at is not None:
            start, width = split_at
            main_ref[:, :start] = side_ref[:, :start].astype(BF16)
            main_ref[:, start:] = side_ref[:, start + width:].astype(BF16)
            cut_ref[...] = side_ref[:, start:start + width].astype(BF16)

    @pl.when((f > 0) & (f < nf - 1))
    def _():
        acc_scr[...] += down_projection()

    @pl.when(f == nf - 1)
    def _():
        y = x_ref[...] + (0.5 * vec(ada_row + 2)) * (acc_scr[...] + down_projection())
        o_ref[...] = y
        if final_norm:
            rinv_scr[...] = _row_rsqrt_mean_square(y)
            final_w = vec(VEC_FINAL_NORM_ROW)

            def slab(rows):
                rinv = _across_lanes(rinv_scr[rows, :], o_ref.shape[1])
                o_ref[rows, :] = o_ref[rows, :] * rinv * final_w

            _for_row_chunks(o_ref.shape[0], slab)


def _ffn(x2d, vec, wg, wu, wd, *, seq, norm_row, ada_row, final_norm, split=None):
    m, d = x2d.shape
    dff = wg.shape[1]
    tm, tf = 512, 512
    assert dff // tf >= 2
    row = lambda i, f: (i, 0)
    in_specs = [
        pl.BlockSpec((tm, d), row),
        pl.BlockSpec(vec.shape, lambda i, f: (0, 0, 0)),
        pl.BlockSpec((d, tf), lambda i, f: (0, f)),
        pl.BlockSpec((d, tf), lambda i, f: (0, f)),
        pl.BlockSpec((tf, d), lambda i, f: (f, 0)),
    ]
    out_specs = [pl.BlockSpec((tm, d), row)]
    out_shape = [jax.ShapeDtypeStruct((m, d), F32)]
    operands = [x2d, vec, wg, wu, wd]
    if split is not None:
        w, start, width = split
        rows, cols = w.shape
        block_rows = rows // (m // tm)
        assert block_rows % BF16_SUBLANES == 0 and start % LANES == 0 and width % LANES == 0
        in_specs.append(pl.BlockSpec((block_rows, cols), row))
        out_specs += [pl.BlockSpec((block_rows, cols - width), row),
                      pl.BlockSpec((block_rows, width), row)]
        out_shape += [jax.ShapeDtypeStruct((rows, cols - width), BF16),
                      jax.ShapeDtypeStruct((rows, width), BF16)]
        operands.append(w)
    results = pl.pallas_call(
        functools.partial(_ffn_kernel, final_norm=final_norm, tiles_per_seq=seq // tm,
                          norm_row=norm_row, ada_row=ada_row,
                          split_at=None if split is None else split[1:]),
        grid=(m // tm, dff // tf),
        in_specs=in_specs,
        out_specs=out_specs,
        out_shape=out_shape,
        scratch_shapes=[pltpu.VMEM((tm, d), BF16), pltpu.VMEM((tm, d), F32),
                        pltpu.VMEM((tm, LANES), F32)],
        compiler_params=pltpu.CompilerParams(
            dimension_semantics=("arbitrary", "arbitrary"), vmem_limit_bytes=VMEM_LIMIT),
        name="ffn_final" if final_norm else "ffn",
    )(*operands)
    return results[0] if split is None else tuple(results)


def _rope128(x, cos, sin):
    return x * cos + pltpu.roll(x, 64, 1) * sin


def _rope64(x, cos, sin_lo, sin_hi):
    return x * cos + pltpu.roll(x, 96, 1) * sin_lo + pltpu.roll(x, 32, 1) * sin_hi


def _proj_kernel(*refs, n_side):
    (x_ref, nw_ref, sh_ref, sc_ref, w_ref, wkv_ref,
     cos_a_ref, sin_a_ref, cos_b_ref, sin_lo_ref, sin_hi_ref) = refs[:11]
    outs = refs[11 + n_side:19 + n_side]
    qa_ref, ka_ref, va_ref, qb_ref, kb_ref, vb_ref, ga_ref, gb_ref = outs
    h_scr, rinv_scr = refs[19 + 2 * n_side:]
    _side_cast(refs[11:11 + n_side], refs[19 + n_side:19 + 2 * n_side])

    n = pl.program_id(1)
    slot = pl.program_id(2)
    chunk = 2 * LANES
    identity = lambda x: x

    def project(dst_ref, epilogue, col0):
        h = h_scr[slot]
        for s in range(0, dst_ref.shape[1], chunk):
            acc = _dot(h, w_ref[:, col0 + s:col0 + s + chunk])
            for t in range(0, chunk, LANES):
                dst_ref[:, s + t:s + t + LANES] = epilogue(acc[:, t:t + LANES]).astype(BF16)

    rope_a = lambda x: _rope128(x, cos_a_ref[...], sin_a_ref[...])
    rope_b = lambda x: _rope64(x, cos_b_ref[...], sin_lo_ref[...], sin_hi_ref[...])

    @pl.when(n == 0)
    def _():
        _rms_modulate_to(h_scr.at[slot], x_ref, rinv_scr, nw_ref[...], sh_ref[...], sc_ref[...])
        project(qa_ref, rope_a, 0)
        project(ka_ref, rope_a, qa_ref.shape[1])

    @pl.when(n == 1)
    def _():
        project(va_ref, identity, 0)
        project(qb_ref, rope_b, va_ref.shape[1])
        kv = _dot(h_scr[slot], wkv_ref[...])
        kb_ref[...] = rope_b(kv[:, :LANES]).astype(BF16)
        vb_ref[...] = kv[:, LANES:].astype(BF16)

    @pl.when(n == 2)
    def _():
        project(ga_ref, identity, 0)

    @pl.when(n == 3)
    def _():
        project(gb_ref, identity, 0)


def _in_projection(x2d, norm_w, shift, scale, w_main, w_kv, tables, side, *, seq):
    m, d = x2d.shape
    tm, tn, half = 512, 2048, 1024
    n_tiles = w_main.shape[1] // tn
    assert n_tiles == 4 and d == tn
    tiles_per_seq = seq // tm
    group = 2
    assert (m // tm) % group == 0
    tile = lambda g, n, t: g * group + t
    per_batch = lambda g, n, t: (tile(g, n, t) // tiles_per_seq, 0, 0)
    x_rows = lambda g, n, t: (jnp.where(n == 0, tile(g, n, t), g * group + group - 1), 0)

    def out_rows(segment):
        def index(g, n, t):
            first, last = g * group, g * group + group - 1
            return (jnp.where(n == segment, tile(g, n, t), jnp.where(n < segment, first, last)), 0)
        return index

    def table_spec(segment):
        rows = out_rows(segment)
        return pl.BlockSpec((tm, LANES), lambda g, n, t: (rows(g, n, t)[0] % tiles_per_seq, 0))

    wide = lambda cols: jax.ShapeDtypeStruct((m, cols), BF16)
    const2 = lambda g, n, t: (0, 0)
    grid = (m // tm // group, n_tiles, group)
    side_specs = _side_cast_specs(side, grid[0] * grid[1] * grid[2],
                                  lambda g, n, t: (g * grid[1] + n) * grid[2] + t)
    results = pl.pallas_call(
        functools.partial(_proj_kernel, n_side=len(side)),
        grid=grid,
        in_specs=[
            pl.BlockSpec((tm, d), x_rows),
            pl.BlockSpec((1, d), const2),
            pl.BlockSpec((None, 1, d), per_batch),
            pl.BlockSpec((None, 1, d), per_batch),
            pl.BlockSpec((d, tn), lambda g, n, t: (0, n)),
            pl.BlockSpec((d, 2 * LANES), const2),
            table_spec(0), table_spec(0), table_spec(1), table_spec(1), table_spec(1),
        ] + side_specs,
        out_specs=[
            pl.BlockSpec((tm, half), out_rows(0)),
            pl.BlockSpec((tm, half), out_rows(0)),
            pl.BlockSpec((tm, half), out_rows(1)),
            pl.BlockSpec((tm, half), out_rows(1)),
            pl.BlockSpec((tm, LANES), out_rows(1)),
            pl.BlockSpec((tm, LANES), out_rows(1)),
            pl.BlockSpec((tm, tn), out_rows(2)),
            pl.BlockSpec((tm, tn), out_rows(3)),
        ] + side_specs,
        out_shape=[wide(half), wide(half), wide(half), wide(half), wide(LANES), wide(LANES),
                   wide(tn), wide(tn)] + _bf16_like(side),
        scratch_shapes=[pltpu.VMEM((group, tm, d), BF16), pltpu.VMEM((tm, LANES), F32)],
        compiler_params=pltpu.CompilerParams(
            dimension_semantics=("arbitrary", "arbitrary", "arbitrary"),
            vmem_limit_bytes=VMEM_LIMIT),
        name="in_projection",
    )(x2d, norm_w.reshape(1, d), shift, scale, w_main, w_kv, *tables, *side)
    return results[:8], results[8:]


def _rope_tables(seq):
    pos = np.arange(seq, dtype=np.float64)[:, None]

    def cos_sin(half):
        inv = ROPE_THETA ** (-np.arange(half, dtype=np.float64) / half)
        ang = pos * inv[None, :]
        return np.cos(ang), np.sin(ang)

    cos, sin = cos_sin(MOBA_HEAD_DIM // 2)
    cos_a = np.concatenate([cos, cos], axis=1)
    sin_a = np.concatenate([-sin, sin], axis=1)
    cos, sin = cos_sin(SWA_HEAD_DIM // 2)
    zero = np.zeros_like(sin)
    cos_b = np.concatenate([cos, cos, cos, cos], axis=1)
    sin_lo = np.concatenate([-sin, zero, -sin, zero], axis=1)
    sin_hi = np.concatenate([zero, sin, zero, sin], axis=1)
    return tuple(jnp.asarray(t, dtype=F32) for t in (cos_a, sin_a, cos_b, sin_lo, sin_hi))


def _moba_kernel(*refs, topk, group, n_side):
    q_ref, k_ref, v_ref = refs[:3]
    o_ref = refs[3 + n_side]
    kaug_scr, vt_scr, kmean_scr, s_scr = refs[4 + 2 * n_side:]
    _side_cast(refs[3:3 + n_side], refs[4 + n_side:4 + 2 * n_side])

    i = pl.program_id(2)
    blk = q_ref.shape[0]
    heads, nb, dh, _ = vt_scr.shape
    exp2_scale = dh ** -0.5 * LOG2_E
    head_lanes = lambda h: slice(h * dh, (h + 1) * dh)

    @pl.when(i == 0)
    def _():
        col = lax.broadcasted_iota(jnp.int32, (blk, LANES), 1)
        for h in range(heads):
            for j in range(nb):
                k = k_ref[j * blk:(j + 1) * blk, head_lanes(h)]
                kmean_scr[h, j:j + 1, :] = jnp.mean(k.astype(F32), axis=0, keepdims=True)
                kaug_scr[h, j, :, :dh] = k
                kaug_scr[h, j, :, dh:] = jnp.where(col == j, 1.0, 0.0).astype(BF16)
                v = v_ref[j * blk:(j + 1) * blk, head_lanes(h)]
                vt_scr[h, j] = v.astype(F32).T.astype(BF16)

    def gate(h):
        q = q_ref[:, head_lanes(h)]
        gs = _dot_nt(kmean_scr[h].astype(BF16), q)
        blk_id = lax.broadcasted_iota(jnp.int32, gs.shape, 0)
        valid = blk_id < i
        vals = jnp.where(valid, gs, -jnp.inf)
        sel = jnp.zeros(gs.shape, dtype=jnp.bool_)
        for _ in range(topk):
            best = jnp.max(vals, axis=0, keepdims=True)
            first = jnp.min(jnp.where(vals == best, blk_id, nb), axis=0, keepdims=True)
            pick = blk_id == first
            sel = sel | pick
            vals = jnp.where(pick, -jnp.inf, vals)
        sel = sel & valid
        bias_t = jnp.where(sel, 0.0, MASK_VALUE).astype(F32)
        bias_t = jnp.concatenate([bias_t, jnp.zeros((LANES - nb, blk), F32)], axis=0)
        q_t = q.astype(F32).T.astype(BF16)
        return jnp.concatenate([q_t, bias_t.astype(BF16)], axis=0)

    def attend(n_past):
        q_aug_t = [gate(h) for h in range(heads)]
        own_slot = n_past
        r = lax.broadcasted_iota(jnp.int32, (blk, blk), 0)
        c = lax.broadcasted_iota(jnp.int32, (blk, blk), 1)
        m = []
        for h in range(heads):
            s = _dot(kaug_scr[h, i, :, :dh], q_aug_t[h][:dh]) * exp2_scale
            s = jnp.where(r <= c, s, MASK_VALUE)
            s_scr[h, own_slot] = s
            mh = jnp.max(s, axis=0, keepdims=True)
            for j in range(n_past):
                s = _dot(kaug_scr[h, j], q_aug_t[h]) * exp2_scale
                s_scr[h, j] = s
                mh = jnp.maximum(mh, jnp.max(s, axis=0, keepdims=True))
            m.append(mh)

        for h in range(heads):
            l = jnp.zeros_like(m[h])
            acc = jnp.zeros((dh, blk), F32)
            for slot in range(n_past + 1):
                p = jnp.exp2(s_scr[h, slot] - m[h])
                l = l + jnp.sum(p, axis=0, keepdims=True)
                vt = vt_scr[h, i] if slot == own_slot else vt_scr[h, slot]
                acc = acc + _dot(vt, p.astype(BF16))
            o_ref[:, head_lanes(h)] = (acc / l).T.astype(o_ref.dtype)

    n_groups = (i + group - 1) // group
    for n in range((nb - 1 + group - 1) // group + 1):
        pl.when(n_groups == n)(functools.partial(attend, n * group))


def _moba_attention(q, k, v, side, *, heads):
    b, seq, _ = q.shape
    dh, blk = MOBA_HEAD_DIM, MOBA_BLOCK
    nb = seq // blk
    hps = MOBA_HEADS_PER_STEP
    grid = (b, heads // hps, nb)
    q_spec = pl.BlockSpec((None, blk, hps * dh), lambda bi, h, i: (bi, i, h))
    kv_spec = pl.BlockSpec((None, seq, hps * dh), lambda bi, h, i: (bi, 0, h))
    side_specs = _side_cast_specs(side, grid[0] * grid[1] * grid[2],
                                  lambda bi, h, i: (bi * grid[1] + h) * grid[2] + i)
    out, *side_bf16 = pl.pallas_call(
        functools.partial(_moba_kernel, topk=min(MOBA_TOPK, nb), group=MOBA_LOOP_GROUP,
                          n_side=len(side)),
        grid=grid,
        in_specs=[q_spec, kv_spec, kv_spec] + side_specs,
        out_specs=[q_spec] + side_specs,
        out_shape=[jax.ShapeDtypeStruct(q.shape, BF16)] + _bf16_like(side),
        scratch_shapes=[pltpu.VMEM((hps, nb, blk, dh + LANES), BF16),
                        pltpu.VMEM((hps, nb, dh, blk), BF16),
                        pltpu.VMEM((hps, nb, dh), F32),
                        pltpu.VMEM((hps, nb + 1, blk, blk), F32)],
        compiler_params=pltpu.CompilerParams(
            dimension_semantics=("arbitrary", "arbitrary", "arbitrary"),
            vmem_limit_bytes=VMEM_LIMIT),
        name="moba_attention",
    )(q, k, v, *side)
    return out, side_bf16


def _swa_kernel(*refs, q_lane_blocks, n_side):
    sink_ref, q_ref, kp_ref, kc_ref, vp_ref, vc_ref = refs[:6]
    o_ref = refs[6 + n_side]
    k_scr, vt_scr, s_scr, qt_scr, ot_scr = refs[7 + 2 * n_side:]
    _side_cast(refs[6:6 + n_side], refs[7 + n_side:7 + 2 * n_side])

    i = pl.program_id(1)
    w = kp_ref.shape[0]
    tq = q_ref.shape[0]
    sub_blocks = tq // w
    half = LANES // 2
    scale = half ** -0.5
    groups = q_lane_blocks // SWA_KV_HEADS
    lanes = lambda jb: slice(jb * LANES, (jb + 1) * LANES)
    zeros = jnp.zeros((half, 2 * w), BF16)

    k_scr[:w, :] = kp_ref[...]
    k_scr[w:, :] = kc_ref[...]
    transposed = lambda x: x.astype(F32).T.astype(BF16)
    vt_scr[0] = transposed(vp_ref[...])
    for c in range(sub_blocks):
        vt_scr[c + 1] = transposed(vc_ref[c * w:(c + 1) * w, :])
    for u in range(sub_blocks):
        for jb in range(q_lane_blocks):
            qt = (q_ref[u * w:(u + 1) * w, lanes(jb)].astype(F32) * scale).T.astype(BF16)
            q2 = jnp.concatenate([qt[:half], qt[half:]], axis=1)
            qt_scr[u * q_lane_blocks + jb] = jnp.concatenate(
                [q2, zeros] if jb < groups else [zeros, q2], axis=0)

    key = lax.broadcasted_iota(jnp.int32, (w, 2 * w), 0)
    col = lax.broadcasted_iota(jnp.int32, (w, 2 * w), 1)
    from_prev = key > jnp.where(col >= w, col - w, col)
    odd_head = lax.broadcasted_iota(jnp.int32, (1, 2 * w), 1) >= w

    def sub_block(u):
        prev_bias = jnp.where((i * sub_blocks + u) == 0, MASK_VALUE, 0.0)
        k2 = k_scr[u * w:(u + 2) * w, :]
        vt2 = jnp.concatenate([vt_scr[u], vt_scr[u + 1]], axis=1)

        sinks, maxima = [], []
        for jb in range(q_lane_blocks):
            s = _dot(k2, qt_scr[u * q_lane_blocks + jb])
            s = jnp.where(from_prev, s[:w] + prev_bias, s[w:])
            s_scr[u * q_lane_blocks + jb] = s
            sink = jnp.where(odd_head, sink_ref[2 * jb + 1], sink_ref[2 * jb])
            sinks.append(sink)
            maxima.append(jnp.maximum(jnp.max(s, axis=0, keepdims=True), sink))

        for jb in range(q_lane_blocks):
            kvh = jb // groups
            p = jnp.exp(s_scr[u * q_lane_blocks + jb] - maxima[jb])
            den = jnp.sum(p, axis=0, keepdims=True) + jnp.exp(sinks[jb] - maxima[jb])
            p2 = jnp.concatenate([jnp.where(from_prev, p, 0.0), jnp.where(from_prev, 0.0, p)],
                                 axis=0).astype(BF16)
            o = _dot(vt2[kvh * half:(kvh + 1) * half], p2) / den
            ot_scr[u * q_lane_blocks + jb] = jnp.concatenate([o[:, :w], o[:, w:]], axis=0)

    for u in range(sub_blocks):
        sub_block(u)

    for u in range(sub_blocks):
        for jb in range(q_lane_blocks):
            o_ref[u * w:(u + 1) * w, lanes(jb)] = ot_scr[u * q_lane_blocks + jb].T.astype(
                o_ref.dtype)


def _swa_attention(q, k, v, sinks, side):
    b, seq, qw = q.shape
    w = SWA_WINDOW
    tq = 512
    sub_blocks = tq // w
    grid = (b, seq // tq)
    q_spec = pl.BlockSpec((None, tq, qw), lambda bi, i: (bi, i, 0))
    cur = pl.BlockSpec((None, tq, LANES), lambda bi, i: (bi, i, 0))
    prev = pl.BlockSpec((None, w, LANES), lambda bi, i: (bi, jnp.maximum(i * sub_blocks - 1, 0), 0))
    side_specs = _side_cast_specs(side, grid[0] * grid[1], lambda bi, i: bi * grid[1] + i)
    out, *side_bf16 = pl.pallas_call(
        functools.partial(_swa_kernel, q_lane_blocks=qw // LANES, n_side=len(side)),
        grid=grid,
        in_specs=[pl.BlockSpec(memory_space=pltpu.SMEM), q_spec, prev, cur, prev, cur] + side_specs,
        out_specs=[q_spec] + side_specs,
        out_shape=[jax.ShapeDtypeStruct(q.shape, BF16)] + _bf16_like(side),
        scratch_shapes=[pltpu.VMEM((tq + w, LANES), BF16),
                        pltpu.VMEM((sub_blocks + 1, LANES, w), BF16),
                        pltpu.VMEM((sub_blocks * (qw // LANES), w, 2 * w), F32),
                        pltpu.VMEM((sub_blocks * (qw // LANES), LANES, 2 * w), BF16),
                        pltpu.VMEM((sub_blocks * (qw // LANES), LANES, w), F32)],
        compiler_params=pltpu.CompilerParams(
            dimension_semantics=("arbitrary", "arbitrary"), vmem_limit_bytes=VMEM_LIMIT),
        name="swa_attention",
    )(sinks, q, k, k, v, v, *side)
    return out, side_bf16


def _merge_kernel(ya_ref, yb_ref, ga_ref, gb_ref, x_ref, g_ref, wa_ref, wb_ref, wo_ref, o_ref):
    merged = (_sigmoid(ga_ref[...].astype(F32)) * _dot(ya_ref[...], wa_ref[...])
              + _sigmoid(gb_ref[...].astype(F32)) * _dot(yb_ref[...], wb_ref[...]))
    o_ref[...] = x_ref[...] + g_ref[...] * _dot(merged.astype(BF16), wo_ref[...])


def _merge(ya, yb, ga, gb, x2d, gate, wa, wb, wo, *, seq):
    m, d = x2d.shape
    tm = 512
    tiles_per_seq = seq // tm
    row = lambda i: (i, 0)
    const = lambda i: (0, 0)
    resident = lambda shape: pl.BlockSpec(shape, const, pipeline_mode=pl.Buffered(1))
    return pl.pallas_call(
        _merge_kernel,
        grid=(m // tm,),
        in_specs=[
            pl.BlockSpec((tm, ya.shape[1]), row),
            pl.BlockSpec((tm, yb.shape[1]), row),
            pl.BlockSpec((tm, d), row),
            pl.BlockSpec((tm, d), row),
            pl.BlockSpec((tm, d), row),
            pl.BlockSpec((None, 1, d), lambda i: (i // tiles_per_seq, 0, 0)),
            resident(wa.shape), resident(wb.shape), resident(wo.shape),
        ],
        out_specs=pl.BlockSpec((tm, d), row),
        out_shape=jax.ShapeDtypeStruct((m, d), F32),
        compiler_params=pltpu.CompilerParams(
            dimension_semantics=("arbitrary",), vmem_limit_bytes=VMEM_LIMIT),
        name="merge_out_projection",
    )(ya, yb, ga, gb, x2d, gate, wa, wb, wo)


def kernel(x, c, w_ada, b_ada, norm_ffn1, ffn1_gate, ffn1_up, ffn1_down, norm_mix, w_in, swa_sinks,
           w_branch_moba, w_branch_swa, w_out, norm_ffn2, ffn2_gate, ffn2_up, ffn2_down, norm_final):
    b, seq, d = x.shape
    depth = w_ada.shape[0]
    moba_w = w_branch_moba.shape[1]
    swa_qw = w_branch_swa.shape[1]
    kv_w = SWA_KV_HEADS * SWA_HEAD_DIM
    qkv_cols = 3 * moba_w + swa_qw
    assert w_in.shape[2] == qkv_cols + 2 * kv_w + 2 * d
    assert moba_w == swa_qw == 1024 and d == 2048 and kv_w == LANES

    tables = _rope_tables(seq)
    c_pad = jnp.pad(c, ((0, 8 - b), (0, 0)))
    x2d = x.reshape(b * seq, d)

    for l in range(depth):
        mod = _ada_modulation(c_pad, w_ada[l], b_ada[l])[:b].reshape(b, N_ADA, d)
        sh2, sc2, g2 = [mod[:, t:t + 1] for t in (3, 4, 5)]
        gains = jnp.stack([norm_ffn1[l], norm_mix[l], norm_ffn2[l], norm_final])
        vec = jnp.concatenate(
            [mod, jnp.broadcast_to(gains, (b,) + gains.shape),
             jnp.zeros((b, VEC_ROWS - N_ADA - gains.shape[0], d), F32)], axis=1)

        gate1, up1 = _cast_bf16([ffn1_gate[l], ffn1_up[l]], block_rows=256)
        down1, = _cast_bf16([ffn1_down[l]], block_rows=512)
        x2d, w_main, w_kv = _ffn(x2d, vec, gate1, up1, down1, seq=seq,
                                 norm_row=VEC_NORM_FFN1_ROW, ada_row=0, final_norm=False,
                                 split=(w_in[l], qkv_cols, 2 * kv_w))

        (qa, ka, va, qb, kb, vb, ga, gb), (gate2, up2, down2) = _in_projection(
            x2d, norm_mix[l], sh2, sc2, w_main, w_kv, tables,
            [ffn2_gate[l], ffn2_up[l], ffn2_down[l]], seq=seq)

        rs = lambda t: t.reshape(b, seq, t.shape[-1])
        ya, _ = _moba_attention(rs(qa), rs(ka), rs(va), [], heads=moba_w // MOBA_HEAD_DIM)
        yb, (wa, wb, wo) = _swa_attention(
            rs(qb), rs(kb), rs(vb), swa_sinks[l], [w_branch_moba[l], w_branch_swa[l], w_out[l]])

        x2d = _merge(ya.reshape(b * seq, moba_w), yb.reshape(b * seq, swa_qw), ga, gb, x2d, g2,
                     wa, wb, wo, seq=seq)

        x2d = _ffn(x2d, vec, gate2, up2, down2, seq=seq, norm_row=VEC_NORM_FFN2_ROW, ada_row=6,
                   final_norm=(l == depth - 1))

    return x2d.reshape(b, seq, d)
```
